```python
import math
import jax, jax.numpy as jnp
from jax import lax
import numpy as np

D_MODEL = 2048
BATCH = 4
SEQ = 2048
DEPTH = 1
DEC_BATCH = 128
DEC_SEQ = 8
PAST_LEN = 16384
PAGE_SIZE = 128

DN_HEADS = 16
DN_HEAD_DIM = 128
DN_WIDTH = DN_HEADS * DN_HEAD_DIM
DN_CONV = 4
DN_CHUNK = 64
SC_WIDTH = D_MODEL // 2
SC_CONV = 3
N_EXPERTS = 64
TOP_K = 8
EXPERT_DIM = 512
SHARED_DIM = 512
ROUTED_SCALE = 2.5
EXPERT_BLOCK = 8
DEEPNORM_ALPHA = (2 * DEPTH) ** 0.25
DEEPNORM_BETA = (8 * DEPTH) ** -0.25
LN_EPS = 1e-5
RMS_EPS = 1e-6
IN_SPLITS = (3 * DN_WIDTH, DN_WIDTH, DN_HEADS, DN_HEADS, 3 * SC_WIDTH, D_MODEL, D_MODEL)
IN_COLS = sum(IN_SPLITS)

kernel_name = "hybrid_gdn_shortconv_moe_deepnorm_step"


def _layer_norm(x, g, b):
    xf = x.astype(jnp.float32)
    mu = jnp.mean(xf, -1, keepdims=True)
    var = jnp.mean(jnp.square(xf - mu), -1, keepdims=True)
    return ((xf - mu) * lax.rsqrt(var + LN_EPS) * g.astype(jnp.float32) + b.astype(jnp.float32)).astype(x.dtype)


def _l2norm(t):
    return t * lax.rsqrt(jnp.sum(t * t, -1, keepdims=True) + RMS_EPS)


def _causal_dwconv(x, buf, w):
    width = w.shape[0]
    L = x.shape[1]
    xx = jnp.concatenate([buf.astype(x.dtype), x], axis=1)
    y = sum(xx[:, i:i + L] * w[i] for i in range(width))
    return y, xx[:, -(width - 1):]


def _gated_delta(q, k, v, g, beta, S0):
    B, L, H, dk = q.shape
    dv = v.shape[-1]
    C = math.gcd(DN_CHUNK, L)
    NC = L // C

    def chunks(t):
        return t.reshape((B, NC, C, H) + t.shape[3:]).swapaxes(2, 3)

    qc, kc, vc, bc = chunks(q), chunks(k), chunks(v), chunks(beta)
    gc = jnp.cumsum(chunks(g), axis=-1)
    causal = jnp.tril(jnp.ones((C, C), bool))
    strict = jnp.tril(jnp.ones((C, C), bool), -1)
    decay = jnp.exp(jnp.where(causal, gc[..., :, None] - gc[..., None, :], -jnp.inf))
    kk = jnp.einsum('bnhid,bnhjd->bnhij', kc, kc)
    a_strict = jnp.where(strict, kk * bc[..., :, None] * decay, 0.0)
    eye = jnp.eye(C, dtype=jnp.float32)
    rhs = jnp.concatenate([vc * bc[..., None], kc * (bc * jnp.exp(gc))[..., None]], axis=-1)
    sol = lax.linalg.triangular_solve(eye + a_strict, rhs, left_side=True, lower=True)
    value, k_cumdecay = sol[..., :dv], sol[..., dv:]
    qk = jnp.einsum('bnhid,bnhjd->bnhij', qc, kc) * decay
    q_dec = qc * jnp.exp(gc)[..., None]
    k_dec = kc * jnp.exp(gc[..., -1:] - gc)[..., None]
    g_last = jnp.exp(gc[..., -1])

    def step(S, inp):
        val, kcd, qk_c, qd, kd, gl = inp
        u = val - jnp.einsum('bhck,bhkv->bhcv', kcd, S)
        o = jnp.einsum('bhck,bhkv->bhcv', qd, S) + jnp.einsum('bhij,bhjv->bhiv', qk_c, u)
        S = S * gl[..., None, None] + jnp.einsum('bhck,bhcv->bhkv', kd, u)
        return S, o

    xs = tuple(jnp.moveaxis(t, 1, 0) for t in (value, k_cumdecay, qk, q_dec, k_dec, g_last))
    S_final, o = lax.scan(step, S0, xs)
    o = jnp.moveaxis(o, 0, 1).swapaxes(2, 3).reshape(B, L, H, dv)
    return o, S_final


def _moe(h, w_router, router_bias, w_e_gate, w_e_up, w_e_down, w_s_gate, w_s_up, w_s_down):
    shp = h.shape
    t = h.reshape(-1, shp[-1])
    scores = jax.nn.sigmoid((t @ w_router).astype(jnp.float32))
    _, idx = lax.top_k(scores + router_bias.astype(jnp.float32), TOP_K)
    w = jnp.take_along_axis(scores, idx, axis=-1)
    w = w / jnp.sum(w, -1, keepdims=True) * ROUTED_SCALE
    combine = jnp.sum(jax.nn.one_hot(idx, N_EXPERTS, dtype=jnp.float32) * w[..., None], axis=1).astype(h.dtype)
    y = (jax.nn.silu(t @ w_s_gate) * (t @ w_s_up)) @ w_s_down
    for e0 in range(0, N_EXPERTS, EXPERT_BLOCK):
        sl = slice(e0, e0 + EXPERT_BLOCK)
        gate = jnp.einsum('td,edf->tef', t, w_e_gate[sl])
        up = jnp.einsum('td,edf->tef', t, w_e_up[sl])
        hid = jax.nn.silu(gate) * up * combine[:, sl, None]
        y = y + jnp.einsum('tef,efd->td', hid, w_e_down[sl])
    return y.reshape(shp)


def _block(x, c, dn_S0, dn_buf, sc_buf, p):
    B, L, _ = x.shape
    mod = (jax.nn.silu(c) @ p['w_ada'] + p['b_ada'])[:, None, :]
    shift1, scale1, gate1, shift2, scale2, gate2 = jnp.split(mod, 6, axis=-1)

    h = x * (1.0 + scale1) + shift1
    proj = h @ p['w_in']
    qkv_raw, z, b_raw, a_raw, sc_in, gate_dn, gate_sc = jnp.split(proj, np.cumsum(IN_SPLITS)[:-1].tolist(), axis=-1)

    qkv, dn_buf_new = _causal_dwconv(qkv_raw, dn_buf, p['w_dn_conv'])
    qkv = jax.nn.silu(qkv).astype(jnp.float32)
    q, k, v = [t.reshape(B, L, DN_HEADS, DN_HEAD_DIM) for t in jnp.split(qkv, 3, axis=-1)]
    q = _l2norm(q) * (DN_HEAD_DIM ** -0.5)
    k = _l2norm(k)
    beta = jax.nn.sigmoid(b_raw.astype(jnp.float32))
    g = -jnp.exp(p['dn_a_log'].astype(jnp.float32)) * jax.nn.softplus(a_raw.astype(jnp.float32) + p['dn_dt_bias'].astype(jnp.float32))
    o, S_new = _gated_delta(q, k, v, g, beta, dn_S0.astype(jnp.float32))
    zf = z.astype(jnp.float32).reshape(B, L, DN_HEADS, DN_HEAD_DIM)
    o = o * lax.rsqrt(jnp.mean(o * o, -1, keepdims=True) + RMS_EPS) * p['dn_norm_w'].astype(jnp.float32) * jax.nn.silu(zf)
    y_dn = o.reshape(B, L, DN_WIDTH).astype(x.dtype) @ p['w_dn_out']

    sc_b, sc_c, sc_x = jnp.split(sc_in, 3, axis=-1)
    sc_conv, sc_buf_new = _causal_dwconv(sc_c * sc_x, sc_buf, p['w_sc_conv'])
    y_sc = (sc_b * sc_conv) @ p['w_sc_out']

    merged = jax.nn.sigmoid(gate_dn) * y_dn + jax.nn.sigmoid(gate_sc) * y_sc
    mix = merged @ p['w_o']
    x = _layer_norm(DEEPNORM_ALPHA * x + (1.0 + gate1) * mix, p['ln1_g'], p['ln1_b'])

    h = x * (1.0 + scale2) + shift2
    ffn = _moe(h, p['w_router'], p['router_bias'], p['w_e_gate'], p['w_e_up'], p['w_e_down'],
               p['w_s_gate'], p['w_s_up'], p['w_s_down'])
    x = _layer_norm(DEEPNORM_ALPHA * x + (1.0 + gate2) * ffn, p['ln2_g'], p['ln2_b'])
    return x, S_new.astype(dn_S0.dtype), dn_buf_new, sc_buf_new


def setup_inputs(seed: int = 0) -> dict:
    key = jax.random.key(seed)
    ks = list(jax.random.split(key, 40))

    def nrm(shape, scale=1.0):
        return jax.random.normal(ks.pop(), shape, jnp.float32) * scale

    def unif(shape, lo, hi):
        return jax.random.uniform(ks.pop(), shape, jnp.float32, lo, hi)

    D = D_MODEL
    dt = unif((DN_HEADS,), 0.001, 0.1)
    return {
        'x_prompt': nrm((BATCH, SEQ, D)),
        'x_sample': nrm((DEC_BATCH, DEC_SEQ, D)),
        'state_dn_S': nrm((DEC_BATCH, DN_HEADS, DN_HEAD_DIM, DN_HEAD_DIM), DN_HEAD_DIM ** -0.5),
        'state_dn_conv': nrm((DEC_BATCH, DN_CONV - 1, 3 * DN_WIDTH)),
        'state_sc_conv': nrm((DEC_BATCH, SC_CONV - 1, SC_WIDTH), 0.5),
        'c_prompt': nrm((BATCH, D)),
        'c_sample': nrm((DEC_BATCH, D)),
        'w_ada': nrm((D, 6 * D), 0.1 * D ** -0.5),
        'b_ada': nrm((6 * D,), 0.01),
        'w_in': nrm((D, IN_COLS), D ** -0.5),
        'w_dn_conv': nrm((DN_CONV, 3 * DN_WIDTH), DN_CONV ** -0.5),
        'dn_a_log': jnp.log(unif((DN_HEADS,), 1.0, 16.0)),
        'dn_dt_bias': dt + jnp.log(-jnp.expm1(-dt)),
        'dn_norm_w': 1.0 + nrm((DN_HEAD_DIM,), 0.02),
        'w_sc_conv': nrm((SC_CONV, SC_WIDTH), SC_CONV ** -0.5),
        'w_dn_out': nrm((DN_WIDTH, D), DN_WIDTH ** -0.5),
        'w_sc_out': nrm((SC_WIDTH, D), SC_WIDTH ** -0.5),
        'w_o': nrm((D, D), DEEPNORM_BETA * D ** -0.5),
        'ln1_g': 1.0 + nrm((D,), 0.02),
        'ln1_b': nrm((D,), 0.02),
        'w_router': nrm((D, N_EXPERTS), D ** -0.5),
        'router_bias': nrm((N_EXPERTS,), 0.01),
        'w_e_gate': nrm((N_EXPERTS, D, EXPERT_DIM), D ** -0.5),
        'w_e_up': nrm((N_EXPERTS, D, EXPERT_DIM), D ** -0.5),
        'w_e_down': nrm((N_EXPERTS, EXPERT_DIM, D), DEEPNORM_BETA * EXPERT_DIM ** -0.5),
        'w_s_gate': nrm((D, SHARED_DIM), D ** -0.5),
        'w_s_up': nrm((D, SHARED_DIM), D ** -0.5),
        'w_s_down': nrm((SHARED_DIM, D), DEEPNORM_BETA * SHARED_DIM ** -0.5),
        'ln2_g': 1.0 + nrm((D,), 0.02),
        'ln2_b': nrm((D,), 0.02),
    }


def reference(x_prompt, x_sample, state_dn_S, state_dn_conv, state_sc_conv, c_prompt, c_sample,
              w_ada, b_ada, w_in, w_dn_conv, dn_a_log, dn_dt_bias, dn_norm_w, w_sc_conv, w_dn_out, w_sc_out,
              w_o, ln1_g, ln1_b, w_router, router_bias, w_e_gate, w_e_up, w_e_down, w_s_gate, w_s_up,
              w_s_down, ln2_g, ln2_b):
    p = dict(w_ada=w_ada, b_ada=b_ada, w_in=w_in, w_dn_conv=w_dn_conv, dn_a_log=dn_a_log, dn_dt_bias=dn_dt_bias,
             dn_norm_w=dn_norm_w, w_sc_conv=w_sc_conv, w_dn_out=w_dn_out, w_sc_out=w_sc_out, w_o=w_o,
             ln1_g=ln1_g, ln1_b=ln1_b, w_router=w_router, router_bias=router_bias, w_e_gate=w_e_gate,
             w_e_up=w_e_up, w_e_down=w_e_down, w_s_gate=w_s_gate, w_s_up=w_s_up, w_s_down=w_s_down,
             ln2_g=ln2_g, ln2_b=ln2_b)
    Bp = x_prompt.shape[0]
    dtp = x_prompt.dtype
    y_p, s_p, dnb_p, scb_p = x_prompt, None, None, None
    y_s, s_s, dnb_s, scb_s = x_sample, None, None, None
    for _ in range(DEPTH):
        y_p, s_p, dnb_p, scb_p = _block(
            y_p, c_prompt,
            jnp.zeros((Bp, DN_HEADS, DN_HEAD_DIM, DN_HEAD_DIM), dtp),
            jnp.zeros((Bp, DN_CONV - 1, 3 * DN_WIDTH), dtp),
            jnp.zeros((Bp, SC_CONV - 1, SC_WIDTH), dtp), p)
        y_s, s_s, dnb_s, scb_s = _block(y_s, c_sample, state_dn_S, state_dn_conv, state_sc_conv, p)
    return (y_p, y_s, s_p, dnb_p, scb_p, s_s, dnb_s, scb_s)
```

```python
import functools
import math

import jax
import jax.numpy as jnp
from jax import lax
from jax.experimental import pallas as pl
from jax.experimental.pallas import tpu as pltpu

f32 = jnp.float32
bf16 = jnp.bfloat16
i32 = jnp.int32
u32 = jnp.uint32

TOP_K = 8
ROUTED_SCALE = 2.5
DN_CHUNK = 64
DEPTH = 1
DEEPNORM_ALPHA = (2 * DEPTH) ** 0.25
LN_EPS = 1e-5
RMS_EPS = 1e-6

LANES = 128
SUBLANES = 8
VMEM_LIMIT = 56 * 1024 * 1024


def _cparams(sem, vmem=VMEM_LIMIT):
    return pltpu.CompilerParams(dimension_semantics=sem, vmem_limit_bytes=vmem)


def _silu(x):
    return x * jax.nn.sigmoid(x)


def _bdot(a, b):
    return jnp.dot(a.astype(bf16), b.astype(bf16), preferred_element_type=f32)


def _layer_norm(x, g, b):
    mu = jnp.mean(x, axis=-1, keepdims=True)
    xc = x - mu
    var = jnp.mean(xc * xc, axis=-1, keepdims=True)
    return xc * lax.rsqrt(var + LN_EPS) * g + b


def _pack_pair(a, b):
    ab = pltpu.bitcast(a.astype(bf16).astype(f32), u32)
    bb = pltpu.bitcast(b.astype(bf16).astype(f32), u32)
    return (ab >> 16) | (bb & jnp.uint32(0xFFFF0000))


def _unpack_pair(p):
    lo = pltpu.bitcast(p << 16, f32)
    hi = pltpu.bitcast(p & jnp.uint32(0xFFFF0000), f32)
    return lo, hi


def _ada_body(c_ref, w_ref, b_ref, o_ref):
    s = _silu(c_ref[...])
    o_ref[...] = _bdot(s, w_ref[...]) + b_ref[...]


def _ada(c_all, w_ada, b_ada):
    m, d = c_all.shape
    n = w_ada.shape[1]
    tn = _pick(n, 1024, LANES)
    return pl.pallas_call(
        _ada_body,
        out_shape=jax.ShapeDtypeStruct((m, n), f32),
        grid=(n // tn,),
        in_specs=[pl.BlockSpec((m, d), lambda j: (0, 0)),
                  pl.BlockSpec((d, tn), lambda j: (0, j)),
                  pl.BlockSpec((1, tn), lambda j: (0, j))],
        out_specs=pl.BlockSpec((m, tn), lambda j: (0, j)),
        compiler_params=_cparams(("arbitrary",)),
        name="ada",
    )(c_all, w_ada, b_ada.reshape(1, n))


def _modmm_body(x_ref, sc_ref, sh_ref, w_ref, o_ref, h_scr):
    @pl.when(pl.program_id(1) == 0)
    def _():
        h_scr[...] = (x_ref[...] * (1.0 + sc_ref[...]) + sh_ref[...]).astype(bf16)

    o_ref[...] = jnp.dot(h_scr[...], w_ref[...], preferred_element_type=f32).astype(o_ref.dtype)


def _mod_specs(mod, tm, seq_len, cols, d):
    if mod.ndim == 3:
        tiles_per_seq = seq_len // tm
        return [pl.BlockSpec((None, 1, d), lambda m, *_, c=c: (m // tiles_per_seq, 0, c)) for c in cols]
    return [pl.BlockSpec((tm, d), lambda m, *_, c=c: (m, c)) for c in cols]


def _modmm(x, mod, seq_len, w, out_dtype, tm, tn):
    t, d = x.shape
    n = w.shape[1]
    tn = _pick(n, tn, LANES)
    sc_spec, sh_spec = _mod_specs(mod, tm, seq_len, (1, 0), d)
    return pl.pallas_call(
        _modmm_body,
        out_shape=jax.ShapeDtypeStruct((t, n), out_dtype),
        grid=(t // tm, n // tn),
        in_specs=[pl.BlockSpec((tm, d), lambda m, j: (m, 0)), sc_spec, sh_spec,
                  pl.BlockSpec((d, tn), lambda m, j: (0, j))],
        out_specs=pl.BlockSpec((tm, tn), lambda m, j: (m, j)),
        scratch_shapes=[pltpu.VMEM((tm, d), bf16)],
        compiler_params=_cparams(("arbitrary", "arbitrary")),
        name="modmm",
    )(x, mod, mod, w)


def _cumsum_rows(x, n):
    row = lax.broadcasted_iota(i32, x.shape, 0)
    s = 1
    while s < n:
        x = x + jnp.where(row >= s, pltpu.roll(x, s, axis=0), 0.0)
        s *= 2
    return x


def _conv_silu(x, prev, w):
    c = x.shape[0]
    taps = w.shape[0]
    xc = jnp.concatenate([prev, x], axis=0)
    y = w[taps - 1:taps, :] * x
    for i in range(taps - 1):
        back = taps - 1 - i
        y = y + w[i:i + 1, :] * pltpu.roll(xc, back, axis=0)[SUBLANES:SUBLANES + c, :]
    return _silu(y)


def _unit_lower_inverse(a, eye):
    c = a.shape[0]
    p = -a
    y = eye + p
    n = 1
    while 2 * n < c:
        p = _bdot(p, p)
        y = y + _bdot(p, y)
        n *= 2
    return y


def _dn_body(q_ref, k_ref, v_ref, z_ref, ba_ref, par_ref, cwq_ref, cwk_ref, cwv_ref, nw_ref,
             hq_ref, hk_ref, hv_ref, s0_ref,
             o_ref, so_ref, nbq_ref, nbk_ref, nbv_ref,
             s_scr, pq, pk, pv, *, chunk, heads, n_chunks):
    t = pl.program_id(2)
    dk = LANES

    @pl.when(t == 0)
    def _init():
        s_scr[...] = s0_ref[...]
        pq[...] = hq_ref[...]
        pk[...] = hk_ref[...]
        pv[...] = hv_ref[...]

    neg_a = -jnp.exp(par_ref[0:1, :])
    dt_b = par_ref[1:2, :]
    nw = nw_ref[...]
    ii = lax.broadcasted_iota(i32, (chunk, chunk), 0)
    jj = lax.broadcasted_iota(i32, (chunk, chunk), 1)
    causal = ii >= jj
    strict = ii > jj
    eye = (ii == jj).astype(f32)
    zpad = jnp.zeros((LANES - chunk, LANES), f32)

    def do_chunk(ci):
        r0 = pl.multiple_of(ci * chunk, chunk)
        rows = pl.ds(r0, chunk)
        ba = ba_ref[rows, :]
        beta_all = jax.nn.sigmoid(ba)
        xs = ba + dt_b
        softplus = jnp.maximum(xs, 0.0) + jnp.log1p(jnp.exp(-jnp.abs(xs)))
        gc_all = _cumsum_rows(neg_a * softplus, chunk)
        gc_t = jnp.concatenate([gc_all, zpad], axis=0).T
        eg_all = jnp.exp(gc_all)

        xq = q_ref[rows, :].astype(f32)
        xk = k_ref[rows, :].astype(f32)
        xv = v_ref[rows, :].astype(f32)
        qc = _conv_silu(xq, pq[...], cwq_ref[...])
        kc = _conv_silu(xk, pk[...], cwk_ref[...])
        vc = _conv_silu(xv, pv[...], cwv_ref[...])
        pq[...] = xq[chunk - SUBLANES:, :]
        pk[...] = xk[chunk - SUBLANES:, :]
        pv[...] = xv[chunk - SUBLANES:, :]

        for h in range(heads):
            sl = slice(h * dk, (h + 1) * dk)
            qh = qc[:, sl]
            kh = kc[:, sl]
            vh = vc[:, sl]
            qh = qh * (lax.rsqrt(jnp.sum(qh * qh, axis=-1, keepdims=True) + RMS_EPS) * (dk ** -0.5))
            kh = kh * lax.rsqrt(jnp.sum(kh * kh, axis=-1, keepdims=True) + RMS_EPS)
            beta = beta_all[:, h:h + 1]
            gc = gc_all[:, heads + h:heads + h + 1]
            eg = eg_all[:, heads + h:heads + h + 1]
            gr = gc_t[heads + h:heads + h + 1, 0:chunk]
            diff = gc - gr
            decay = jnp.where(causal, jnp.exp(jnp.where(causal, diff, 0.0)), 0.0)

            qk_kk = lax.dot_general(jnp.concatenate([qh, kh], axis=0).astype(bf16), kh.astype(bf16),
                                    (((1,), (1,)), ((), ())), preferred_element_type=f32)
            qk = qk_kk[:chunk, :] * decay
            a_strict = jnp.where(strict, qk_kk[chunk:, :] * beta * decay, 0.0)
            tinv = _unit_lower_inverse(a_strict, eye)
            rhs = jnp.concatenate([vh * beta, kh * (beta * eg)], axis=1)
            sol = _bdot(tinv, rhs)
            value = sol[:, :dk]
            kcd = sol[:, dk:]

            s_old = s_scr[h]
            ks_qs = _bdot(jnp.concatenate([kcd, qh * eg], axis=0), s_old)
            u = value - ks_qs[:chunk, :]
            o = ks_qs[chunk:, :] + _bdot(qk, u)
            gc_last = gc[chunk - 1:chunk, :]
            kd = kh * jnp.exp(gc_last - gc)
            s_new = s_old * jnp.exp(gc_last) + lax.dot_general(
                kd.astype(bf16), u.astype(bf16), (((0,), (0,)), ((), ())), preferred_element_type=f32)
            s_scr[h] = s_new

            zh = z_ref[rows, sl].astype(f32)
            on = o * lax.rsqrt(jnp.mean(o * o, axis=-1, keepdims=True) + RMS_EPS) * nw * _silu(zh)
            o_ref[rows, sl] = on.astype(o_ref.dtype)

    if n_chunks == 1:
        do_chunk(0)
    else:
        def loop_body(ci, carry):
            do_chunk(ci)
            return carry
        lax.fori_loop(0, n_chunks, loop_body, 0)

    @pl.when(t == pl.num_programs(2) - 1)
    def _fin():
        so_ref[...] = s_scr[...]
        nbq_ref[...] = pq[...]
        nbk_ref[...] = pk[...]
        nbv_ref[...] = pv[...]


def _deltanet(proj3, ba3, par, w_conv, norm_w, hist8, s0, *, n_heads, heads_per_step, chunk, lt):
    bn, seq, _ = proj3.shape
    dk = LANES
    dnw = n_heads * dk
    hg = heads_per_step
    gw = hg * dk
    n_g = n_heads // hg
    n_t = seq // lt
    kern = functools.partial(_dn_body, chunk=chunk, heads=hg, n_chunks=lt // chunk)

    def col(off):
        return pl.BlockSpec((None, lt, gw), lambda b, g, t, off=off: (b, t, off * n_g + g))

    def cw(off):
        return pl.BlockSpec((w_conv.shape[0], gw), lambda b, g, t, off=off: (0, off * n_g + g))

    def hist(off):
        return pl.BlockSpec((None, SUBLANES, gw), lambda b, g, t, off=off: (b, 0, off * n_g + g))

    nb_spec = pl.BlockSpec((None, SUBLANES, gw), lambda b, g, t: (b, 0, g))
    nb_shape = jax.ShapeDtypeStruct((bn, SUBLANES, dnw), f32)
    return pl.pallas_call(
        kern,
        out_shape=(jax.ShapeDtypeStruct((bn, seq, dnw), bf16),
                   jax.ShapeDtypeStruct(s0.shape, f32), nb_shape, nb_shape, nb_shape),
        grid=(bn, n_g, n_t),
        in_specs=[col(0), col(1), col(2), col(3),
                  pl.BlockSpec((None, lt, LANES), lambda b, g, t: (b, t, g)),
                  pl.BlockSpec((None, SUBLANES, LANES), lambda b, g, t: (g, 0, 0)),
                  cw(0), cw(1), cw(2),
                  pl.BlockSpec((1, dk), lambda b, g, t: (0, 0)),
                  hist(0), hist(1), hist(2),
                  pl.BlockSpec((None, hg, dk, dk), lambda b, g, t: (b, g, 0, 0))],
        out_specs=(pl.BlockSpec((None, lt, gw), lambda b, g, t: (b, t, g)),
                   pl.BlockSpec((None, hg, dk, dk), lambda b, g, t: (b, g, 0, 0)),
                   nb_spec, nb_spec, nb_spec),
        scratch_shapes=[pltpu.VMEM((hg, dk, dk), f32),
                        pltpu.VMEM((SUBLANES, gw), f32), pltpu.VMEM((SUBLANES, gw), f32),
                        pltpu.VMEM((SUBLANES, gw), f32)],
        compiler_params=_cparams(("arbitrary", "arbitrary", "arbitrary")),
        name="deltanet",
    )(proj3, proj3, proj3, proj3, ba3, par, w_conv, w_conv, w_conv, norm_w.reshape(1, dk),
      hist8, hist8, hist8, s0)


def _sc_body(b_ref, c_ref, x_ref, w_ref, ha_ref, hb_ref, o_ref, tail_ref, *, seq_len):
    u = c_ref[...].astype(f32) * x_ref[...].astype(f32)
    rows = u.shape[0]
    tin = lax.broadcasted_iota(i32, u.shape, 0) % seq_len
    w = w_ref[...]
    conv = (w[2:3, :] * u
            + w[1:2, :] * jnp.where(tin >= 1, pltpu.roll(u, 1, axis=0), 0.0)
            + w[0:1, :] * jnp.where(tin >= 2, pltpu.roll(u, 2, axis=0), 0.0))
    corr = w[1:2, :] * ha_ref[...] + w[0:1, :] * hb_ref[...]
    bv = b_ref[...].astype(f32)
    if corr.shape[0] == rows:
        o_ref[...] = (bv * (conv + corr)).astype(o_ref.dtype)
        tail_ref[...] = u
    else:
        o_ref[...] = (bv * conv).astype(o_ref.dtype)
        o_ref[0:SUBLANES, :] = (bv[0:SUBLANES, :] * (conv[0:SUBLANES, :] + corr)).astype(o_ref.dtype)
        tail_ref[...] = u[rows - SUBLANES:, :]


def _short_conv(proj, col0, scw, w_sc, hist_a, hist_b, seq_len, rows_per_step, tw):
    t = proj.shape[0]
    n_seq = t // seq_len
    hr = SUBLANES if rows_per_step == seq_len else rows_per_step
    n_w = scw // tw
    cb = col0 // tw

    def pin(off):
        return pl.BlockSpec((rows_per_step, tw), lambda r, j, off=off: (r, cb + off * n_w + j))

    h_spec = pl.BlockSpec((hr, tw), lambda r, j: (r, j))
    return pl.pallas_call(
        functools.partial(_sc_body, seq_len=seq_len),
        out_shape=(jax.ShapeDtypeStruct((t, scw), bf16),
                   jax.ShapeDtypeStruct((n_seq * SUBLANES, scw), f32)),
        grid=(t // rows_per_step, n_w),
        in_specs=[pin(0), pin(1), pin(2), pl.BlockSpec((w_sc.shape[0], tw), lambda r, j: (0, j)),
                  h_spec, h_spec],
        out_specs=(pl.BlockSpec((rows_per_step, tw), lambda r, j: (r, j)),
                   pl.BlockSpec((hr, tw), lambda r, j: (r, j))),
        compiler_params=_cparams(("arbitrary", "arbitrary")),
        name="short_conv",
    )(proj, proj, proj, w_sc, hist_a, hist_b)


def _merge_body(o_ref, s_ref, gd_ref, gs_ref, wdn_ref, wsc_ref, m_ref):
    ydn = jnp.dot(o_ref[...], wdn_ref[...], preferred_element_type=f32)
    ysc = jnp.dot(s_ref[...], wsc_ref[...], preferred_element_type=f32)
    m = jax.nn.sigmoid(gd_ref[...].astype(f32)) * ydn + jax.nn.sigmoid(gs_ref[...].astype(f32)) * ysc
    m_ref[...] = m.astype(m_ref.dtype)


def _merge(o_n, scin, proj, gd_col, gs_col, w_dn, w_sc, tm):
    t, dnw = o_n.shape
    scw = scin.shape[1]
    d = w_dn.shape[1]
    return pl.pallas_call(
        _merge_body,
        out_shape=jax.ShapeDtypeStruct((t, d), bf16),
        grid=(t // tm,),
        in_specs=[pl.BlockSpec((tm, dnw), lambda m: (m, 0)),
                  pl.BlockSpec((tm, scw), lambda m: (m, 0)),
                  pl.BlockSpec((tm, d), lambda m: (m, gd_col // d)),
                  pl.BlockSpec((tm, d), lambda m: (m, gs_col // d)),
                  pl.BlockSpec((dnw, d), lambda m: (0, 0)),
                  pl.BlockSpec((scw, d), lambda m: (0, 0))],
        out_specs=pl.BlockSpec((tm, d), lambda m: (m, 0)),
        compiler_params=_cparams(("arbitrary",)),
        name="merge",
    )(o_n, scin, proj, proj, w_dn, w_sc)


def _split_dot3(a, b):
    a_hi = a.astype(bf16)
    a_lo = (a - a_hi.astype(f32)).astype(bf16)
    b_hi = b.astype(bf16)
    b_lo = (b - b_hi.astype(f32)).astype(bf16)
    return (jnp.dot(a_hi, b_hi, preferred_element_type=f32) + jnp.dot(a_hi, b_lo, preferred_element_type=f32)
            + jnp.dot(a_lo, b_hi, preferred_element_type=f32))


def _mix_body(m_ref, x_ref, g1_ref, sh2_ref, sc2_ref, wo_ref, lg_ref, lb_ref, wr_ref, rb_ref, cnt_in_ref,
              x1_ref, hp_ref, idx_ref, wt_ref, pos_ref, cnt_ref, cnt_scr, *, top_k):
    step = pl.program_id(0)

    @pl.when(step == 0)
    def _():
        cnt_scr[...] = cnt_in_ref[...]

    mix = jnp.dot(m_ref[...], wo_ref[...], preferred_element_type=f32)
    x1 = _layer_norm(DEEPNORM_ALPHA * x_ref[...] + (1.0 + g1_ref[...]) * mix, lg_ref[...], lb_ref[...])
    x1_ref[...] = x1
    h2 = x1 * (1.0 + sc2_ref[...]) + sh2_ref[...]
    half = h2.shape[1] // 2
    hp_ref[...] = _pack_pair(h2[:, :half], h2[:, half:])

    tm = h2.shape[0]
    n_e = wr_ref.shape[1]
    scores = jax.nn.sigmoid(_split_dot3(h2, wr_ref[...]))
    work = scores + rb_ref[...]
    lane_e = lax.broadcasted_iota(i32, (tm, n_e), 1).astype(f32)
    lane_o = lax.broadcasted_iota(i32, (tm, LANES), 1)
    sel = jnp.zeros((tm, n_e), f32)
    picks = []
    wsum = jnp.zeros((tm, 1), f32)
    for _ in range(top_k):
        mx = jnp.max(work, axis=-1, keepdims=True)
        idx = jnp.min(jnp.where(work == mx, lane_e, float(n_e)), axis=-1, keepdims=True)
        hit = lane_e == idx
        wk = jnp.sum(jnp.where(hit, scores, 0.0), axis=-1, keepdims=True)
        work = jnp.where(hit, -jnp.inf, work)
        sel = jnp.where(hit, 1.0, sel)
        wsum = wsum + wk
        picks.append((idx, hit, wk))

    ri = lax.broadcasted_iota(i32, (tm, tm), 0)
    ci = lax.broadcasted_iota(i32, (tm, tm), 1)
    tri = jnp.where(ri > ci, 1.0, 0.0).astype(bf16)
    before = jnp.dot(tri, sel.astype(bf16), preferred_element_type=f32) + cnt_scr[...]
    cnt_scr[...] = cnt_scr[...] + jnp.sum(sel, axis=0, keepdims=True)
    cnt_ref[...] = cnt_scr[...]

    idx_o = jnp.zeros((tm, LANES), i32)
    wt_o = jnp.zeros((tm, LANES), f32)
    pos_o = jnp.zeros((tm, LANES), i32)
    for k, (idx, hit, wk) in enumerate(picks):
        pk = jnp.sum(jnp.where(hit, before, 0.0), axis=-1, keepdims=True).astype(i32)
        idx_o = jnp.where(lane_o == k, idx.astype(i32), idx_o)
        wt_o = jnp.where(lane_o == k, wk / wsum * ROUTED_SCALE, wt_o)
        pos_o = jnp.where(lane_o == k, pk, pos_o)
    idx_ref[...] = idx_o
    wt_ref[...] = wt_o
    pos_ref[...] = pos_o


def _mix(merged, x, mod, seq_len, w_o, ln_g, ln_b, w_router, router_bias, cnt_in, tm):
    t, d = x.shape
    n_e = w_router.shape[1]
    g1_spec, sh2_spec, sc2_spec = _mod_specs(mod, tm, seq_len, (2, 3, 4), d)
    row = lambda m: (m, 0)
    const = lambda m: (0, 0)
    lane_out = jax.ShapeDtypeStruct((t, LANES), i32)
    return pl.pallas_call(
        functools.partial(_mix_body, top_k=TOP_K),
        out_shape=(jax.ShapeDtypeStruct((t, d), f32), jax.ShapeDtypeStruct((t, d // 2), u32),
                   lane_out, jax.ShapeDtypeStruct((t, LANES), f32), lane_out,
                   jax.ShapeDtypeStruct((1, n_e), f32)),
        grid=(t // tm,),
        in_specs=[pl.BlockSpec((tm, d), row), pl.BlockSpec((tm, d), row), g1_spec, sh2_spec, sc2_spec,
                  pl.BlockSpec((d, d), const), pl.BlockSpec((1, d), const), pl.BlockSpec((1, d), const),
                  pl.BlockSpec((d, n_e), const), pl.BlockSpec((1, n_e), const), pl.BlockSpec((1, n_e), const)],
        out_specs=(pl.BlockSpec((tm, d), row), pl.BlockSpec((tm, d // 2), row),
                   pl.BlockSpec((tm, LANES), row), pl.BlockSpec((tm, LANES), row), pl.BlockSpec((tm, LANES), row),
                   pl.BlockSpec((1, n_e), const)),
        scratch_shapes=[pltpu.VMEM((1, n_e), f32)],
        compiler_params=_cparams(("arbitrary",)),
        name="mix",
    )(merged, x, mod, mod, mod, w_o, ln_g.reshape(1, d), ln_b.reshape(1, d), w_router,
      router_bias.reshape(1, n_e), cnt_in)


def _row_copy(src, src_row, dst, dst_row, sem):
    return pltpu.make_async_copy(src.at[pl.ds(src_row, 1)], dst.at[pl.ds(dst_row, 1)], sem)


def _dispatch_body(dest_hbm, h_ref, xs_in, xs_out, dsm, sem_idx, sem, *, top_k):
    del xs_in
    step = pl.program_id(0)
    tm = h_ref.shape[0]
    fetch = pltpu.make_async_copy(dest_hbm.at[step], dsm, sem_idx)
    fetch.start()
    fetch.wait()

    def issue(r, c):
        for k in range(top_k):
            _row_copy(h_ref, r, xs_out, dsm[r * top_k + k], sem).start()
        return c

    lax.fori_loop(0, tm, issue, 0)

    def drain(r, c):
        for k in range(top_k):
            _row_copy(h_ref, 0, xs_out, 0, sem).wait()
        return c

    lax.fori_loop(0, tm, drain, 0)


def _dispatch(dest2d, hp, xs, tm):
    t, half = hp.shape
    return pl.pallas_call(
        functools.partial(_dispatch_body, top_k=TOP_K),
        out_shape=jax.ShapeDtypeStruct(xs.shape, xs.dtype),
        grid=(t // tm,),
        in_specs=[pl.BlockSpec(memory_space=pl.ANY),
                  pl.BlockSpec((tm, half), lambda m: (m, 0)),
                  pl.BlockSpec(memory_space=pl.ANY)],
        out_specs=pl.BlockSpec(memory_space=pl.ANY),
        scratch_shapes=[pltpu.SMEM((tm * TOP_K,), i32), pltpu.SemaphoreType.DMA(()), pltpu.SemaphoreType.DMA(())],
        input_output_aliases={2: 0},
        compiler_params=_cparams(("arbitrary",)),
        name="dispatch",
    )(dest2d, hp, xs)


def _expert_body(te_ref, nt_ref, x_ref, wg_ref, wu_ref, wd_ref, o_ref, wg_s, wu_s, wd_s):
    step = pl.program_id(0)

    @pl.when(step < nt_ref[0])
    def _():
        e = te_ref[step]
        prev = te_ref[jnp.maximum(step - 1, 0)]

        @pl.when((step == 0) | (e != prev))
        def _():
            wg_s[...] = wg_ref[...].astype(bf16)
            wu_s[...] = wu_ref[...].astype(bf16)
            wd_s[...] = wd_ref[...].astype(bf16)

        lo, hi = _unpack_pair(x_ref[...])
        lo = lo.astype(bf16)
        hi = hi.astype(bf16)
        half = lo.shape[1]
        g = (jnp.dot(lo, wg_s[0:half, :], preferred_element_type=f32)
             + jnp.dot(hi, wg_s[half:, :], preferred_element_type=f32))
        u = (jnp.dot(lo, wu_s[0:half, :], preferred_element_type=f32)
             + jnp.dot(hi, wu_s[half:, :], preferred_element_type=f32))
        hid = (_silu(g) * u).astype(bf16)
        out = jnp.dot(hid, wd_s[...], preferred_element_type=f32)
        o_ref[...] = _pack_pair(out[:, :half], out[:, half:])


def _experts(tile_expert, n_tiles, xs, w_gate, w_up, w_down, tm):
    rows, half = xs.shape
    n_e, d, f = w_gate.shape
    max_tiles = rows // tm

    def tile(i, te, nt):
        return (jnp.minimum(i, nt[0] - 1), 0)

    def wsel(i, te, nt):
        return (te[jnp.minimum(i, nt[0] - 1)], 0, 0)

    return pl.pallas_call(
        _expert_body,
        out_shape=jax.ShapeDtypeStruct((rows, half), u32),
        grid_spec=pltpu.PrefetchScalarGridSpec(
            num_scalar_prefetch=2,
            grid=(max_tiles,),
            in_specs=[pl.BlockSpec((tm, half), tile),
                      pl.BlockSpec((None, d, f), wsel), pl.BlockSpec((None, d, f), wsel),
                      pl.BlockSpec((None, f, d), wsel)],
            out_specs=pl.BlockSpec((tm, half), tile),
            scratch_shapes=[pltpu.VMEM((d, f), bf16), pltpu.VMEM((d, f), bf16), pltpu.VMEM((f, d), bf16)]),
        input_output_aliases={2: 0},
        compiler_params=_cparams(("arbitrary",)),
        name="experts",
    )(tile_expert, n_tiles, xs, w_gate, w_up, w_down)


def _combine_body(dest_hbm, os_hbm, x1_ref, hp_ref, wt_ref, g2_ref, lg_ref, lb_ref, wsg_ref, wsu_ref, wsd_ref,
                  y_ref, dsm, buf, sem_idx, sem, *, top_k):
    step = pl.program_id(0)
    tm = x1_ref.shape[0]
    fetch = pltpu.make_async_copy(dest_hbm.at[step], dsm, sem_idx)
    fetch.start()
    fetch.wait()

    def issue(r, c):
        for k in range(top_k):
            _row_copy(os_hbm, dsm[r * top_k + k], buf.at[k], r, sem).start()
        return c

    lax.fori_loop(0, tm, issue, 0)

    lo, hi = _unpack_pair(hp_ref[...])
    lo = lo.astype(bf16)
    hi = hi.astype(bf16)
    half = lo.shape[1]
    g = (jnp.dot(lo, wsg_ref[0:half, :], preferred_element_type=f32)
         + jnp.dot(hi, wsg_ref[half:, :], preferred_element_type=f32))
    u = (jnp.dot(lo, wsu_ref[0:half, :], preferred_element_type=f32)
         + jnp.dot(hi, wsu_ref[half:, :], preferred_element_type=f32))
    shared = jnp.dot((_silu(g) * u).astype(bf16), wsd_ref[...], preferred_element_type=f32)

    def drain(r, c):
        for k in range(top_k):
            _row_copy(os_hbm, 0, buf.at[k], 0, sem).wait()
        return c

    lax.fori_loop(0, tm, drain, 0)

    y_lo = shared[:, :half]
    y_hi = shared[:, half:]
    wt = wt_ref[...]
    for k in range(top_k):
        e_lo, e_hi = _unpack_pair(buf[k])
        wk = wt[:, k:k + 1]
        y_lo = y_lo + wk * e_lo
        y_hi = y_hi + wk * e_hi
    ffn = jnp.concatenate([y_lo, y_hi], axis=1)
    y_ref[...] = _layer_norm(DEEPNORM_ALPHA * x1_ref[...] + (1.0 + g2_ref[...]) * ffn, lg_ref[...], lb_ref[...])


def _combine(dest2d, out_sorted, x1, hp, wts, mod, seq_len, ln_g, ln_b, ws_gate, ws_up, ws_down, tm):
    t, d = x1.shape
    half = d // 2
    sf = ws_gate.shape[1]
    (g2_spec,) = _mod_specs(mod, tm, seq_len, (5,), d)
    row = lambda m: (m, 0)
    const = lambda m: (0, 0)
    return pl.pallas_call(
        functools.partial(_combine_body, top_k=TOP_K),
        out_shape=jax.ShapeDtypeStruct((t, d), f32),
        grid=(t // tm,),
        in_specs=[pl.BlockSpec(memory_space=pl.ANY), pl.BlockSpec(memory_space=pl.ANY),
                  pl.BlockSpec((tm, d), row), pl.BlockSpec((tm, half), row), pl.BlockSpec((tm, LANES), row),
                  g2_spec, pl.BlockSpec((1, d), const), pl.BlockSpec((1, d), const),
                  pl.BlockSpec((d, sf), const), pl.BlockSpec((d, sf), const), pl.BlockSpec((sf, d), const)],
        out_specs=pl.BlockSpec((tm, d), row),
        scratch_shapes=[pltpu.SMEM((tm * TOP_K,), i32), pltpu.VMEM((TOP_K, tm, half), u32),
                        pltpu.SemaphoreType.DMA(()), pltpu.SemaphoreType.DMA(())],
        compiler_params=_cparams(("arbitrary",)),
        name="combine",
    )(dest2d, out_sorted, x1, hp, wts, mod, ln_g.reshape(1, d), ln_b.reshape(1, d), ws_gate, ws_up, ws_down)


def _pick(total, pref, unit=SUBLANES):
    if total <= pref:
        return total
    c = pref - pref % unit
    while total % c:
        c -= unit
    return c


def kernel(x_prompt, x_sample, state_dn_S, state_dn_conv, state_sc_conv, c_prompt, c_sample, w_ada, b_ada, w_in, w_dn_conv, dn_a_log, dn_dt_bias, dn_norm_w, w_sc_conv, w_dn_out, w_sc_out, w_o, ln1_g, ln1_b, w_router, router_bias, w_e_gate, w_e_up, w_e_down, w_s_gate, w_s_up, w_s_down, ln2_g, ln2_b):
    bp, lp, d = x_prompt.shape
    bs, ls, _ = x_sample.shape
    n_heads = dn_a_log.shape[0]
    dk = dn_norm_w.shape[0]
    assert dk == LANES
    dnw = n_heads * dk
    scw = w_sc_conv.shape[1]
    n_e = w_router.shape[1]
    tp, ts = bp * lp, bs * ls

    n_a = 4 * dnw
    o_sc = n_a + 2 * n_heads
    o_gd = o_sc + 3 * scw
    w_main = jnp.concatenate([w_in[:, :n_a], w_in[:, o_gd:o_gd + 2 * d], w_in[:, o_sc:o_gd]], axis=1).astype(bf16)
    c_gd = n_a
    c_gs = c_gd + d
    c_sc = c_gs + d
    hg_p = min(4, n_heads)
    hg_s = n_heads

    def ba_weights(hg):
        n_g = n_heads // hg
        wb = w_in[:, n_a:n_a + n_heads].reshape(d, n_g, hg)
        wa = w_in[:, n_a + n_heads:n_a + 2 * n_heads].reshape(d, n_g, hg)
        pad = jnp.zeros((d, n_g, LANES - 2 * hg), w_in.dtype)
        w = jnp.concatenate([wb, wa, pad], axis=2).reshape(d, n_g * LANES).astype(bf16)
        zl = jnp.zeros((n_g, hg), f32)
        lane = lambda v: jnp.concatenate([zl, v.astype(f32).reshape(n_g, hg), jnp.zeros((n_g, LANES - 2 * hg), f32)], axis=1)
        par = jnp.stack([lane(dn_a_log), lane(dn_dt_bias)] + [jnp.zeros((n_g, LANES), f32)] * (SUBLANES - 2), axis=1)
        return w, par

    w_dn_b = w_dn_out.astype(bf16)
    w_sc_b = w_sc_out.astype(bf16)
    w_o_b = w_o.astype(bf16)
    ws_gate_b = w_s_gate.astype(bf16)
    ws_up_b = w_s_up.astype(bf16)
    ws_down_b = w_s_down.astype(bf16)

    n_c = bp + bs
    m_pad = -(-n_c // SUBLANES) * SUBLANES
    c_all = jnp.concatenate([c_prompt, c_sample, jnp.zeros((m_pad - n_c, d), f32)], axis=0)
    mod = _ada(c_all, w_ada, b_ada)
    mod_p = mod[:bp].reshape(bp, 1, 6 * d)
    mod_s = jnp.repeat(mod[bp:n_c], ls, axis=0)

    def group(x3, mod_g, s0, dn_hist, sc_hist, hg, chunk, lt, cnt_in):
        bn, seq, _ = x3.shape
        t = bn * seq
        x = x3.reshape(t, d)
        tile = (lambda pref: _pick(seq, pref)) if mod_g.ndim == 3 else (lambda pref: _pick(t, pref))
        proj = _modmm(x, mod_g, seq, w_main, bf16, tile(1024), 512)
        w_ba, par = ba_weights(hg)
        ba = _modmm(x, mod_g, seq, w_ba, f32, tile(1024), 512)
        hist8 = jnp.concatenate([jnp.zeros((bn, SUBLANES - dn_hist.shape[1], 3 * dnw), f32), dn_hist], axis=1)
        o_n, s_new, nbq, nbk, nbv = _deltanet(
            proj.reshape(bn, seq, -1), ba.reshape(bn, seq, -1), par, w_dn_conv, dn_norm_w, hist8, s0,
            n_heads=n_heads, heads_per_step=hg, chunk=chunk, lt=lt)
        keep = w_dn_conv.shape[0] - 1
        dn_buf = jnp.concatenate([nbq, nbk, nbv], axis=2)[:, SUBLANES - keep:, :]

        zrow = jnp.zeros((bn, SUBLANES - 2, scw), f32)
        hist_a = jnp.concatenate([sc_hist[:, 1:2], jnp.zeros((bn, 1, scw), f32), zrow], axis=1).reshape(bn * SUBLANES, scw)
        hist_b = jnp.concatenate([sc_hist[:, 0:1], sc_hist[:, 1:2], zrow], axis=1).reshape(bn * SUBLANES, scw)
        rows_sc = seq if seq > SUBLANES else _pick(t, 256)
        scin, tail = _short_conv(proj, c_sc, scw, w_sc_conv, hist_a, hist_b, seq, rows_sc, min(scw, 256))
        sc_keep = w_sc_conv.shape[0] - 1
        sc_buf = tail.reshape(bn, SUBLANES, scw)[:, SUBLANES - sc_keep:, :]

        merged = _merge(o_n.reshape(t, dnw), scin, proj, c_gd, c_gs, w_dn_b, w_sc_b, tile(256))
        x1, hp, idx, wts, pos, cnt = _mix(merged, x, mod_g, seq, w_o_b, ln1_g, ln1_b, w_router, router_bias,
                                          cnt_in, tile(256))
        return dict(x1=x1, hp=hp, idx=idx[:, :TOP_K], wts=wts, pos=pos[:, :TOP_K], cnt=cnt,
                    s=s_new, dn_buf=dn_buf, sc_buf=sc_buf, mod=mod_g, seq=seq, tm_d=tile(256))

    zeros_s = jnp.zeros((bp,) + state_dn_S.shape[1:], f32)
    zeros_dn = jnp.zeros((bp,) + state_dn_conv.shape[1:], f32)
    zeros_sc = jnp.zeros((bp,) + state_sc_conv.shape[1:], f32)
    chunk_p = math.gcd(DN_CHUNK, lp)
    chunk_s = math.gcd(DN_CHUNK, ls)
    gp = group(x_prompt, mod_p, zeros_s, zeros_dn, zeros_sc, hg_p, chunk_p, _pick(lp, 512),
               jnp.zeros((1, n_e), f32))
    gs = group(x_sample, mod_s, state_dn_S, state_dn_conv, state_sc_conv, hg_s, chunk_s, ls, gp["cnt"])

    tm_e = 256
    counts = gs["cnt"][0].astype(i32)
    tiles_per_e = (counts + tm_e - 1) // tm_e
    tile_end = jnp.cumsum(tiles_per_e)
    start_row = (tile_end - tiles_per_e) * tm_e
    n_rows = (tp + ts) * TOP_K
    max_tiles = n_rows // tm_e + n_e
    tile_expert = jnp.minimum(jnp.searchsorted(tile_end, jnp.arange(max_tiles, dtype=i32), side="right"),
                              n_e - 1).astype(i32)
    n_tiles = tile_end[-1:].astype(i32)

    for g in (gp, gs):
        t = g["x1"].shape[0]
        g["dest"] = (start_row[g["idx"]] + g["pos"]).reshape(t // g["tm_d"], g["tm_d"] * TOP_K)
    xs = jnp.zeros((max_tiles * tm_e, d // 2), u32)
    for g in (gp, gs):
        xs = _dispatch(g["dest"], g["hp"], xs, g["tm_d"])
    out_sorted = _experts(tile_expert, n_tiles, xs, w_e_gate, w_e_up, w_e_down, tm_e)
    ys = [_combine(g["dest"], out_sorted, g["x1"], g["hp"], g["wts"], g["mod"], g["seq"], ln2_g, ln2_b,
                   ws_gate_b, ws_up_b, ws_down_b, g["tm_d"]) for g in (gp, gs)]

    y_p = ys[0].reshape(bp, lp, d)
    y_s = ys[1].reshape(bs, ls, d)
    return (y_p, y_s, gp["s"], gp["dn_buf"], gp["sc_buf"], gs["s"], gs["dn_buf"], gs["sc_buf"])
```

```python
import functools
import math

import jax
import jax.numpy as jnp
from jax import lax
from jax.experimental import pallas as pl
from jax.experimental.pallas import tpu as pltpu

f32 = jnp.float32
bf16 = jnp.bfloat16
i32 = jnp.int32
u32 = jnp.uint32

TOP_K = 8
ROUTED_SCALE = 2.5
DN_CHUNK = 64
DEPTH = 1
DEEPNORM_ALPHA = (2 * DEPTH) ** 0.25
LN_EPS = 1e-5
RMS_EPS = 1e-6

LANES = 128
SUBLANES = 8
VMEM_LIMIT = 56 * 1024 * 1024


def _cparams(sem, vmem=VMEM_LIMIT):
    return pltpu.CompilerParams(dimension_semantics=sem, vmem_limit_bytes=vmem)


def _silu(x):
    return x * jax.nn.sigmoid(x)


def _bdot(a, b):
    return jnp.dot(a.astype(bf16), b.astype(bf16), preferred_element_type=f32)


def _layer_norm(x, g, b):
    mu = jnp.mean(x, axis=-1, keepdims=True)
    xc = x - mu
    var = jnp.mean(xc * xc, axis=-1, keepdims=True)
    return xc * lax.rsqrt(var + LN_EPS) * g + b


def _pack_pair(a, b):
    ab = pltpu.bitcast(a.astype(bf16).astype(f32), u32)
    bb = pltpu.bitcast(b.astype(bf16).astype(f32), u32)
    return (ab >> 16) | (bb & jnp.uint32(0xFFFF0000))


def _unpack_pair(p):
    lo = pltpu.bitcast(p << 16, f32)
    hi = pltpu.bitcast(p & jnp.uint32(0xFFFF0000), f32)
    return lo, hi


def _ada_body(c_ref, w_ref, b_ref, o_ref):
    s = _silu(c_ref[...])
    o_ref[...] = _bdot(s, w_ref[...]) + b_ref[...]


def _ada(c_all, w_ada, b_ada):
    m, d = c_all.shape
    n = w_ada.shape[1]
    tn = _pick(n, 1024, LANES)
    return pl.pallas_call(
        _ada_body,
        out_shape=jax.ShapeDtypeStruct((m, n), f32),
        grid=(n // tn,),
        in_specs=[pl.BlockSpec((m, d), lambda j: (0, 0)),
                  pl.BlockSpec((d, tn), lambda j: (0, j)),
                  pl.BlockSpec((1, tn), lambda j: (0, j))],
        out_specs=pl.BlockSpec((m, tn), lambda j: (0, j)),
        compiler_params=_cparams(("arbitrary",)),
        name="ada",
    )(c_all, w_ada, b_ada.reshape(1, n))


def _modmm_body(x_ref, sc_ref, sh_ref, w_ref, o_ref, h_scr):
    @pl.when(pl.program_id(1) == 0)
    def _():
        h_scr[...] = (x_ref[...] * (1.0 + sc_ref[...]) + sh_ref[...]).astype(bf16)

    o_ref[...] = jnp.dot(h_scr[...], w_ref[...], preferred_element_type=f32).astype(o_ref.dtype)


def _mod_specs(mod, tm, seq_len, cols, d):
    if mod.ndim == 3:
        tiles_per_seq = seq_len // tm
        return [pl.BlockSpec((None, 1, d), lambda m, *_, c=c: (m // tiles_per_seq, 0, c)) for c in cols]
    return [pl.BlockSpec((tm, d), lambda m, *_, c=c: (m, c)) for c in cols]


def _modmm(x, mod, seq_len, w, out_dtype, tm, tn):
    t, d = x.shape
    n = w.shape[1]
    tn = _pick(n, tn, LANES)
    sc_spec, sh_spec = _mod_specs(mod, tm, seq_len, (1, 0), d)
    return pl.pallas_call(
        _modmm_body,
        out_shape=jax.ShapeDtypeStruct((t, n), out_dtype),
        grid=(t // tm, n // tn),
        in_specs=[pl.BlockSpec((tm, d), lambda m, j: (m, 0)), sc_spec, sh_spec,
                  pl.BlockSpec((d, tn), lambda m, j: (0, j))],
        out_specs=pl.BlockSpec((tm, tn), lambda m, j: (m, j)),
        scratch_shapes=[pltpu.VMEM((tm, d), bf16)],
        compiler_params=_cparams(("arbitrary", "arbitrary")),
        name="modmm",
    )(x, mod, mod, w)


def _cumsum_rows(x, n):
    row = lax.broadcasted_iota(i32, x.shape, 0)
    s = 1
    while s < n:
        x = x + jnp.where(row >= s, pltpu.roll(x, s, axis=0), 0.0)
        s *= 2
    return x


def _conv_silu(x, prev, w):
    c = x.shape[0]
    taps = w.shape[0]
    xc = jnp.concatenate([prev, x], axis=0)
    y = w[taps - 1:taps, :] * x
    for i in range(taps - 1):
        back = taps - 1 - i
        y = y + w[i:i + 1, :] * pltpu.roll(xc, back, axis=0)[SUBLANES:SUBLANES + c, :]
    return _silu(y)


def _dn_body(q_ref, k_ref, v_ref, z_ref, ba_ref, par_ref, cwq_ref, cwk_ref, cwv_ref, nw_ref,
             hq_ref, hk_ref, hv_ref, s0_ref,
             o_ref, so_ref, nbq_ref, nbk_ref, nbv_ref,
             s_scr, pq, pk, pv, *, chunk, heads, n_chunks):
    t = pl.program_id(2)
    dk = LANES

    @pl.when(t == 0)
    def _init():
        s_scr[...] = s0_ref[...]
        pq[...] = hq_ref[...]
        pk[...] = hk_ref[...]
        pv[...] = hv_ref[...]

    neg_a = -jnp.exp(par_ref[0:1, :])
    dt_b = par_ref[1:2, :]
    nw = nw_ref[...]
    ii = lax.broadcasted_iota(i32, (chunk, chunk), 0)
    jj = lax.broadcasted_iota(i32, (chunk, chunk), 1)
    causal = ii >= jj
    strict = ii > jj
    eye = (ii == jj).astype(f32)
    zpad = jnp.zeros((LANES - chunk, LANES), f32)

    def do_chunk(ci):
        r0 = pl.multiple_of(ci * chunk, chunk)
        rows = pl.ds(r0, chunk)
        ba = ba_ref[rows, :]
        beta_all = jax.nn.sigmoid(ba)
        xs = ba + dt_b
        softplus = jnp.maximum(xs, 0.0) + jnp.log1p(jnp.exp(-jnp.abs(xs)))
        gc_all = _cumsum_rows(neg_a * softplus, chunk)
        gc_t = jnp.concatenate([gc_all, zpad], axis=0).T
        eg_all = jnp.exp(gc_all)

        xq = q_ref[rows, :].astype(f32)
        xk = k_ref[rows, :].astype(f32)
        xv = v_ref[rows, :].astype(f32)
        qc = _conv_silu(xq, pq[...], cwq_ref[...])
        kc = _conv_silu(xk, pk[...], cwk_ref[...])
        vc = _conv_silu(xv, pv[...], cwv_ref[...])
        pq[...] = xq[chunk - SUBLANES:, :]
        pk[...] = xk[chunk - SUBLANES:, :]
        pv[...] = xv[chunk - SUBLANES:, :]

        hs = range(heads)
        sls = [slice(h * dk, (h + 1) * dk) for h in hs]
        qn = [qc[:, sl] for sl in sls]
        kn = [kc[:, sl] for sl in sls]
        qn = [q * (lax.rsqrt(jnp.sum(q * q, axis=-1, keepdims=True) + RMS_EPS) * (dk ** -0.5)) for q in qn]
        kn = [k * lax.rsqrt(jnp.sum(k * k, axis=-1, keepdims=True) + RMS_EPS) for k in kn]
        beta = [beta_all[:, h:h + 1] for h in hs]
        gc = [gc_all[:, heads + h:heads + h + 1] for h in hs]
        eg = [eg_all[:, heads + h:heads + h + 1] for h in hs]
        qk_kk = [lax.dot_general(jnp.concatenate([qn[h], kn[h]], axis=0).astype(bf16), kn[h].astype(bf16),
                                 (((1,), (1,)), ((), ())), preferred_element_type=f32) for h in hs]
        decay = []
        for h in hs:
            diff = gc[h] - gc_t[heads + h:heads + h + 1, 0:chunk]
            decay.append(jnp.where(causal, jnp.exp(jnp.where(causal, diff, 0.0)), 0.0))
        qk = [qk_kk[h][:chunk, :] * decay[h] for h in hs]
        p = [-jnp.where(strict, qk_kk[h][chunk:, :] * beta[h] * decay[h], 0.0) for h in hs]
        tinv = [eye + p[h] for h in hs]
        n = 1
        while 2 * n < chunk:
            p = [_bdot(p[h], p[h]) for h in hs]
            tinv = [tinv[h] + _bdot(p[h], tinv[h]) for h in hs]
            n *= 2
        sol = [_bdot(tinv[h], jnp.concatenate([vc[:, sls[h]] * beta[h], kn[h] * (beta[h] * eg[h])], axis=1))
               for h in hs]
        s_old = [s_scr[h] for h in hs]
        ks_qs = [_bdot(jnp.concatenate([sol[h][:, dk:], qn[h] * eg[h]], axis=0), s_old[h]) for h in hs]
        u = [sol[h][:, :dk] - ks_qs[h][:chunk, :] for h in hs]
        o = [ks_qs[h][chunk:, :] + _bdot(qk[h], u[h]) for h in hs]
        for h in hs:
            gc_last = gc[h][chunk - 1:chunk, :]
            kd = kn[h] * jnp.exp(gc_last - gc[h])
            s_scr[h] = s_old[h] * jnp.exp(gc_last) + lax.dot_general(
                kd.astype(bf16), u[h].astype(bf16), (((0,), (0,)), ((), ())), preferred_element_type=f32)
        for h in hs:
            zh = z_ref[rows, sls[h]].astype(f32)
            on = o[h] * lax.rsqrt(jnp.mean(o[h] * o[h], axis=-1, keepdims=True) + RMS_EPS) * nw * _silu(zh)
            o_ref[rows, sls[h]] = on.astype(o_ref.dtype)

    if n_chunks == 1:
        do_chunk(0)
    else:
        def loop_body(ci, carry):
            do_chunk(ci)
            return carry
        lax.fori_loop(0, n_chunks, loop_body, 0)

    @pl.when(t == pl.num_programs(2) - 1)
    def _fin():
        so_ref[...] = s_scr[...]
        nbq_ref[...] = pq[...]
        nbk_ref[...] = pk[...]
        nbv_ref[...] = pv[...]


def _deltanet(proj3, ba3, par, w_conv, norm_w, hist8, s0, *, n_heads, heads_per_step, chunk, lt):
    bn, seq, _ = proj3.shape
    dk = LANES
    dnw = n_heads * dk
    hg = heads_per_step
    gw = hg * dk
    n_g = n_heads // hg
    n_t = seq // lt
    kern = functools.partial(_dn_body, chunk=chunk, heads=hg, n_chunks=lt // chunk)

    def col(off):
        return pl.BlockSpec((None, lt, gw), lambda b, g, t, off=off: (b, t, off * n_g + g))

    def cw(off):
        return pl.BlockSpec((w_conv.shape[0], gw), lambda b, g, t, off=off: (0, off * n_g + g))

    def hist(off):
        return pl.BlockSpec((None, SUBLANES, gw), lambda b, g, t, off=off: (b, 0, off * n_g + g))

    nb_spec = pl.BlockSpec((None, SUBLANES, gw), lambda b, g, t: (b, 0, g))
    nb_shape = jax.ShapeDtypeStruct((bn, SUBLANES, dnw), f32)
    return pl.pallas_call(
        kern,
        out_shape=(jax.ShapeDtypeStruct((bn, seq, dnw), bf16),
                   jax.ShapeDtypeStruct(s0.shape, f32), nb_shape, nb_shape, nb_shape),
        grid=(bn, n_g, n_t),
        in_specs=[col(0), col(1), col(2), col(3),
                  pl.BlockSpec((None, lt, LANES), lambda b, g, t: (b, t, g)),
                  pl.BlockSpec((None, SUBLANES, LANES), lambda b, g, t: (g, 0, 0)),
                  cw(0), cw(1), cw(2),
                  pl.BlockSpec((1, dk), lambda b, g, t: (0, 0)),
                  hist(0), hist(1), hist(2),
                  pl.BlockSpec((None, hg, dk, dk), lambda b, g, t: (b, g, 0, 0))],
        out_specs=(pl.BlockSpec((None, lt, gw), lambda b, g, t: (b, t, g)),
                   pl.BlockSpec((None, hg, dk, dk), lambda b, g, t: (b, g, 0, 0)),
                   nb_spec, nb_spec, nb_spec),
        scratch_shapes=[pltpu.VMEM((hg, dk, dk), f32),
                        pltpu.VMEM((SUBLANES, gw), f32), pltpu.VMEM((SUBLANES, gw), f32),
                        pltpu.VMEM((SUBLANES, gw), f32)],
        compiler_params=_cparams(("arbitrary", "arbitrary", "arbitrary")),
        name="deltanet",
    )(proj3, proj3, proj3, proj3, ba3, par, w_conv, w_conv, w_conv, norm_w.reshape(1, dk),
      hist8, hist8, hist8, s0)


def _sc_body(b_ref, c_ref, x_ref, w_ref, ha_ref, hb_ref, o_ref, tail_ref, *, seq_len):
    u = c_ref[...].astype(f32) * x_ref[...].astype(f32)
    rows = u.shape[0]
    tin = lax.broadcasted_iota(i32, u.shape, 0) % seq_len
    w = w_ref[...]
    conv = (w[2:3, :] * u
            + w[1:2, :] * jnp.where(tin >= 1, pltpu.roll(u, 1, axis=0), 0.0)
            + w[0:1, :] * jnp.where(tin >= 2, pltpu.roll(u, 2, axis=0), 0.0))
    corr = w[1:2, :] * ha_ref[...] + w[0:1, :] * hb_ref[...]
    bv = b_ref[...].astype(f32)
    if corr.shape[0] == rows:
        o_ref[...] = (bv * (conv + corr)).astype(o_ref.dtype)
        tail_ref[...] = u
    else:
        o_ref[...] = (bv * conv).astype(o_ref.dtype)
        o_ref[0:SUBLANES, :] = (bv[0:SUBLANES, :] * (conv[0:SUBLANES, :] + corr)).astype(o_ref.dtype)
        tail_ref[...] = u[rows - SUBLANES:, :]


def _short_conv(proj, col0, scw, w_sc, hist_a, hist_b, seq_len, rows_per_step, tw):
    t = proj.shape[0]
    n_seq = t // seq_len
    hr = SUBLANES if rows_per_step == seq_len else rows_per_step
    n_w = scw // tw
    cb = col0 // tw

    def pin(off):
        return pl.BlockSpec((rows_per_step, tw), lambda r, j, off=off: (r, cb + off * n_w + j))

    h_spec = pl.BlockSpec((hr, tw), lambda r, j: (r, j))
    return pl.pallas_call(
        functools.partial(_sc_body, seq_len=seq_len),
        out_shape=(jax.ShapeDtypeStruct((t, scw), bf16),
                   jax.ShapeDtypeStruct((n_seq * SUBLANES, scw), f32)),
        grid=(t // rows_per_step, n_w),
        in_specs=[pin(0), pin(1), pin(2), pl.BlockSpec((w_sc.shape[0], tw), lambda r, j: (0, j)),
                  h_spec, h_spec],
        out_specs=(pl.BlockSpec((rows_per_step, tw), lambda r, j: (r, j)),
                   pl.BlockSpec((hr, tw), lambda r, j: (r, j))),
        compiler_params=_cparams(("arbitrary", "arbitrary")),
        name="short_conv",
    )(proj, proj, proj, w_sc, hist_a, hist_b)


def _merge_body(o_ref, s_ref, gd_ref, gs_ref, wdn_ref, wsc_ref, m_ref):
    ydn = jnp.dot(o_ref[...], wdn_ref[...], preferred_element_type=f32)
    ysc = jnp.dot(s_ref[...], wsc_ref[...], preferred_element_type=f32)
    m = jax.nn.sigmoid(gd_ref[...].astype(f32)) * ydn + jax.nn.sigmoid(gs_ref[...].astype(f32)) * ysc
    m_ref[...] = m.astype(m_ref.dtype)


def _merge(o_n, scin, proj, gd_col, gs_col, w_dn, w_sc, tm):
    t, dnw = o_n.shape
    scw = scin.shape[1]
    d = w_dn.shape[1]
    return pl.pallas_call(
        _merge_body,
        out_shape=jax.ShapeDtypeStruct((t, d), bf16),
        grid=(t // tm,),
        in_specs=[pl.BlockSpec((tm, dnw), lambda m: (m, 0)),
                  pl.BlockSpec((tm, scw), lambda m: (m, 0)),
                  pl.BlockSpec((tm, d), lambda m: (m, gd_col // d)),
                  pl.BlockSpec((tm, d), lambda m: (m, gs_col // d)),
                  pl.BlockSpec((dnw, d), lambda m: (0, 0)),
                  pl.BlockSpec((scw, d), lambda m: (0, 0))],
        out_specs=pl.BlockSpec((tm, d), lambda m: (m, 0)),
        compiler_params=_cparams(("arbitrary",)),
        name="merge",
    )(o_n, scin, proj, proj, w_dn, w_sc)


def _split_dot3(a, b):
    a_hi = a.astype(bf16)
    a_lo = (a - a_hi.astype(f32)).astype(bf16)
    b_hi = b.astype(bf16)
    b_lo = (b - b_hi.astype(f32)).astype(bf16)
    return (jnp.dot(a_hi, b_hi, preferred_element_type=f32) + jnp.dot(a_hi, b_lo, preferred_element_type=f32)
            + jnp.dot(a_lo, b_hi, preferred_element_type=f32))


def _mix_body(m_ref, x_ref, g1_ref, sh2_ref, sc2_ref, wo_ref, lg_ref, lb_ref, wr_ref, rb_ref, cnt_in_ref,
              x1_ref, hp_ref, idx_ref, wt_ref, pos_ref, cnt_ref, cnt_scr, *, top_k):
    step = pl.program_id(0)

    @pl.when(step == 0)
    def _():
        cnt_scr[...] = cnt_in_ref[...]

    mix = jnp.dot(m_ref[...], wo_ref[...], preferred_element_type=f32)
    x1 = _layer_norm(DEEPNORM_ALPHA * x_ref[...] + (1.0 + g1_ref[...]) * mix, lg_ref[...], lb_ref[...])
    x1_ref[...] = x1
    h2 = x1 * (1.0 + sc2_ref[...]) + sh2_ref[...]
    half = h2.shape[1] // 2
    hp_ref[...] = _pack_pair(h2[:, :half], h2[:, half:])

    tm = h2.shape[0]
    n_e = wr_ref.shape[1]
    scores = jax.nn.sigmoid(_split_dot3(h2, wr_ref[...]))
    work = scores + rb_ref[...]
    lane_e = lax.broadcasted_iota(i32, (tm, n_e), 1).astype(f32)
    lane_o = lax.broadcasted_iota(i32, (tm, LANES), 1)
    sel = jnp.zeros((tm, n_e), f32)
    picks = []
    wsum = jnp.zeros((tm, 1), f32)
    for _ in range(top_k):
        mx = jnp.max(work, axis=-1, keepdims=True)
        idx = jnp.min(jnp.where(work == mx, lane_e, float(n_e)), axis=-1, keepdims=True)
        hit = lane_e == idx
        wk = jnp.sum(jnp.where(hit, scores, 0.0), axis=-1, keepdims=True)
        work = jnp.where(hit, -jnp.inf, work)
        sel = jnp.where(hit, 1.0, sel)
        wsum = wsum + wk
        picks.append((idx, hit, wk))

    ri = lax.broadcasted_iota(i32, (tm, tm), 0)
    ci = lax.broadcasted_iota(i32, (tm, tm), 1)
    tri = jnp.where(ri > ci, 1.0, 0.0).astype(bf16)
    before = jnp.dot(tri, sel.astype(bf16), preferred_element_type=f32) + cnt_scr[...]
    cnt_scr[...] = cnt_scr[...] + jnp.sum(sel, axis=0, keepdims=True)
    cnt_ref[...] = cnt_scr[...]

    idx_o = jnp.zeros((tm, LANES), i32)
    wt_o = jnp.zeros((tm, LANES), f32)
    pos_o = jnp.zeros((tm, LANES), i32)
    for k, (idx, hit, wk) in enumerate(picks):
        pk = jnp.sum(jnp.where(hit, before, 0.0), axis=-1, keepdims=True).astype(i32)
        idx_o = jnp.where(lane_o == k, idx.astype(i32), idx_o)
        wt_o = jnp.where(lane_o == k, wk / wsum * ROUTED_SCALE, wt_o)
        pos_o = jnp.where(lane_o == k, pk, pos_o)
    idx_ref[...] = idx_o
    wt_ref[...] = wt_o
    pos_ref[...] = pos_o


def _mix(merged, x, mod, seq_len, w_o, ln_g, ln_b, w_router, router_bias, cnt_in, tm):
    t, d = x.shape
    n_e = w_router.shape[1]
    g1_spec, sh2_spec, sc2_spec = _mod_specs(mod, tm, seq_len, (2, 3, 4), d)
    row = lambda m: (m, 0)
    const = lambda m: (0, 0)
    lane_out = jax.ShapeDtypeStruct((t, LANES), i32)
    return pl.pallas_call(
        functools.partial(_mix_body, top_k=TOP_K),
        out_shape=(jax.ShapeDtypeStruct((t, d), f32), jax.ShapeDtypeStruct((t, d // 2), u32),
                   lane_out, jax.ShapeDtypeStruct((t, LANES), f32), lane_out,
                   jax.ShapeDtypeStruct((1, n_e), f32)),
        grid=(t // tm,),
        in_specs=[pl.BlockSpec((tm, d), row), pl.BlockSpec((tm, d), row), g1_spec, sh2_spec, sc2_spec,
                  pl.BlockSpec((d, d), const), pl.BlockSpec((1, d), const), pl.BlockSpec((1, d), const),
                  pl.BlockSpec((d, n_e), const), pl.BlockSpec((1, n_e), const), pl.BlockSpec((1, n_e), const)],
        out_specs=(pl.BlockSpec((tm, d), row), pl.BlockSpec((tm, d // 2), row),
                   pl.BlockSpec((tm, LANES), row), pl.BlockSpec((tm, LANES), row), pl.BlockSpec((tm, LANES), row),
                   pl.BlockSpec((1, n_e), const)),
        scratch_shapes=[pltpu.VMEM((1, n_e), f32)],
        compiler_params=_cparams(("arbitrary",)),
        name="mix",
    )(merged, x, mod, mod, mod, w_o, ln_g.reshape(1, d), ln_b.reshape(1, d), w_router,
      router_bias.reshape(1, n_e), cnt_in)


def _row_copy(src, src_row, dst, dst_row, sem):
    return pltpu.make_async_copy(src.at[pl.ds(src_row, 1)], dst.at[pl.ds(dst_row, 1)], sem)


def _dispatch_body(dest_hbm, h_ref, xs_in, xs_out, dsm, sem_idx, sem, *, top_k):
    del xs_in
    step = pl.program_id(0)
    tm = h_ref.shape[0]
    fetch = pltpu.make_async_copy(dest_hbm.at[step], dsm, sem_idx)
    fetch.start()
    fetch.wait()

    def issue(r, c):
        for k in range(top_k):
            _row_copy(h_ref, r, xs_out, dsm[r * top_k + k], sem).start(priority=k % 2)
        return c

    lax.fori_loop(0, tm, issue, 0)

    def drain(r, c):
        for k in range(top_k):
            _row_copy(h_ref, 0, xs_out, 0, sem).wait()
        return c

    lax.fori_loop(0, tm, drain, 0)


def _dispatch(dest2d, hp, xs, tm):
    t, half = hp.shape
    return pl.pallas_call(
        functools.partial(_dispatch_body, top_k=TOP_K),
        out_shape=jax.ShapeDtypeStruct(xs.shape, xs.dtype),
        grid=(t // tm,),
        in_specs=[pl.BlockSpec(memory_space=pl.ANY),
                  pl.BlockSpec((tm, half), lambda m: (m, 0)),
                  pl.BlockSpec(memory_space=pl.ANY)],
        out_specs=pl.BlockSpec(memory_space=pl.ANY),
        scratch_shapes=[pltpu.SMEM((tm * TOP_K,), i32), pltpu.SemaphoreType.DMA(()), pltpu.SemaphoreType.DMA(())],
        input_output_aliases={2: 0},
        compiler_params=_cparams(("arbitrary",)),
        name="dispatch",
    )(dest2d, hp, xs)


def _expert_body(te_ref, nt_ref, x_ref, wg_ref, wu_ref, wd_ref, o_ref, wg_s, wu_s, wd_s):
    step = pl.program_id(0)

    @pl.when(step < nt_ref[0])
    def _():
        e = te_ref[step]
        prev = te_ref[jnp.maximum(step - 1, 0)]

        @pl.when((step == 0) | (e != prev))
        def _():
            wg_s[...] = wg_ref[...].astype(bf16)
            wu_s[...] = wu_ref[...].astype(bf16)
            wd_s[...] = wd_ref[...].astype(bf16)

        lo, hi = _unpack_pair(x_ref[...])
        lo = lo.astype(bf16)
        hi = hi.astype(bf16)
        half = lo.shape[1]
        g = (jnp.dot(lo, wg_s[0:half, :], preferred_element_type=f32)
             + jnp.dot(hi, wg_s[half:, :], preferred_element_type=f32))
        u = (jnp.dot(lo, wu_s[0:half, :], preferred_element_type=f32)
             + jnp.dot(hi, wu_s[half:, :], preferred_element_type=f32))
        hid = (_silu(g) * u).astype(bf16)
        out = jnp.dot(hid, wd_s[...], preferred_element_type=f32)
        o_ref[...] = _pack_pair(out[:, :half], out[:, half:])


def _experts(tile_expert, n_tiles, xs, w_gate, w_up, w_down, tm):
    rows, half = xs.shape
    n_e, d, f = w_gate.shape
    max_tiles = rows // tm

    def tile(i, te, nt):
        return (jnp.minimum(i, nt[0] - 1), 0)

    def wsel(i, te, nt):
        return (te[jnp.minimum(i, nt[0] - 1)], 0, 0)

    return pl.pallas_call(
        _expert_body,
        out_shape=jax.ShapeDtypeStruct((rows, half), u32),
        grid_spec=pltpu.PrefetchScalarGridSpec(
            num_scalar_prefetch=2,
            grid=(max_tiles,),
            in_specs=[pl.BlockSpec((tm, half), tile),
                      pl.BlockSpec((None, d, f), wsel), pl.BlockSpec((None, d, f), wsel),
                      pl.BlockSpec((None, f, d), wsel)],
            out_specs=pl.BlockSpec((tm, half), tile),
            scratch_shapes=[pltpu.VMEM((d, f), bf16), pltpu.VMEM((d, f), bf16), pltpu.VMEM((f, d), bf16)]),
        input_output_aliases={2: 0},
        compiler_params=_cparams(("arbitrary",)),
        name="experts",
    )(tile_expert, n_tiles, xs, w_gate, w_up, w_down)


def _combine_body(dest_hbm, os_hbm, x1_ref, hp_ref, wt_ref, g2_ref, lg_ref, lb_ref, wsg_ref, wsu_ref, wsd_ref,
                  y_ref, dsm, buf, sem_idx, sem, *, top_k):
    step = pl.program_id(0)
    tm = x1_ref.shape[0]
    fetch = pltpu.make_async_copy(dest_hbm.at[step], dsm, sem_idx)
    fetch.start()
    fetch.wait()

    def issue(r, c):
        for k in range(top_k):
            _row_copy(os_hbm, dsm[r * top_k + k], buf.at[k], r, sem).start(priority=k % 2)
        return c

    lax.fori_loop(0, tm, issue, 0)

    lo, hi = _unpack_pair(hp_ref[...])
    lo = lo.astype(bf16)
    hi = hi.astype(bf16)
    half = lo.shape[1]
    g = (jnp.dot(lo, wsg_ref[0:half, :], preferred_element_type=f32)
         + jnp.dot(hi, wsg_ref[half:, :], preferred_element_type=f32))
    u = (jnp.dot(lo, wsu_ref[0:half, :], preferred_element_type=f32)
         + jnp.dot(hi, wsu_ref[half:, :], preferred_element_type=f32))
    shared = jnp.dot((_silu(g) * u).astype(bf16), wsd_ref[...], preferred_element_type=f32)

    def drain(r, c):
        for k in range(top_k):
            _row_copy(os_hbm, 0, buf.at[k], 0, sem).wait()
        return c

    lax.fori_loop(0, tm, drain, 0)

    y_lo = shared[:, :half]
    y_hi = shared[:, half:]
    wt = wt_ref[...]
    for k in range(top_k):
        e_lo, e_hi = _unpack_pair(buf[k])
        wk = wt[:, k:k + 1]
        y_lo = y_lo + wk * e_lo
        y_hi = y_hi + wk * e_hi
    ffn = jnp.concatenate([y_lo, y_hi], axis=1)
    y_ref[...] = _layer_norm(DEEPNORM_ALPHA * x1_ref[...] + (1.0 + g2_ref[...]) * ffn, lg_ref[...], lb_ref[...])


def _combine(dest2d, out_sorted, x1, hp, wts, mod, seq_len, ln_g, ln_b, ws_gate, ws_up, ws_down, tm):
    t, d = x1.shape
    half = d // 2
    sf = ws_gate.shape[1]
    (g2_spec,) = _mod_specs(mod, tm, seq_len, (5,), d)
    row = lambda m: (m, 0)
    const = lambda m: (0, 0)
    return pl.pallas_call(
        functools.partial(_combine_body, top_k=TOP_K),
        out_shape=jax.ShapeDtypeStruct((t, d), f32),
        grid=(t // tm,),
        in_specs=[pl.BlockSpec(memory_space=pl.ANY), pl.BlockSpec(memory_space=pl.ANY),
                  pl.BlockSpec((tm, d), row), pl.BlockSpec((tm, half), row), pl.BlockSpec((tm, LANES), row),
                  g2_spec, pl.BlockSpec((1, d), const), pl.BlockSpec((1, d), const),
                  pl.BlockSpec((d, sf), const), pl.BlockSpec((d, sf), const), pl.BlockSpec((sf, d), const)],
        out_specs=pl.BlockSpec((tm, d), row),
        scratch_shapes=[pltpu.SMEM((tm * TOP_K,), i32), pltpu.VMEM((TOP_K, tm, half), u32),
                        pltpu.SemaphoreType.DMA(()), pltpu.SemaphoreType.DMA(())],
        compiler_params=_cparams(("arbitrary",)),
        name="combine",
    )(dest2d, out_sorted, x1, hp, wts, mod, ln_g.reshape(1, d), ln_b.reshape(1, d), ws_gate, ws_up, ws_down)


def _pick(total, pref, unit=SUBLANES):
    if total <= pref:
        return total
    c = pref - pref % unit
    while total % c:
        c -= unit
    return c


def kernel(x_prompt, x_sample, state_dn_S, state_dn_conv, state_sc_conv, c_prompt, c_sample, w_ada, b_ada, w_in, w_dn_conv, dn_a_log, dn_dt_bias, dn_norm_w, w_sc_conv, w_dn_out, w_sc_out, w_o, ln1_g, ln1_b, w_router, router_bias, w_e_gate, w_e_up, w_e_down, w_s_gate, w_s_up, w_s_down, ln2_g, ln2_b):
    bp, lp, d = x_prompt.shape
    bs, ls, _ = x_sample.shape
    n_heads = dn_a_log.shape[0]
    dk = dn_norm_w.shape[0]
    assert dk == LANES
    dnw = n_heads * dk
    scw = w_sc_conv.shape[1]
    n_e = w_router.shape[1]
    tp, ts = bp * lp, bs * ls

    n_a = 4 * dnw
    o_sc = n_a + 2 * n_heads
    o_gd = o_sc + 3 * scw
    w_main = jnp.concatenate([w_in[:, :n_a], w_in[:, o_gd:o_gd + 2 * d], w_in[:, o_sc:o_gd]], axis=1).astype(bf16)
    c_gd = n_a
    c_gs = c_gd + d
    c_sc = c_gs + d
    hg_p = min(8, n_heads)
    hg_s = n_heads

    def ba_weights(hg):
        n_g = n_heads // hg
        wb = w_in[:, n_a:n_a + n_heads].reshape(d, n_g, hg)
        wa = w_in[:, n_a + n_heads:n_a + 2 * n_heads].reshape(d, n_g, hg)
        pad = jnp.zeros((d, n_g, LANES - 2 * hg), w_in.dtype)
        w = jnp.concatenate([wb, wa, pad], axis=2).reshape(d, n_g * LANES).astype(bf16)
        zl = jnp.zeros((n_g, hg), f32)
        lane = lambda v: jnp.concatenate([zl, v.astype(f32).reshape(n_g, hg), jnp.zeros((n_g, LANES - 2 * hg), f32)], axis=1)
        par = jnp.stack([lane(dn_a_log), lane(dn_dt_bias)] + [jnp.zeros((n_g, LANES), f32)] * (SUBLANES - 2), axis=1)
        return w, par

    w_dn_b = w_dn_out.astype(bf16)
    w_sc_b = w_sc_out.astype(bf16)
    w_o_b = w_o.astype(bf16)
    ws_gate_b = w_s_gate.astype(bf16)
    ws_up_b = w_s_up.astype(bf16)
    ws_down_b = w_s_down.astype(bf16)

    n_c = bp + bs
    m_pad = -(-n_c // SUBLANES) * SUBLANES
    c_all = jnp.concatenate([c_prompt, c_sample, jnp.zeros((m_pad - n_c, d), f32)], axis=0)
    mod = _ada(c_all, w_ada, b_ada)
    mod_p = mod[:bp].reshape(bp, 1, 6 * d)
    mod_s = jnp.repeat(mod[bp:n_c], ls, axis=0)

    def group(x3, mod_g, s0, dn_hist, sc_hist, hg, chunk, lt, cnt_in):
        bn, seq, _ = x3.shape
        t = bn * seq
        x = x3.reshape(t, d)
        tile = (lambda pref: _pick(seq, pref)) if mod_g.ndim == 3 else (lambda pref: _pick(t, pref))
        proj = _modmm(x, mod_g, seq, w_main, bf16, tile(1024), 512)
        w_ba, par = ba_weights(hg)
        ba = _modmm(x, mod_g, seq, w_ba, f32, tile(1024), 512)
        hist8 = jnp.concatenate([jnp.zeros((bn, SUBLANES - dn_hist.shape[1], 3 * dnw), f32), dn_hist], axis=1)
        o_n, s_new, nbq, nbk, nbv = _deltanet(
            proj.reshape(bn, seq, -1), ba.reshape(bn, seq, -1), par, w_dn_conv, dn_norm_w, hist8, s0,
            n_heads=n_heads, heads_per_step=hg, chunk=chunk, lt=lt)
        keep = w_dn_conv.shape[0] - 1
        dn_buf = jnp.concatenate([nbq, nbk, nbv], axis=2)[:, SUBLANES - keep:, :]

        zrow = jnp.zeros((bn, SUBLANES - 2, scw), f32)
        hist_a = jnp.concatenate([sc_hist[:, 1:2], jnp.zeros((bn, 1, scw), f32), zrow], axis=1).reshape(bn * SUBLANES, scw)
        hist_b = jnp.concatenate([sc_hist[:, 0:1], sc_hist[:, 1:2], zrow], axis=1).reshape(bn * SUBLANES, scw)
        rows_sc = seq if seq > SUBLANES else _pick(t, 256)
        scin, tail = _short_conv(proj, c_sc, scw, w_sc_conv, hist_a, hist_b, seq, rows_sc, min(scw, 256))
        sc_keep = w_sc_conv.shape[0] - 1
        sc_buf = tail.reshape(bn, SUBLANES, scw)[:, SUBLANES - sc_keep:, :]

        merged = _merge(o_n.reshape(t, dnw), scin, proj, c_gd, c_gs, w_dn_b, w_sc_b, tile(256))
        x1, hp, idx, wts, pos, cnt = _mix(merged, x, mod_g, seq, w_o_b, ln1_g, ln1_b, w_router, router_bias,
                                          cnt_in, tile(256))
        return dict(x1=x1, hp=hp, idx=idx[:, :TOP_K], wts=wts, pos=pos[:, :TOP_K], cnt=cnt,
                    s=s_new, dn_buf=dn_buf, sc_buf=sc_buf, mod=mod_g, seq=seq, tm_d=tile(256))

    zeros_s = jnp.zeros((bp,) + state_dn_S.shape[1:], f32)
    zeros_dn = jnp.zeros((bp,) + state_dn_conv.shape[1:], f32)
    zeros_sc = jnp.zeros((bp,) + state_sc_conv.shape[1:], f32)
    chunk_p = math.gcd(DN_CHUNK, lp)
    chunk_s = math.gcd(DN_CHUNK, ls)
    gp = group(x_prompt, mod_p, zeros_s, zeros_dn, zeros_sc, hg_p, chunk_p, _pick(lp, 512),
               jnp.zeros((1, n_e), f32))
    gs = group(x_sample, mod_s, state_dn_S, state_dn_conv, state_sc_conv, hg_s, chunk_s, ls, gp["cnt"])

    tm_e = 256
    counts = gs["cnt"][0].astype(i32)
    tiles_per_e = (counts + tm_e - 1) // tm_e
    tile_end = jnp.cumsum(tiles_per_e)
    start_row = (tile_end - tiles_per_e) * tm_e
    n_rows = (tp + ts) * TOP_K
    max_tiles = n_rows // tm_e + n_e
    tile_ids = jnp.arange(max_tiles, dtype=i32)
    tile_expert = jnp.minimum(jnp.sum((tile_end[None, :] <= tile_ids[:, None]).astype(i32), axis=1), n_e - 1)
    n_tiles = tile_end[-1:].astype(i32)

    for g in (gp, gs):
        t = g["x1"].shape[0]
        g["dest"] = (start_row[g["idx"]] + g["pos"]).reshape(t // g["tm_d"], g["tm_d"] * TOP_K)
    xs = jnp.zeros((max_tiles * tm_e, d // 2), u32)
    for g in (gp, gs):
        xs = _dispatch(g["dest"], g["hp"], xs, g["tm_d"])
    out_sorted = _experts(tile_expert, n_tiles, xs, w_e_gate, w_e_up, w_e_down, tm_e)
    ys = [_combine(g["dest"], out_sorted, g["x1"], g["hp"], g["wts"], g["mod"], g["seq"], ln2_g, ln2_b,
                   ws_gate_b, ws_up_b, ws_down_b, g["tm_d"]) for g in (gp, gs)]

    y_p = ys[0].reshape(bp, lp, d)
    y_s = ys[1].reshape(bs, ls, d)
    return (y_p, y_s, gp["s"], gp["dn_buf"], gp["sc_buf"], gs["s"], gs["dn_buf"], gs["sc_buf"])
```

```python
import functools
import math

import jax
import jax.numpy as jnp
from jax import lax
from jax.experimental import pallas as pl
from jax.experimental.pallas import tpu as pltpu

f32 = jnp.float32
bf16 = jnp.bfloat16
i32 = jnp.int32
u32 = jnp.uint32

TOP_K = 8
ROUTED_SCALE = 2.5
DN_CHUNK = 64
DEPTH = 1
DEEPNORM_ALPHA = (2 * DEPTH) ** 0.25
LN_EPS = 1e-5
RMS_EPS = 1e-6

LANES = 128
SUBLANES = 8
VMEM_LIMIT = 56 * 1024 * 1024


def _cparams(sem, vmem=VMEM_LIMIT):
    return pltpu.CompilerParams(dimension_semantics=sem, vmem_limit_bytes=vmem)


def _silu(x):
    return x * jax.nn.sigmoid(x)


def _bdot(a, b):
    return jnp.dot(a.astype(bf16), b.astype(bf16), preferred_element_type=f32)


def _layer_norm(x, g, b):
    mu = jnp.mean(x, axis=-1, keepdims=True)
    xc = x - mu
    var = jnp.mean(xc * xc, axis=-1, keepdims=True)
    return xc * lax.rsqrt(var + LN_EPS) * g + b


def _pack_pair(a, b):
    ab = pltpu.bitcast(a.astype(bf16).astype(f32), u32)
    bb = pltpu.bitcast(b.astype(bf16).astype(f32), u32)
    return (ab >> 16) | (bb & jnp.uint32(0xFFFF0000))


def _unpack_pair(p):
    lo = pltpu.bitcast(p << 16, f32)
    hi = pltpu.bitcast(p & jnp.uint32(0xFFFF0000), f32)
    return lo, hi


def _ada_body(c_ref, w_ref, b_ref, o_ref):
    s = _silu(c_ref[...])
    o_ref[...] = _bdot(s, w_ref[...]) + b_ref[...]


def _ada(c_all, w_ada, b_ada):
    m, d = c_all.shape
    n = w_ada.shape[1]
    tn = _pick(n, 1024, LANES)
    return pl.pallas_call(
        _ada_body,
        out_shape=jax.ShapeDtypeStruct((m, n), f32),
        grid=(n // tn,),
        in_specs=[pl.BlockSpec((m, d), lambda j: (0, 0)),
                  pl.BlockSpec((d, tn), lambda j: (0, j)),
                  pl.BlockSpec((1, tn), lambda j: (0, j))],
        out_specs=pl.BlockSpec((m, tn), lambda j: (0, j)),
        compiler_params=_cparams(("arbitrary",)),
        name="ada",
    )(c_all, w_ada, b_ada.reshape(1, n))


def _modmm_body(x_ref, sc_ref, sh_ref, w_ref, o_ref, h_scr):
    @pl.when(pl.program_id(1) == 0)
    def _():
        h_scr[...] = (x_ref[...] * (1.0 + sc_ref[...]) + sh_ref[...]).astype(bf16)

    o_ref[...] = jnp.dot(h_scr[...], w_ref[...], preferred_element_type=f32).astype(o_ref.dtype)


def _mod_specs(mod, tm, seq_len, cols, d):
    if mod.ndim == 3:
        tiles_per_seq = seq_len // tm
        return [pl.BlockSpec((None, 1, d), lambda m, *_, c=c: (m // tiles_per_seq, 0, c)) for c in cols]
    return [pl.BlockSpec((tm, d), lambda m, *_, c=c: (m, c)) for c in cols]


def _modmm(x, mod, seq_len, w, out_dtype, tm, tn):
    t, d = x.shape
    n = w.shape[1]
    tn = _pick(n, tn, LANES)
    sc_spec, sh_spec = _mod_specs(mod, tm, seq_len, (1, 0), d)
    return pl.pallas_call(
        _modmm_body,
        out_shape=jax.ShapeDtypeStruct((t, n), out_dtype),
        grid=(t // tm, n // tn),
        in_specs=[pl.BlockSpec((tm, d), lambda m, j: (m, 0)), sc_spec, sh_spec,
                  pl.BlockSpec((d, tn), lambda m, j: (0, j))],
        out_specs=pl.BlockSpec((tm, tn), lambda m, j: (m, j)),
        scratch_shapes=[pltpu.VMEM((tm, d), bf16)],
        compiler_params=_cparams(("arbitrary", "arbitrary")),
        name="modmm",
    )(x, mod, mod, w)


def _cumsum_rows(x, n):
    row = lax.broadcasted_iota(i32, x.shape, 0)
    s = 1
    while s < n:
        x = x + jnp.where(row >= s, pltpu.roll(x, s, axis=0), 0.0)
        s *= 2
    return x


def _conv_silu(x, prev, w):
    c = x.shape[0]
    taps = w.shape[0]
    xc = jnp.concatenate([prev, x], axis=0)
    y = w[taps - 1:taps, :] * x
    for i in range(taps - 1):
        back = taps - 1 - i
        y = y + w[i:i + 1, :] * pltpu.roll(xc, back, axis=0)[SUBLANES:SUBLANES + c, :]
    return _silu(y)


def _dn_body(q_ref, k_ref, v_ref, z_ref, ba_ref, par_ref, cwq_ref, cwk_ref, cwv_ref, nw_ref,
             hq_ref, hk_ref, hv_ref, s0_ref,
             o_ref, so_ref, nbq_ref, nbk_ref, nbv_ref,
             s_scr, pq, pk, pv, *, chunk, heads, n_chunks):
    t = pl.program_id(2)
    dk = LANES

    @pl.when(t == 0)
    def _init():
        s_scr[...] = s0_ref[...]
        pq[...] = hq_ref[...]
        pk[...] = hk_ref[...]
        pv[...] = hv_ref[...]

    neg_a = -jnp.exp(par_ref[0:1, :])
    dt_b = par_ref[1:2, :]
    nw = nw_ref[...]
    ii = lax.broadcasted_iota(i32, (chunk, chunk), 0)
    jj = lax.broadcasted_iota(i32, (chunk, chunk), 1)
    causal = ii >= jj
    strict = ii > jj
    eye = (ii == jj).astype(f32)
    zpad = jnp.zeros((LANES - chunk, LANES), f32)

    def do_chunk(ci):
        r0 = pl.multiple_of(ci * chunk, chunk)
        rows = pl.ds(r0, chunk)
        ba = ba_ref[rows, :]
        beta_all = jax.nn.sigmoid(ba)
        xs = ba + dt_b
        softplus = jnp.maximum(xs, 0.0) + jnp.log1p(jnp.exp(-jnp.abs(xs)))
        gc_all = _cumsum_rows(neg_a * softplus, chunk)
        gc_t = jnp.concatenate([gc_all, zpad], axis=0).T
        eg_all = jnp.exp(gc_all)

        xq = q_ref[rows, :].astype(f32)
        xk = k_ref[rows, :].astype(f32)
        xv = v_ref[rows, :].astype(f32)
        qc = _conv_silu(xq, pq[...], cwq_ref[...])
        kc = _conv_silu(xk, pk[...], cwk_ref[...])
        vc = _conv_silu(xv, pv[...], cwv_ref[...])
        pq[...] = xq[chunk - SUBLANES:, :]
        pk[...] = xk[chunk - SUBLANES:, :]
        pv[...] = xv[chunk - SUBLANES:, :]

        hs = range(heads)
        sls = [slice(h * dk, (h + 1) * dk) for h in hs]
        qn = [qc[:, sl] for sl in sls]
        kn = [kc[:, sl] for sl in sls]
        qn = [q * (lax.rsqrt(jnp.sum(q * q, axis=-1, keepdims=True) + RMS_EPS) * (dk ** -0.5)) for q in qn]
        kn = [k * lax.rsqrt(jnp.sum(k * k, axis=-1, keepdims=True) + RMS_EPS) for k in kn]
        beta = [beta_all[:, h:h + 1] for h in hs]
        gc = [gc_all[:, heads + h:heads + h + 1] for h in hs]
        eg = [eg_all[:, heads + h:heads + h + 1] for h in hs]
        qk_kk = [lax.dot_general(jnp.concatenate([qn[h], kn[h]], axis=0).astype(bf16), kn[h].astype(bf16),
                                 (((1,), (1,)), ((), ())), preferred_element_type=f32) for h in hs]
        decay = []
        for h in hs:
            diff = gc[h] - gc_t[heads + h:heads + h + 1, 0:chunk]
            decay.append(jnp.where(causal, jnp.exp(jnp.where(causal, diff, 0.0)), 0.0))
        qk = [qk_kk[h][:chunk, :] * decay[h] for h in hs]
        p = [-jnp.where(strict, qk_kk[h][chunk:, :] * beta[h] * decay[h], 0.0) for h in hs]
        tinv = [eye + p[h] for h in hs]
        n = 1
        while 2 * n < chunk:
            p = [_bdot(p[h], p[h]) for h in hs]
            tinv = [tinv[h] + _bdot(p[h], tinv[h]) for h in hs]
            n *= 2
        sol = [_bdot(tinv[h], jnp.concatenate([vc[:, sls[h]] * beta[h], kn[h] * (beta[h] * eg[h])], axis=1))
               for h in hs]
        s_old = [s_scr[h] for h in hs]
        ks_qs = [_bdot(jnp.concatenate([sol[h][:, dk:], qn[h] * eg[h]], axis=0), s_old[h]) for h in hs]
        u = [sol[h][:, :dk] - ks_qs[h][:chunk, :] for h in hs]
        o = [ks_qs[h][chunk:, :] + _bdot(qk[h], u[h]) for h in hs]
        for h in hs:
            gc_last = gc[h][chunk - 1:chunk, :]
            kd = kn[h] * jnp.exp(gc_last - gc[h])
            s_scr[h] = s_old[h] * jnp.exp(gc_last) + lax.dot_general(
                kd.astype(bf16), u[h].astype(bf16), (((0,), (0,)), ((), ())), preferred_element_type=f32)
        for h in hs:
            zh = z_ref[rows, sls[h]].astype(f32)
            on = o[h] * lax.rsqrt(jnp.mean(o[h] * o[h], axis=-1, keepdims=True) + RMS_EPS) * nw * _silu(zh)
            o_ref[rows, sls[h]] = on.astype(o_ref.dtype)

    if n_chunks == 1:
        do_chunk(0)
    else:
        def loop_body(ci, carry):
            do_chunk(ci)
            return carry
        lax.fori_loop(0, n_chunks, loop_body, 0)

    @pl.when(t == pl.num_programs(2) - 1)
    def _fin():
        so_ref[...] = s_scr[...]
        nbq_ref[...] = pq[...]
        nbk_ref[...] = pk[...]
        nbv_ref[...] = pv[...]


def _deltanet(proj3, ba3, par, w_conv, norm_w, hist8, s0, *, n_heads, heads_per_step, chunk, lt):
    bn, seq, _ = proj3.shape
    dk = LANES
    dnw = n_heads * dk
    hg = heads_per_step
    gw = hg * dk
    n_g = n_heads // hg
    n_t = seq // lt
    kern = functools.partial(_dn_body, chunk=chunk, heads=hg, n_chunks=lt // chunk)

    def col(off):
        return pl.BlockSpec((None, lt, gw), lambda b, g, t, off=off: (b, t, off * n_g + g))

    def cw(off):
        return pl.BlockSpec((w_conv.shape[0], gw), lambda b, g, t, off=off: (0, off * n_g + g))

    def hist(off):
        return pl.BlockSpec((None, SUBLANES, gw), lambda b, g, t, off=off: (b, 0, off * n_g + g))

    nb_spec = pl.BlockSpec((None, SUBLANES, gw), lambda b, g, t: (b, 0, g))
    nb_shape = jax.ShapeDtypeStruct((bn, SUBLANES, dnw), f32)
    return pl.pallas_call(
        kern,
        out_shape=(jax.ShapeDtypeStruct((bn, seq, dnw), bf16),
                   jax.ShapeDtypeStruct(s0.shape, f32), nb_shape, nb_shape, nb_shape),
        grid=(bn, n_g, n_t),
        in_specs=[col(0), col(1), col(2), col(3),
                  pl.BlockSpec((None, lt, LANES), lambda b, g, t: (b, t, g)),
                  pl.BlockSpec((None, SUBLANES, LANES), lambda b, g, t: (g, 0, 0)),
                  cw(0), cw(1), cw(2),
                  pl.BlockSpec((1, dk), lambda b, g, t: (0, 0)),
                  hist(0), hist(1), hist(2),
                  pl.BlockSpec((None, hg, dk, dk), lambda b, g, t: (b, g, 0, 0))],
        out_specs=(pl.BlockSpec((None, lt, gw), lambda b, g, t: (b, t, g)),
                   pl.BlockSpec((None, hg, dk, dk), lambda b, g, t: (b, g, 0, 0)),
                   nb_spec, nb_spec, nb_spec),
        scratch_shapes=[pltpu.VMEM((hg, dk, dk), f32),
                        pltpu.VMEM((SUBLANES, gw), f32), pltpu.VMEM((SUBLANES, gw), f32),
                        pltpu.VMEM((SUBLANES, gw), f32)],
        compiler_params=_cparams(("arbitrary", "arbitrary", "arbitrary")),
        name="deltanet",
    )(proj3, proj3, proj3, proj3, ba3, par, w_conv, w_conv, w_conv, norm_w.reshape(1, dk),
      hist8, hist8, hist8, s0)


def _sc_body(b_ref, c_ref, x_ref, w_ref, ha_ref, hb_ref, o_ref, tail_ref, *, seq_len):
    u = c_ref[...].astype(f32) * x_ref[...].astype(f32)
    rows = u.shape[0]
    tin = lax.broadcasted_iota(i32, u.shape, 0) % seq_len
    w = w_ref[...]
    conv = (w[2:3, :] * u
            + w[1:2, :] * jnp.where(tin >= 1, pltpu.roll(u, 1, axis=0), 0.0)
            + w[0:1, :] * jnp.where(tin >= 2, pltpu.roll(u, 2, axis=0), 0.0))
    corr = w[1:2, :] * ha_ref[...] + w[0:1, :] * hb_ref[...]
    bv = b_ref[...].astype(f32)
    if corr.shape[0] == rows:
        o_ref[...] = (bv * (conv + corr)).astype(o_ref.dtype)
        tail_ref[...] = u
    else:
        o_ref[...] = (bv * conv).astype(o_ref.dtype)
        o_ref[0:SUBLANES, :] = (bv[0:SUBLANES, :] * (conv[0:SUBLANES, :] + corr)).astype(o_ref.dtype)
        tail_ref[...] = u[rows - SUBLANES:, :]


def _short_conv(proj, col0, scw, w_sc, hist_a, hist_b, seq_len, rows_per_step, tw):
    t = proj.shape[0]
    n_seq = t // seq_len
    hr = SUBLANES if rows_per_step == seq_len else rows_per_step
    n_w = scw // tw
    cb = col0 // tw

    def pin(off):
        return pl.BlockSpec((rows_per_step, tw), lambda r, j, off=off: (r, cb + off * n_w + j))

    h_spec = pl.BlockSpec((hr, tw), lambda r, j: (r, j))
    return pl.pallas_call(
        functools.partial(_sc_body, seq_len=seq_len),
        out_shape=(jax.ShapeDtypeStruct((t, scw), bf16),
                   jax.ShapeDtypeStruct((n_seq * SUBLANES, scw), f32)),
        grid=(t // rows_per_step, n_w),
        in_specs=[pin(0), pin(1), pin(2), pl.BlockSpec((w_sc.shape[0], tw), lambda r, j: (0, j)),
                  h_spec, h_spec],
        out_specs=(pl.BlockSpec((rows_per_step, tw), lambda r, j: (r, j)),
                   pl.BlockSpec((hr, tw), lambda r, j: (r, j))),
        compiler_params=_cparams(("arbitrary", "arbitrary")),
        name="short_conv",
    )(proj, proj, proj, w_sc, hist_a, hist_b)


def _merge_body(o_ref, s_ref, gd_ref, gs_ref, wdn_ref, wsc_ref, m_ref):
    ydn = jnp.dot(o_ref[...], wdn_ref[...], preferred_element_type=f32)
    ysc = jnp.dot(s_ref[...], wsc_ref[...], preferred_element_type=f32)
    m = jax.nn.sigmoid(gd_ref[...].astype(f32)) * ydn + jax.nn.sigmoid(gs_ref[...].astype(f32)) * ysc
    m_ref[...] = m.astype(m_ref.dtype)


def _merge(o_n, scin, proj, gd_col, gs_col, w_dn, w_sc, tm):
    t, dnw = o_n.shape
    scw = scin.shape[1]
    d = w_dn.shape[1]
    return pl.pallas_call(
        _merge_body,
        out_shape=jax.ShapeDtypeStruct((t, d), bf16),
        grid=(t // tm,),
        in_specs=[pl.BlockSpec((tm, dnw), lambda m: (m, 0)),
                  pl.BlockSpec((tm, scw), lambda m: (m, 0)),
                  pl.BlockSpec((tm, d), lambda m: (m, gd_col // d)),
                  pl.BlockSpec((tm, d), lambda m: (m, gs_col // d)),
                  pl.BlockSpec((dnw, d), lambda m: (0, 0)),
                  pl.BlockSpec((scw, d), lambda m: (0, 0))],
        out_specs=pl.BlockSpec((tm, d), lambda m: (m, 0)),
        compiler_params=_cparams(("arbitrary",)),
        name="merge",
    )(o_n, scin, proj, proj, w_dn, w_sc)


def _split_dot3(a, b):
    a_hi = a.astype(bf16)
    a_lo = (a - a_hi.astype(f32)).astype(bf16)
    b_hi = b.astype(bf16)
    b_lo = (b - b_hi.astype(f32)).astype(bf16)
    return (jnp.dot(a_hi, b_hi, preferred_element_type=f32) + jnp.dot(a_hi, b_lo, preferred_element_type=f32)
            + jnp.dot(a_lo, b_hi, preferred_element_type=f32))


def _mix_body(m_ref, x_ref, g1_ref, sh2_ref, sc2_ref, wo_ref, lg_ref, lb_ref, wr_ref, rb_ref, cnt_in_ref,
              x1_ref, hp_ref, idx_ref, wt_ref, pos_ref, cnt_ref, cnt_scr, *, top_k):
    step = pl.program_id(0)

    @pl.when(step == 0)
    def _():
        cnt_scr[...] = cnt_in_ref[...]

    mix = jnp.dot(m_ref[...], wo_ref[...], preferred_element_type=f32)
    x1 = _layer_norm(DEEPNORM_ALPHA * x_ref[...] + (1.0 + g1_ref[...]) * mix, lg_ref[...], lb_ref[...])
    x1_ref[...] = x1
    h2 = x1 * (1.0 + sc2_ref[...]) + sh2_ref[...]
    half = h2.shape[1] // 2
    hp_ref[...] = _pack_pair(h2[:, :half], h2[:, half:])

    tm = h2.shape[0]
    n_e = wr_ref.shape[1]
    scores = jax.nn.sigmoid(_split_dot3(h2, wr_ref[...]))
    work = scores + rb_ref[...]
    lane_e = lax.broadcasted_iota(i32, (tm, n_e), 1).astype(f32)
    lane_o = lax.broadcasted_iota(i32, (tm, LANES), 1)
    sel = jnp.zeros((tm, n_e), f32)
    picks = []
    wsum = jnp.zeros((tm, 1), f32)
    for _ in range(top_k):
        mx = jnp.max(work, axis=-1, keepdims=True)
        idx = jnp.min(jnp.where(work == mx, lane_e, float(n_e)), axis=-1, keepdims=True)
        hit = lane_e == idx
        wk = jnp.sum(jnp.where(hit, scores, 0.0), axis=-1, keepdims=True)
        work = jnp.where(hit, -jnp.inf, work)
        sel = jnp.where(hit, 1.0, sel)
        wsum = wsum + wk
        picks.append((idx, hit, wk))

    ri = lax.broadcasted_iota(i32, (tm, tm), 0)
    ci = lax.broadcasted_iota(i32, (tm, tm), 1)
    tri = jnp.where(ri > ci, 1.0, 0.0).astype(bf16)
    before = jnp.dot(tri, sel.astype(bf16), preferred_element_type=f32) + cnt_scr[...]
    cnt_scr[...] = cnt_scr[...] + jnp.sum(sel, axis=0, keepdims=True)
    cnt_ref[...] = cnt_scr[...]

    idx_o = jnp.zeros((tm, LANES), i32)
    wt_o = jnp.zeros((tm, LANES), f32)
    pos_o = jnp.zeros((tm, LANES), i32)
    for k, (idx, hit, wk) in enumerate(picks):
        pk = jnp.sum(jnp.where(hit, before, 0.0), axis=-1, keepdims=True).astype(i32)
        idx_o = jnp.where(lane_o == k, idx.astype(i32), idx_o)
        wt_o = jnp.where(lane_o == k, wk / wsum * ROUTED_SCALE, wt_o)
        pos_o = jnp.where(lane_o == k, pk, pos_o)
    idx_ref[...] = idx_o
    wt_ref[...] = wt_o
    pos_ref[...] = pos_o


def _mix(merged, x, mod, seq_len, w_o, ln_g, ln_b, w_router, router_bias, cnt_in, tm):
    t, d = x.shape
    n_e = w_router.shape[1]
    g1_spec, sh2_spec, sc2_spec = _mod_specs(mod, tm, seq_len, (2, 3, 4), d)
    row = lambda m: (m, 0)
    const = lambda m: (0, 0)
    lane_out = jax.ShapeDtypeStruct((t, LANES), i32)
    return pl.pallas_call(
        functools.partial(_mix_body, top_k=TOP_K),
        out_shape=(jax.ShapeDtypeStruct((t, d), f32), jax.ShapeDtypeStruct((t, d // 2), u32),
                   lane_out, jax.ShapeDtypeStruct((t, LANES), f32), lane_out,
                   jax.ShapeDtypeStruct((1, n_e), f32)),
        grid=(t // tm,),
        in_specs=[pl.BlockSpec((tm, d), row), pl.BlockSpec((tm, d), row), g1_spec, sh2_spec, sc2_spec,
                  pl.BlockSpec((d, d), const), pl.BlockSpec((1, d), const), pl.BlockSpec((1, d), const),
                  pl.BlockSpec((d, n_e), const), pl.BlockSpec((1, n_e), const), pl.BlockSpec((1, n_e), const)],
        out_specs=(pl.BlockSpec((tm, d), row), pl.BlockSpec((tm, d // 2), row),
                   pl.BlockSpec((tm, LANES), row), pl.BlockSpec((tm, LANES), row), pl.BlockSpec((tm, LANES), row),
                   pl.BlockSpec((1, n_e), const)),
        scratch_shapes=[pltpu.VMEM((1, n_e), f32)],
        compiler_params=_cparams(("arbitrary",)),
        name="mix",
    )(merged, x, mod, mod, mod, w_o, ln_g.reshape(1, d), ln_b.reshape(1, d), w_router,
      router_bias.reshape(1, n_e), cnt_in)


def _row_copy(src, src_row, dst, dst_row, sem):
    return pltpu.make_async_copy(src.at[pl.ds(src_row, 1)], dst.at[pl.ds(dst_row, 1)], sem)


def _dispatch_body(dest_hbm, h_ref, xs_in, xs_out, dsm, sem_idx, sem, *, top_k):
    del xs_in
    step = pl.program_id(0)
    tm = h_ref.shape[0]
    fetch = pltpu.make_async_copy(dest_hbm.at[step], dsm, sem_idx)
    fetch.start()
    fetch.wait()

    def issue(r8, c):
        base = pl.multiple_of(r8 * SUBLANES, SUBLANES)
        for j in range(SUBLANES):
            for k in range(top_k):
                _row_copy(h_ref, base + j, xs_out, dsm[(base + j) * top_k + k], sem).start(priority=k % 2)
        return c

    lax.fori_loop(0, tm // SUBLANES, issue, 0)

    def drain(r, c):
        for k in range(top_k):
            _row_copy(h_ref, 0, xs_out, 0, sem).wait()
        return c

    lax.fori_loop(0, tm, drain, 0)


def _dispatch(dest2d, hp, xs, tm):
    t, half = hp.shape
    return pl.pallas_call(
        functools.partial(_dispatch_body, top_k=TOP_K),
        out_shape=jax.ShapeDtypeStruct(xs.shape, xs.dtype),
        grid=(t // tm,),
        in_specs=[pl.BlockSpec(memory_space=pl.ANY),
                  pl.BlockSpec((tm, half), lambda m: (m, 0)),
                  pl.BlockSpec(memory_space=pl.ANY)],
        out_specs=pl.BlockSpec(memory_space=pl.ANY),
        scratch_shapes=[pltpu.SMEM((tm * TOP_K,), i32), pltpu.SemaphoreType.DMA(()), pltpu.SemaphoreType.DMA(())],
        input_output_aliases={2: 0},
        compiler_params=_cparams(("arbitrary",)),
        name="dispatch",
    )(dest2d, hp, xs)


def _expert_body(te_ref, nt_ref, first_ref, slot_ref, nxt_ref, x_ref, wg_hbm, wu_hbm, wd_hbm, o_ref,
                 wg_f, wu_f, wd_f, wg_s, wu_s, wd_s, sems):
    step = pl.program_id(0)

    def weight_copies(e, s):
        return (pltpu.make_async_copy(wg_hbm.at[e], wg_f.at[s], sems.at[s, 0]),
                pltpu.make_async_copy(wu_hbm.at[e], wu_f.at[s], sems.at[s, 1]),
                pltpu.make_async_copy(wd_hbm.at[e], wd_f.at[s], sems.at[s, 2]))

    @pl.when(step < nt_ref[0])
    def _():
        e = te_ref[step]
        s = slot_ref[step]

        @pl.when(first_ref[step] == 1)
        def _():
            @pl.when(step == 0)
            def _():
                for c in weight_copies(e, s):
                    c.start()

            for c in weight_copies(e, s):
                c.wait()
            nxt = nxt_ref[step]

            @pl.when(nxt >= 0)
            def _():
                for c in weight_copies(nxt, 1 - s):
                    c.start()

            wg_s[...] = wg_f[s].astype(bf16)
            wu_s[...] = wu_f[s].astype(bf16)
            wd_s[...] = wd_f[s].astype(bf16)

        lo, hi = _unpack_pair(x_ref[...])
        lo = lo.astype(bf16)
        hi = hi.astype(bf16)
        half = lo.shape[1]
        g = (jnp.dot(lo, wg_s[0:half, :], preferred_element_type=f32)
             + jnp.dot(hi, wg_s[half:, :], preferred_element_type=f32))
        u = (jnp.dot(lo, wu_s[0:half, :], preferred_element_type=f32)
             + jnp.dot(hi, wu_s[half:, :], preferred_element_type=f32))
        hid = (_silu(g) * u).astype(bf16)
        out = jnp.dot(hid, wd_s[...], preferred_element_type=f32)
        o_ref[...] = _pack_pair(out[:, :half], out[:, half:])


def _experts(tile_expert, n_tiles, first, slot, nxt, xs, w_gate, w_up, w_down, tm):
    rows, half = xs.shape
    n_e, d, f = w_gate.shape
    max_tiles = rows // tm

    def tile(i, te, nt, *_):
        return (jnp.minimum(i, nt[0] - 1), 0)

    hbm = pl.BlockSpec(memory_space=pl.ANY)
    return pl.pallas_call(
        _expert_body,
        out_shape=jax.ShapeDtypeStruct((rows, half), u32),
        grid_spec=pltpu.PrefetchScalarGridSpec(
            num_scalar_prefetch=5,
            grid=(max_tiles,),
            in_specs=[pl.BlockSpec((tm, half), tile), hbm, hbm, hbm],
            out_specs=pl.BlockSpec((tm, half), tile),
            scratch_shapes=[pltpu.VMEM((2, d, f), f32), pltpu.VMEM((2, d, f), f32), pltpu.VMEM((2, f, d), f32),
                            pltpu.VMEM((d, f), bf16), pltpu.VMEM((d, f), bf16), pltpu.VMEM((f, d), bf16),
                            pltpu.SemaphoreType.DMA((2, 3))]),
        input_output_aliases={5: 0},
        compiler_params=_cparams(("arbitrary",)),
        name="experts",
    )(tile_expert, n_tiles, first, slot, nxt, xs, w_gate, w_up, w_down)


def _combine_body(dest_hbm, os_hbm, x1_ref, hp_ref, wt_ref, g2_ref, lg_ref, lb_ref, wsg_ref, wsu_ref, wsd_ref,
                  y_ref, dsm0, dsm1, buf, sem_idx, sems, *, top_k):
    step = pl.program_id(0)
    tm = x1_ref.shape[0]
    slot = step % 2

    def gather(tile, s):
        dsm = (dsm0, dsm1)[s]
        fetch = pltpu.make_async_copy(dest_hbm.at[tile], dsm, sem_idx)
        fetch.start()
        fetch.wait()

        def issue(r8, c):
            base = pl.multiple_of(r8 * SUBLANES, SUBLANES)
            for j in range(SUBLANES):
                for k in range(top_k):
                    src_row = dsm[(base + j) * top_k + k]
                    _row_copy(os_hbm, src_row, buf.at[s, k], base + j, sems.at[s]).start(priority=k % 2)
            return c

        lax.fori_loop(0, tm // SUBLANES, issue, 0)

    @pl.when(step == 0)
    def _():
        gather(0, 0)

    has_next = step + 1 < pl.num_programs(0)
    for s in (0, 1):
        @pl.when(has_next & (slot == 1 - s))
        def _(s=s):
            gather(step + 1, s)

    lo, hi = _unpack_pair(hp_ref[...])
    lo = lo.astype(bf16)
    hi = hi.astype(bf16)
    half = lo.shape[1]
    g = (jnp.dot(lo, wsg_ref[0:half, :], preferred_element_type=f32)
         + jnp.dot(hi, wsg_ref[half:, :], preferred_element_type=f32))
    u = (jnp.dot(lo, wsu_ref[0:half, :], preferred_element_type=f32)
         + jnp.dot(hi, wsu_ref[half:, :], preferred_element_type=f32))
    shared = jnp.dot((_silu(g) * u).astype(bf16), wsd_ref[...], preferred_element_type=f32)

    for s in (0, 1):
        @pl.when(slot == s)
        def _(s=s):
            def drain(r, c):
                for k in range(top_k):
                    _row_copy(os_hbm, 0, buf.at[s, k], 0, sems.at[s]).wait()
                return c

            lax.fori_loop(0, tm, drain, 0)

    y_lo = shared[:, :half]
    y_hi = shared[:, half:]
    wt = wt_ref[...]
    for k in range(top_k):
        e_lo, e_hi = _unpack_pair(buf[slot, k])
        wk = wt[:, k:k + 1]
        y_lo = y_lo + wk * e_lo
        y_hi = y_hi + wk * e_hi
    ffn = jnp.concatenate([y_lo, y_hi], axis=1)
    y_ref[...] = _layer_norm(DEEPNORM_ALPHA * x1_ref[...] + (1.0 + g2_ref[...]) * ffn, lg_ref[...], lb_ref[...])


def _combine(dest2d, out_sorted, x1, hp, wts, mod, seq_len, ln_g, ln_b, ws_gate, ws_up, ws_down, tm):
    t, d = x1.shape
    half = d // 2
    sf = ws_gate.shape[1]
    (g2_spec,) = _mod_specs(mod, tm, seq_len, (5,), d)
    row = lambda m: (m, 0)
    const = lambda m: (0, 0)
    return pl.pallas_call(
        functools.partial(_combine_body, top_k=TOP_K),
        out_shape=jax.ShapeDtypeStruct((t, d), f32),
        grid=(t // tm,),
        in_specs=[pl.BlockSpec(memory_space=pl.ANY), pl.BlockSpec(memory_space=pl.ANY),
                  pl.BlockSpec((tm, d), row), pl.BlockSpec((tm, half), row), pl.BlockSpec((tm, LANES), row),
                  g2_spec, pl.BlockSpec((1, d), const), pl.BlockSpec((1, d), const),
                  pl.BlockSpec((d, sf), const), pl.BlockSpec((d, sf), const), pl.BlockSpec((sf, d), const)],
        out_specs=pl.BlockSpec((tm, d), row),
        scratch_shapes=[pltpu.SMEM((tm * TOP_K,), i32), pltpu.SMEM((tm * TOP_K,), i32),
                        pltpu.VMEM((2, TOP_K, tm, half), u32),
                        pltpu.SemaphoreType.DMA(()), pltpu.SemaphoreType.DMA((2,))],
        compiler_params=_cparams(("arbitrary",)),
        name="combine",
    )(dest2d, out_sorted, x1, hp, wts, mod, ln_g.reshape(1, d), ln_b.reshape(1, d), ws_gate, ws_up, ws_down)


def _pick(total, pref, unit=SUBLANES):
    if total <= pref:
        return total
    c = pref - pref % unit
    while total % c:
        c -= unit
    return c


def kernel(x_prompt, x_sample, state_dn_S, state_dn_conv, state_sc_conv, c_prompt, c_sample, w_ada, b_ada, w_in, w_dn_conv, dn_a_log, dn_dt_bias, dn_norm_w, w_sc_conv, w_dn_out, w_sc_out, w_o, ln1_g, ln1_b, w_router, router_bias, w_e_gate, w_e_up, w_e_down, w_s_gate, w_s_up, w_s_down, ln2_g, ln2_b):
    bp, lp, d = x_prompt.shape
    bs, ls, _ = x_sample.shape
    n_heads = dn_a_log.shape[0]
    dk = dn_norm_w.shape[0]
    assert dk == LANES
    dnw = n_heads * dk
    scw = w_sc_conv.shape[1]
    n_e = w_router.shape[1]
    tp, ts = bp * lp, bs * ls

    n_a = 4 * dnw
    o_sc = n_a + 2 * n_heads
    o_gd = o_sc + 3 * scw
    w_main = jnp.concatenate([w_in[:, :n_a], w_in[:, o_gd:o_gd + 2 * d], w_in[:, o_sc:o_gd]], axis=1).astype(bf16)
    c_gd = n_a
    c_gs = c_gd + d
    c_sc = c_gs + d
    hg_p = min(16, n_heads)
    hg_s = n_heads

    def ba_weights(hg):
        n_g = n_heads // hg
        wb = w_in[:, n_a:n_a + n_heads].reshape(d, n_g, hg)
        wa = w_in[:, n_a + n_heads:n_a + 2 * n_heads].reshape(d, n_g, hg)
        pad = jnp.zeros((d, n_g, LANES - 2 * hg), w_in.dtype)
        w = jnp.concatenate([wb, wa, pad], axis=2).reshape(d, n_g * LANES).astype(bf16)
        zl = jnp.zeros((n_g, hg), f32)
        lane = lambda v: jnp.concatenate([zl, v.astype(f32).reshape(n_g, hg), jnp.zeros((n_g, LANES - 2 * hg), f32)], axis=1)
        par = jnp.stack([lane(dn_a_log), lane(dn_dt_bias)] + [jnp.zeros((n_g, LANES), f32)] * (SUBLANES - 2), axis=1)
        return w, par

    w_dn_b = w_dn_out.astype(bf16)
    w_sc_b = w_sc_out.astype(bf16)
    w_o_b = w_o.astype(bf16)
    ws_gate_b = w_s_gate.astype(bf16)
    ws_up_b = w_s_up.astype(bf16)
    ws_down_b = w_s_down.astype(bf16)

    n_c = bp + bs
    m_pad = -(-n_c // SUBLANES) * SUBLANES
    c_all = jnp.concatenate([c_prompt, c_sample, jnp.zeros((m_pad - n_c, d), f32)], axis=0)
    mod = _ada(c_all, w_ada, b_ada)
    mod_p = mod[:bp].reshape(bp, 1, 6 * d)
    mod_s = jnp.repeat(mod[bp:n_c], ls, axis=0)

    def group(x3, mod_g, s0, dn_hist, sc_hist, hg, chunk, lt, cnt_in):
        bn, seq, _ = x3.shape
        t = bn * seq
        x = x3.reshape(t, d)
        tile = (lambda pref: _pick(seq, pref)) if mod_g.ndim == 3 else (lambda pref: _pick(t, pref))
        proj = _modmm(x, mod_g, seq, w_main, bf16, tile(1024), 512)
        w_ba, par = ba_weights(hg)
        ba = _modmm(x, mod_g, seq, w_ba, f32, tile(1024), 512)
        hist8 = jnp.concatenate([jnp.zeros((bn, SUBLANES - dn_hist.shape[1], 3 * dnw), f32), dn_hist], axis=1)
        o_n, s_new, nbq, nbk, nbv = _deltanet(
            proj.reshape(bn, seq, -1), ba.reshape(bn, seq, -1), par, w_dn_conv, dn_norm_w, hist8, s0,
            n_heads=n_heads, heads_per_step=hg, chunk=chunk, lt=lt)
        keep = w_dn_conv.shape[0] - 1
        dn_buf = jnp.concatenate([nbq, nbk, nbv], axis=2)[:, SUBLANES - keep:, :]

        zrow = jnp.zeros((bn, SUBLANES - 2, scw), f32)
        hist_a = jnp.concatenate([sc_hist[:, 1:2], jnp.zeros((bn, 1, scw), f32), zrow], axis=1).reshape(bn * SUBLANES, scw)
        hist_b = jnp.concatenate([sc_hist[:, 0:1], sc_hist[:, 1:2], zrow], axis=1).reshape(bn * SUBLANES, scw)
        rows_sc = seq if seq > SUBLANES else _pick(t, 256)
        scin, tail = _short_conv(proj, c_sc, scw, w_sc_conv, hist_a, hist_b, seq, rows_sc, min(scw, 256))
        sc_keep = w_sc_conv.shape[0] - 1
        sc_buf = tail.reshape(bn, SUBLANES, scw)[:, SUBLANES - sc_keep:, :]

        merged = _merge(o_n.reshape(t, dnw), scin, proj, c_gd, c_gs, w_dn_b, w_sc_b, tile(256))
        x1, hp, idx, wts, pos, cnt = _mix(merged, x, mod_g, seq, w_o_b, ln1_g, ln1_b, w_router, router_bias,
                                          cnt_in, tile(256))
        return dict(x1=x1, hp=hp, idx=idx[:, :TOP_K], wts=wts, pos=pos[:, :TOP_K], cnt=cnt,
                    s=s_new, dn_buf=dn_buf, sc_buf=sc_buf, mod=mod_g, seq=seq, tm_d=tile(256))

    zeros_s = jnp.zeros((bp,) + state_dn_S.shape[1:], f32)
    zeros_dn = jnp.zeros((bp,) + state_dn_conv.shape[1:], f32)
    zeros_sc = jnp.zeros((bp,) + state_sc_conv.shape[1:], f32)
    chunk_p = math.gcd(DN_CHUNK, lp)
    chunk_s = math.gcd(DN_CHUNK, ls)
    gp = group(x_prompt, mod_p, zeros_s, zeros_dn, zeros_sc, hg_p, chunk_p, _pick(lp, 512),
               jnp.zeros((1, n_e), f32))
    gs = group(x_sample, mod_s, state_dn_S, state_dn_conv, state_sc_conv, hg_s, chunk_s, ls, gp["cnt"])

    tm_e = 256
    counts = gs["cnt"][0].astype(i32)
    tiles_per_e = (counts + tm_e - 1) // tm_e
    tile_end = jnp.cumsum(tiles_per_e)
    start_row = (tile_end - tiles_per_e) * tm_e
    n_rows = (tp + ts) * TOP_K
    max_tiles = n_rows // tm_e + n_e
    tile_ids = jnp.arange(max_tiles, dtype=i32)
    tile_expert = jnp.minimum(jnp.sum((tile_end[None, :] <= tile_ids[:, None]).astype(i32), axis=1), n_e - 1)
    n_tiles = tile_end[-1:].astype(i32)
    e_ids = jnp.arange(n_e, dtype=i32)
    nonempty = tiles_per_e > 0
    ordinal = jnp.cumsum(nonempty.astype(i32)) - 1
    later = (e_ids[None, :] > e_ids[:, None]) & nonempty[None, :]
    next_e = jnp.min(jnp.where(later, e_ids[None, :], n_e), axis=1)
    next_e = jnp.where(next_e == n_e, -1, next_e).astype(i32)
    onehot_te = tile_expert[:, None] == e_ids[None, :]
    lookup = lambda table: jnp.sum(jnp.where(onehot_te, table[None, :], 0), axis=1).astype(i32)
    first = (tile_ids == lookup(tile_end - tiles_per_e)).astype(i32)
    slot = lookup(ordinal) % 2
    nxt = lookup(next_e)

    for g in (gp, gs):
        t = g["x1"].shape[0]
        idx_flat = g["idx"].reshape(-1)
        start_flat = jnp.sum(jnp.where(idx_flat[:, None] == e_ids[None, :], start_row[None, :], 0), axis=1)
        g["dest"] = (start_flat + g["pos"].reshape(-1)).astype(i32).reshape(t // g["tm_d"], g["tm_d"] * TOP_K)
    xs = jnp.zeros((max_tiles * tm_e, d // 2), u32)
    for g in (gp, gs):
        xs = _dispatch(g["dest"], g["hp"], xs, g["tm_d"])
    out_sorted = _experts(tile_expert, n_tiles, first, slot, nxt, xs, w_e_gate, w_e_up, w_e_down, tm_e)
    ys = [_combine(g["dest"], out_sorted, g["x1"], g["hp"], g["wts"], g["mod"], g["seq"], ln2_g, ln2_b,
                   ws_gate_b, ws_up_b, ws_down_b, g["tm_d"]) for g in (gp, gs)]

    y_p = ys[0].reshape(bp, lp, d)
    y_s = ys[1].reshape(bs, ls, d)
    return (y_p, y_s, gp["s"], gp["dn_buf"], gp["sc_buf"], gs["s"], gs["dn_buf"], gs["sc_buf"])
```

```python
import functools
import math

import jax
import jax.numpy as jnp
from jax import lax
from jax.experimental import pallas as pl
from jax.experimental.pallas import tpu as pltpu

f32 = jnp.float32
bf16 = jnp.bfloat16
i32 = jnp.int32
u32 = jnp.uint32

TOP_K = 8
ROUTED_SCALE = 2.5
DN_CHUNK = 64
DEPTH = 1
DEEPNORM_ALPHA = (2 * DEPTH) ** 0.25
LN_EPS = 1e-5
RMS_EPS = 1e-6

LANES = 128
SUBLANES = 8
VMEM_LIMIT = 56 * 1024 * 1024


def _cparams(sem, vmem=VMEM_LIMIT):
    return pltpu.CompilerParams(dimension_semantics=sem, vmem_limit_bytes=vmem)


def _silu(x):
    return x * jax.nn.sigmoid(x)


def _bdot(a, b):
    return jnp.dot(a.astype(bf16), b.astype(bf16), preferred_element_type=f32)


def _layer_norm(x, g, b):
    mu = jnp.mean(x, axis=-1, keepdims=True)
    xc = x - mu
    var = jnp.mean(xc * xc, axis=-1, keepdims=True)
    return xc * lax.rsqrt(var + LN_EPS) * g + b


def _pack_pair(a, b):
    ab = pltpu.bitcast(a.astype(bf16).astype(f32), u32)
    bb = pltpu.bitcast(b.astype(bf16).astype(f32), u32)
    return (ab >> 16) | (bb & jnp.uint32(0xFFFF0000))


def _unpack_pair(p):
    lo = pltpu.bitcast(p << 16, f32)
    hi = pltpu.bitcast(p & jnp.uint32(0xFFFF0000), f32)
    return lo, hi


def _ada_body(c_ref, w_ref, b_ref, o_ref):
    s = _silu(c_ref[...])
    o_ref[...] = _bdot(s, w_ref[...]) + b_ref[...]


def _ada(c_all, w_ada, b_ada):
    m, d = c_all.shape
    n = w_ada.shape[1]
    tn = _pick(n, 1024, LANES)
    return pl.pallas_call(
        _ada_body,
        out_shape=jax.ShapeDtypeStruct((m, n), f32),
        grid=(n // tn,),
        in_specs=[pl.BlockSpec((m, d), lambda j: (0, 0)),
                  pl.BlockSpec((d, tn), lambda j: (0, j)),
                  pl.BlockSpec((1, tn), lambda j: (0, j))],
        out_specs=pl.BlockSpec((m, tn), lambda j: (0, j)),
        compiler_params=_cparams(("arbitrary",)),
        name="ada",
    )(c_all, w_ada, b_ada.reshape(1, n))


def _modmm_body(x_ref, sc_ref, sh_ref, w_ref, o_ref, h_scr):
    @pl.when(pl.program_id(1) == 0)
    def _():
        h_scr[...] = (x_ref[...] * (1.0 + sc_ref[...]) + sh_ref[...]).astype(bf16)

    o_ref[...] = jnp.dot(h_scr[...], w_ref[...].astype(bf16), preferred_element_type=f32).astype(o_ref.dtype)


def _mod_specs(mod, tm, seq_len, cols, d):
    if mod.ndim == 3:
        tiles_per_seq = seq_len // tm
        return [pl.BlockSpec((None, 1, d), lambda m, *_, c=c: (m // tiles_per_seq, 0, c)) for c in cols]
    return [pl.BlockSpec((tm, d), lambda m, *_, c=c: (m, c)) for c in cols]


def _modmm(x, mod, seq_len, w, n, out_dtype, tm, tn):
    t, d = x.shape
    tn = _pick(n, tn, LANES)
    sc_spec, sh_spec = _mod_specs(mod, tm, seq_len, (1, 0), d)
    return pl.pallas_call(
        _modmm_body,
        out_shape=jax.ShapeDtypeStruct((t, n), out_dtype),
        grid=(t // tm, n // tn),
        in_specs=[pl.BlockSpec((tm, d), lambda m, j: (m, 0)), sc_spec, sh_spec,
                  pl.BlockSpec((d, tn), lambda m, j: (0, j))],
        out_specs=pl.BlockSpec((tm, tn), lambda m, j: (m, j)),
        scratch_shapes=[pltpu.VMEM((tm, d), bf16)],
        compiler_params=_cparams(("arbitrary", "arbitrary")),
        name="modmm",
    )(x, mod, mod, w)


def _regroup_body(table_ref, a_ref, b_ref, o_ref, *, shift):
    del table_ref
    tn = a_ref.shape[1]
    ab = jnp.concatenate([a_ref[...], b_ref[...]], axis=1)
    o_ref[...] = pltpu.roll(ab, 2 * tn - shift, axis=1)[:, :tn].astype(o_ref.dtype)


def _regroup_cols(w, src_cols, widths, tn):
    d = w.shape[0]
    shift = src_cols[0] % tn
    assert all(c % tn == shift for c in src_cols) and all(x % tn == 0 for x in widths)
    first_blocks = []
    for c, x in zip(src_cols, widths):
        first_blocks += [c // tn + i for i in range(x // tn)]
    table = jnp.asarray(first_blocks, i32)
    return pl.pallas_call(
        functools.partial(_regroup_body, shift=shift),
        out_shape=jax.ShapeDtypeStruct((d, sum(widths)), bf16),
        grid_spec=pltpu.PrefetchScalarGridSpec(
            num_scalar_prefetch=1,
            grid=(len(first_blocks),),
            in_specs=[pl.BlockSpec((d, tn), lambda j, tb: (0, tb[j])),
                      pl.BlockSpec((d, tn), lambda j, tb: (0, tb[j] + 1))],
            out_specs=pl.BlockSpec((d, tn), lambda j, tb: (0, j))),
        compiler_params=_cparams(("arbitrary",)),
        name="regroup",
    )(table, w, w)


def _cumsum_rows(x, n):
    row = lax.broadcasted_iota(i32, x.shape, 0)
    s = 1
    while s < n:
        x = x + jnp.where(row >= s, pltpu.roll(x, s, axis=0), 0.0)
        s *= 2
    return x


def _conv_silu(x, prev, w):
    c = x.shape[0]
    taps = w.shape[0]
    xc = jnp.concatenate([prev, x], axis=0)
    y = w[taps - 1:taps, :] * x
    for i in range(taps - 1):
        back = taps - 1 - i
        y = y + w[i:i + 1, :] * pltpu.roll(xc, back, axis=0)[SUBLANES:SUBLANES + c, :]
    return _silu(y)


def _dn_body(q_ref, k_ref, v_ref, z_ref, ba_ref, par_ref, cwq_ref, cwk_ref, cwv_ref, nw_ref,
             hq_ref, hk_ref, hv_ref, s0_ref,
             o_ref, so_ref, nbq_ref, nbk_ref, nbv_ref,
             s_scr, pq, pk, pv, *, chunk, heads, n_chunks):
    t = pl.program_id(2)
    dk = LANES

    @pl.when(t == 0)
    def _init():
        s_scr[...] = s0_ref[...]
        pq[...] = hq_ref[...]
        pk[...] = hk_ref[...]
        pv[...] = hv_ref[...]

    neg_a = -jnp.exp(par_ref[0:1, :])
    dt_b = par_ref[1:2, :]
    nw = nw_ref[...]
    ii = lax.broadcasted_iota(i32, (chunk, chunk), 0)
    jj = lax.broadcasted_iota(i32, (chunk, chunk), 1)
    causal = ii >= jj
    strict = ii > jj
    eye = (ii == jj).astype(f32)
    zpad = jnp.zeros((LANES - chunk, LANES), f32)

    def do_chunk(ci):
        r0 = pl.multiple_of(ci * chunk, chunk)
        rows = pl.ds(r0, chunk)
        ba = ba_ref[rows, :]
        beta_all = jax.nn.sigmoid(ba)
        xs = ba + dt_b
        softplus = jnp.maximum(xs, 0.0) + jnp.log1p(jnp.exp(-jnp.abs(xs)))
        gc_all = _cumsum_rows(neg_a * softplus, chunk)
        gc_t = jnp.concatenate([gc_all, zpad], axis=0).T
        eg_all = jnp.exp(gc_all)

        xq = q_ref[rows, :].astype(f32)
        xk = k_ref[rows, :].astype(f32)
        xv = v_ref[rows, :].astype(f32)
        qc = _conv_silu(xq, pq[...], cwq_ref[...])
        kc = _conv_silu(xk, pk[...], cwk_ref[...])
        vc = _conv_silu(xv, pv[...], cwv_ref[...])
        pq[...] = xq[chunk - SUBLANES:, :]
        pk[...] = xk[chunk - SUBLANES:, :]
        pv[...] = xv[chunk - SUBLANES:, :]

        hs = range(heads)
        sls = [slice(h * dk, (h + 1) * dk) for h in hs]
        qn = [qc[:, sl] for sl in sls]
        kn = [kc[:, sl] for sl in sls]
        qn = [q * (lax.rsqrt(jnp.sum(q * q, axis=-1, keepdims=True) + RMS_EPS) * (dk ** -0.5)) for q in qn]
        kn = [k * lax.rsqrt(jnp.sum(k * k, axis=-1, keepdims=True) + RMS_EPS) for k in kn]
        beta = [beta_all[:, h:h + 1] for h in hs]
        gc = [gc_all[:, heads + h:heads + h + 1] for h in hs]
        eg = [eg_all[:, heads + h:heads + h + 1] for h in hs]
        qk_kk = [lax.dot_general(jnp.concatenate([qn[h], kn[h]], axis=0).astype(bf16), kn[h].astype(bf16),
                                 (((1,), (1,)), ((), ())), preferred_element_type=f32) for h in hs]
        decay = []
        for h in hs:
            diff = gc[h] - gc_t[heads + h:heads + h + 1, 0:chunk]
            decay.append(jnp.where(causal, jnp.exp(jnp.where(causal, diff, 0.0)), 0.0))
        qk = [qk_kk[h][:chunk, :] * decay[h] for h in hs]
        p = [-jnp.where(strict, qk_kk[h][chunk:, :] * beta[h] * decay[h], 0.0) for h in hs]
        tinv = [eye + p[h] for h in hs]
        n = 1
        while 2 * n < chunk:
            p = [_bdot(p[h], p[h]) for h in hs]
            tinv = [tinv[h] + _bdot(p[h], tinv[h]) for h in hs]
            n *= 2
        sol = [_bdot(tinv[h], jnp.concatenate([vc[:, sls[h]] * beta[h], kn[h] * (beta[h] * eg[h])], axis=1))
               for h in hs]
        s_old = [s_scr[h] for h in hs]
        ks_qs = [_bdot(jnp.concatenate([sol[h][:, dk:], qn[h] * eg[h]], axis=0), s_old[h]) for h in hs]
        u = [sol[h][:, :dk] - ks_qs[h][:chunk, :] for h in hs]
        o = [ks_qs[h][chunk:, :] + _bdot(qk[h], u[h]) for h in hs]
        for h in hs:
            gc_last = gc[h][chunk - 1:chunk, :]
            kd = kn[h] * jnp.exp(gc_last - gc[h])
            s_scr[h] = s_old[h] * jnp.exp(gc_last) + lax.dot_general(
                kd.astype(bf16), u[h].astype(bf16), (((0,), (0,)), ((), ())), preferred_element_type=f32)
        for h in hs:
            zh = z_ref[rows, sls[h]].astype(f32)
            on = o[h] * lax.rsqrt(jnp.mean(o[h] * o[h], axis=-1, keepdims=True) + RMS_EPS) * nw * _silu(zh)
            o_ref[rows, sls[h]] = on.astype(o_ref.dtype)

    if n_chunks == 1:
        do_chunk(0)
    else:
        def loop_body(ci, carry):
            do_chunk(ci)
            return carry
        lax.fori_loop(0, n_chunks, loop_body, 0)

    @pl.when(t == pl.num_programs(2) - 1)
    def _fin():
        so_ref[...] = s_scr[...]
        nbq_ref[...] = pq[...]
        nbk_ref[...] = pk[...]
        nbv_ref[...] = pv[...]


def _deltanet(proj3, ba3, par, w_conv, norm_w, hist8, s0, *, n_heads, heads_per_step, chunk, lt):
    bn, seq, _ = proj3.shape
    dk = LANES
    dnw = n_heads * dk
    hg = heads_per_step
    gw = hg * dk
    n_g = n_heads // hg
    n_t = seq // lt
    kern = functools.partial(_dn_body, chunk=chunk, heads=hg, n_chunks=lt // chunk)

    def col(off):
        return pl.BlockSpec((None, lt, gw), lambda b, g, t, off=off: (b, t, off * n_g + g))

    def cw(off):
        return pl.BlockSpec((w_conv.shape[0], gw), lambda b, g, t, off=off: (0, off * n_g + g))

    def hist(off):
        return pl.BlockSpec((None, SUBLANES, gw), lambda b, g, t, off=off: (b, 0, off * n_g + g))

    nb_spec = pl.BlockSpec((None, SUBLANES, gw), lambda b, g, t: (b, 0, g))
    nb_shape = jax.ShapeDtypeStruct((bn, SUBLANES, dnw), f32)
    return pl.pallas_call(
        kern,
        out_shape=(jax.ShapeDtypeStruct((bn, seq, dnw), bf16),
                   jax.ShapeDtypeStruct(s0.shape, f32), nb_shape, nb_shape, nb_shape),
        grid=(bn, n_g, n_t),
        in_specs=[col(0), col(1), col(2), col(3),
                  pl.BlockSpec((None, lt, LANES), lambda b, g, t: (b, t, g)),
                  pl.BlockSpec((None, SUBLANES, LANES), lambda b, g, t: (g, 0, 0)),
                  cw(0), cw(1), cw(2),
                  pl.BlockSpec((1, dk), lambda b, g, t: (0, 0)),
                  hist(0), hist(1), hist(2),
                  pl.BlockSpec((None, hg, dk, dk), lambda b, g, t: (b, g, 0, 0))],
        out_specs=(pl.BlockSpec((None, lt, gw), lambda b, g, t: (b, t, g)),
                   pl.BlockSpec((None, hg, dk, dk), lambda b, g, t: (b, g, 0, 0)),
                   nb_spec, nb_spec, nb_spec),
        scratch_shapes=[pltpu.VMEM((hg, dk, dk), f32),
                        pltpu.VMEM((SUBLANES, gw), f32), pltpu.VMEM((SUBLANES, gw), f32),
                        pltpu.VMEM((SUBLANES, gw), f32)],
        compiler_params=_cparams(("arbitrary", "arbitrary", "arbitrary")),
        name="deltanet",
    )(proj3, proj3, proj3, proj3, ba3, par, w_conv, w_conv, w_conv, norm_w.reshape(1, dk),
      hist8, hist8, hist8, s0)


def _sc_body(b_ref, c_ref, x_ref, w_ref, ha_ref, hb_ref, o_ref, tail_ref, *, seq_len):
    u = c_ref[...].astype(f32) * x_ref[...].astype(f32)
    rows = u.shape[0]
    tin = lax.broadcasted_iota(i32, u.shape, 0) % seq_len
    w = w_ref[...]
    conv = (w[2:3, :] * u
            + w[1:2, :] * jnp.where(tin >= 1, pltpu.roll(u, 1, axis=0), 0.0)
            + w[0:1, :] * jnp.where(tin >= 2, pltpu.roll(u, 2, axis=0), 0.0))
    corr = w[1:2, :] * ha_ref[...] + w[0:1, :] * hb_ref[...]
    bv = b_ref[...].astype(f32)
    if corr.shape[0] == rows:
        o_ref[...] = (bv * (conv + corr)).astype(o_ref.dtype)
        tail_ref[...] = u
    else:
        o_ref[...] = (bv * conv).astype(o_ref.dtype)
        o_ref[0:SUBLANES, :] = (bv[0:SUBLANES, :] * (conv[0:SUBLANES, :] + corr)).astype(o_ref.dtype)
        tail_ref[...] = u[rows - SUBLANES:, :]


def _short_conv(proj, col0, scw, w_sc, hist_a, hist_b, seq_len, rows_per_step, tw):
    t = proj.shape[0]
    n_seq = t // seq_len
    hr = SUBLANES if rows_per_step == seq_len else rows_per_step
    n_w = scw // tw
    cb = col0 // tw

    def pin(off):
        return pl.BlockSpec((rows_per_step, tw), lambda r, j, off=off: (r, cb + off * n_w + j))

    h_spec = pl.BlockSpec((hr, tw), lambda r, j: (r, j))
    return pl.pallas_call(
        functools.partial(_sc_body, seq_len=seq_len),
        out_shape=(jax.ShapeDtypeStruct((t, scw), bf16),
                   jax.ShapeDtypeStruct((n_seq * SUBLANES, scw), f32)),
        grid=(t // rows_per_step, n_w),
        in_specs=[pin(0), pin(1), pin(2), pl.BlockSpec((w_sc.shape[0], tw), lambda r, j: (0, j)),
                  h_spec, h_spec],
        out_specs=(pl.BlockSpec((rows_per_step, tw), lambda r, j: (r, j)),
                   pl.BlockSpec((hr, tw), lambda r, j: (r, j))),
        compiler_params=_cparams(("arbitrary", "arbitrary")),
        name="short_conv",
    )(proj, proj, proj, w_sc, hist_a, hist_b)


def _merge_body(o_ref, s_ref, gd_ref, gs_ref, wdn_ref, wsc_ref, m_ref):
    ydn = jnp.dot(o_ref[...], wdn_ref[...], preferred_element_type=f32)
    ysc = jnp.dot(s_ref[...], wsc_ref[...], preferred_element_type=f32)
    m = jax.nn.sigmoid(gd_ref[...].astype(f32)) * ydn + jax.nn.sigmoid(gs_ref[...].astype(f32)) * ysc
    m_ref[...] = m.astype(m_ref.dtype)


def _merge(o_n, scin, proj, gd_col, gs_col, w_dn, w_sc, tm):
    t, dnw = o_n.shape
    scw = scin.shape[1]
    d = w_dn.shape[1]
    return pl.pallas_call(
        _merge_body,
        out_shape=jax.ShapeDtypeStruct((t, d), bf16),
        grid=(t // tm,),
        in_specs=[pl.BlockSpec((tm, dnw), lambda m: (m, 0)),
                  pl.BlockSpec((tm, scw), lambda m: (m, 0)),
                  pl.BlockSpec((tm, d), lambda m: (m, gd_col // d)),
                  pl.BlockSpec((tm, d), lambda m: (m, gs_col // d)),
                  pl.BlockSpec((dnw, d), lambda m: (0, 0)),
                  pl.BlockSpec((scw, d), lambda m: (0, 0))],
        out_specs=pl.BlockSpec((tm, d), lambda m: (m, 0)),
        compiler_params=_cparams(("arbitrary",)),
        name="merge",
    )(o_n, scin, proj, proj, w_dn, w_sc)


def _split_dot3(a, b):
    a_hi = a.astype(bf16)
    a_lo = (a - a_hi.astype(f32)).astype(bf16)
    b_hi = b.astype(bf16)
    b_lo = (b - b_hi.astype(f32)).astype(bf16)
    return (jnp.dot(a_hi, b_hi, preferred_element_type=f32) + jnp.dot(a_hi, b_lo, preferred_element_type=f32)
            + jnp.dot(a_lo, b_hi, preferred_element_type=f32))


def _mix_body(m_ref, x_ref, g1_ref, sh2_ref, sc2_ref, wo_ref, lg_ref, lb_ref, wr_ref, rb_ref, cnt_in_ref,
              x1_ref, hp_ref, idx_ref, wt_ref, pos_ref, cnt_ref, cnt_scr, *, top_k):
    step = pl.program_id(0)

    @pl.when(step == 0)
    def _():
        cnt_scr[...] = cnt_in_ref[...]

    mix = jnp.dot(m_ref[...], wo_ref[...], preferred_element_type=f32)
    x1 = _layer_norm(DEEPNORM_ALPHA * x_ref[...] + (1.0 + g1_ref[...]) * mix, lg_ref[...], lb_ref[...])
    x1_ref[...] = x1
    h2 = x1 * (1.0 + sc2_ref[...]) + sh2_ref[...]
    half = h2.shape[1] // 2
    hp_ref[...] = _pack_pair(h2[:, :half], h2[:, half:])

    tm = h2.shape[0]
    n_e = wr_ref.shape[1]
    scores = jax.nn.sigmoid(_split_dot3(h2, wr_ref[...]))
    work = scores + rb_ref[...]
    lane_e = lax.broadcasted_iota(i32, (tm, n_e), 1).astype(f32)
    lane_o = lax.broadcasted_iota(i32, (tm, LANES), 1)
    sel = jnp.zeros((tm, n_e), f32)
    picks = []
    wsum = jnp.zeros((tm, 1), f32)
    for _ in range(top_k):
        mx = jnp.max(work, axis=-1, keepdims=True)
        idx = jnp.min(jnp.where(work == mx, lane_e, float(n_e)), axis=-1, keepdims=True)
        hit = lane_e == idx
        wk = jnp.sum(jnp.where(hit, scores, 0.0), axis=-1, keepdims=True)
        work = jnp.where(hit, -jnp.inf, work)
        sel = jnp.where(hit, 1.0, sel)
        wsum = wsum + wk
        picks.append((idx, hit, wk))

    ri = lax.broadcasted_iota(i32, (tm, tm), 0)
    ci = lax.broadcasted_iota(i32, (tm, tm), 1)
    tri = jnp.where(ri > ci, 1.0, 0.0).astype(bf16)
    before = jnp.dot(tri, sel.astype(bf16), preferred_element_type=f32) + cnt_scr[...]
    cnt_scr[...] = cnt_scr[...] + jnp.sum(sel, axis=0, keepdims=True)
    cnt_ref[...] = cnt_scr[...]

    idx_o = jnp.zeros((tm, LANES), i32)
    wt_o = jnp.zeros((tm, LANES), f32)
    pos_o = jnp.zeros((tm, LANES), i32)
    for k, (idx, hit, wk) in enumerate(picks):
        pk = jnp.sum(jnp.where(hit, before, 0.0), axis=-1, keepdims=True).astype(i32)
        idx_o = jnp.where(lane_o == k, idx.astype(i32), idx_o)
        wt_o = jnp.where(lane_o == k, wk / wsum * ROUTED_SCALE, wt_o)
        pos_o = jnp.where(lane_o == k, pk, pos_o)
    idx_ref[...] = idx_o
    wt_ref[...] = wt_o
    pos_ref[...] = pos_o


def _mix(merged, x, mod, seq_len, w_o, ln_g, ln_b, w_router, router_bias, cnt_in, tm):
    t, d = x.shape
    n_e = w_router.shape[1]
    g1_spec, sh2_spec, sc2_spec = _mod_specs(mod, tm, seq_len, (2, 3, 4), d)
    row = lambda m: (m, 0)
    const = lambda m: (0, 0)
    lane_out = jax.ShapeDtypeStruct((t, LANES), i32)
    return pl.pallas_call(
        functools.partial(_mix_body, top_k=TOP_K),
        out_shape=(jax.ShapeDtypeStruct((t, d), f32), jax.ShapeDtypeStruct((t, d // 2), u32),
                   lane_out, jax.ShapeDtypeStruct((t, LANES), f32), lane_out,
                   jax.ShapeDtypeStruct((1, n_e), f32)),
        grid=(t // tm,),
        in_specs=[pl.BlockSpec((tm, d), row), pl.BlockSpec((tm, d), row), g1_spec, sh2_spec, sc2_spec,
                  pl.BlockSpec((d, d), const), pl.BlockSpec((1, d), const), pl.BlockSpec((1, d), const),
                  pl.BlockSpec((d, n_e), const), pl.BlockSpec((1, n_e), const), pl.BlockSpec((1, n_e), const)],
        out_specs=(pl.BlockSpec((tm, d), row), pl.BlockSpec((tm, d // 2), row),
                   pl.BlockSpec((tm, LANES), row), pl.BlockSpec((tm, LANES), row), pl.BlockSpec((tm, LANES), row),
                   pl.BlockSpec((1, n_e), const)),
        scratch_shapes=[pltpu.VMEM((1, n_e), f32)],
        compiler_params=_cparams(("arbitrary",)),
        name="mix",
    )(merged, x, mod, mod, mod, w_o, ln_g.reshape(1, d), ln_b.reshape(1, d), w_router,
      router_bias.reshape(1, n_e), cnt_in)


def _row_copy(src, src_row, dst, dst_row, sem):
    return pltpu.make_async_copy(src.at[pl.ds(src_row, 1)], dst.at[pl.ds(dst_row, 1)], sem)


def _dispatch_body(flag_ref, dest_hbm, ha_ref, hb_ref, xs_out, dsm, zeros, sem_idx, sem_zero, sem,
                   *, top_k, steps_a, tile_rows, n_tiles):
    step = pl.program_id(0)
    tm = ha_ref.shape[0]

    @pl.when(step == 0)
    def _():
        zeros[...] = jnp.zeros(zeros.shape, zeros.dtype)

        def tile_copy(i):
            start = pl.multiple_of(i * tile_rows, tile_rows)
            return pltpu.make_async_copy(zeros, xs_out.at[pl.ds(start, tile_rows)], sem_zero)

        def issue_zero(i, c):
            @pl.when(flag_ref[i] == 1)
            def _():
                tile_copy(i).start()
            return c

        def drain_zero(i, c):
            @pl.when(flag_ref[i] == 1)
            def _():
                tile_copy(i).wait()
            return c

        lax.fori_loop(0, n_tiles, issue_zero, 0)
        lax.fori_loop(0, n_tiles, drain_zero, 0)

    fetch = pltpu.make_async_copy(dest_hbm.at[step], dsm, sem_idx)
    fetch.start()
    fetch.wait()

    def scatter(h_ref):
        def issue(r8, c):
            base = pl.multiple_of(r8 * SUBLANES, SUBLANES)
            for j in range(SUBLANES):
                for k in range(top_k):
                    _row_copy(h_ref, base + j, xs_out, dsm[(base + j) * top_k + k], sem).start(priority=k % 2)
            return c

        lax.fori_loop(0, tm // SUBLANES, issue, 0)

        def drain(r, c):
            for k in range(top_k):
                _row_copy(h_ref, 0, xs_out, 0, sem).wait()
            return c

        lax.fori_loop(0, tm, drain, 0)

    @pl.when(step < steps_a)
    def _():
        scatter(ha_ref)

    @pl.when(step >= steps_a)
    def _():
        scatter(hb_ref)


def _dispatch(flags, dest2d, hp_a, hp_b, rows, tile_rows, tm):
    half = hp_a.shape[1]
    steps_a = hp_a.shape[0] // tm
    steps_b = hp_b.shape[0] // tm
    n_tiles = rows // tile_rows
    kern = functools.partial(_dispatch_body, top_k=TOP_K, steps_a=steps_a, tile_rows=tile_rows, n_tiles=n_tiles)
    return pl.pallas_call(
        kern,
        out_shape=jax.ShapeDtypeStruct((rows, half), u32),
        grid_spec=pltpu.PrefetchScalarGridSpec(
            num_scalar_prefetch=1,
            grid=(steps_a + steps_b,),
            in_specs=[pl.BlockSpec(memory_space=pl.ANY),
                      pl.BlockSpec((tm, half), lambda m, fl: (jnp.minimum(m, steps_a - 1), 0)),
                      pl.BlockSpec((tm, half), lambda m, fl: (jnp.maximum(m - steps_a, 0), 0))],
            out_specs=pl.BlockSpec(memory_space=pl.ANY),
            scratch_shapes=[pltpu.SMEM((tm * TOP_K,), i32), pltpu.VMEM((tile_rows, half), u32),
                            pltpu.SemaphoreType.DMA(()), pltpu.SemaphoreType.DMA(()), pltpu.SemaphoreType.DMA(())]),
        compiler_params=_cparams(("arbitrary",)),
        name="dispatch",
    )(flags, dest2d, hp_a, hp_b)


def _expert_body(te_ref, nt_ref, first_ref, slot_ref, nxt_ref, x_ref, wg_hbm, wu_hbm, wd_hbm, o_ref,
                 wg_f, wu_f, wd_f, wg_s, wu_s, wd_s, sems):
    step = pl.program_id(0)

    def weight_copies(e, s):
        return (pltpu.make_async_copy(wg_hbm.at[e], wg_f.at[s], sems.at[s, 0]),
                pltpu.make_async_copy(wu_hbm.at[e], wu_f.at[s], sems.at[s, 1]),
                pltpu.make_async_copy(wd_hbm.at[e], wd_f.at[s], sems.at[s, 2]))

    @pl.when(step < nt_ref[0])
    def _():
        e = te_ref[step]
        s = slot_ref[step]

        @pl.when(first_ref[step] == 1)
        def _():
            @pl.when(step == 0)
            def _():
                for c in weight_copies(e, s):
                    c.start()

            for c in weight_copies(e, s):
                c.wait()
            nxt = nxt_ref[step]

            @pl.when(nxt >= 0)
            def _():
                for c in weight_copies(nxt, 1 - s):
                    c.start()

            wg_s[...] = wg_f[s].astype(bf16)
            wu_s[...] = wu_f[s].astype(bf16)
            wd_s[...] = wd_f[s].astype(bf16)

        lo, hi = _unpack_pair(x_ref[...])
        lo = lo.astype(bf16)
        hi = hi.astype(bf16)
        half = lo.shape[1]
        g = (jnp.dot(lo, wg_s[0:half, :], preferred_element_type=f32)
             + jnp.dot(hi, wg_s[half:, :], preferred_element_type=f32))
        u = (jnp.dot(lo, wu_s[0:half, :], preferred_element_type=f32)
             + jnp.dot(hi, wu_s[half:, :], preferred_element_type=f32))
        hid = (_silu(g) * u).astype(bf16)
        out = jnp.dot(hid, wd_s[...], preferred_element_type=f32)
        o_ref[...] = _pack_pair(out[:, :half], out[:, half:])


def _experts(tile_expert, n_tiles, first, slot, nxt, xs, w_gate, w_up, w_down, tm):
    rows, half = xs.shape
    n_e, d, f = w_gate.shape
    max_tiles = rows // tm

    def tile(i, te, nt, *_):
        return (jnp.minimum(i, nt[0] - 1), 0)

    hbm = pl.BlockSpec(memory_space=pl.ANY)
    return pl.pallas_call(
        _expert_body,
        out_shape=jax.ShapeDtypeStruct((rows, half), u32),
        grid_spec=pltpu.PrefetchScalarGridSpec(
            num_scalar_prefetch=5,
            grid=(max_tiles,),
            in_specs=[pl.BlockSpec((tm, half), tile), hbm, hbm, hbm],
            out_specs=pl.BlockSpec((tm, half), tile),
            scratch_shapes=[pltpu.VMEM((2, d, f), f32), pltpu.VMEM((2, d, f), f32), pltpu.VMEM((2, f, d), f32),
                            pltpu.VMEM((d, f), bf16), pltpu.VMEM((d, f), bf16), pltpu.VMEM((f, d), bf16),
                            pltpu.SemaphoreType.DMA((2, 3))]),
        input_output_aliases={5: 0},
        compiler_params=_cparams(("arbitrary",)),
        name="experts",
    )(tile_expert, n_tiles, first, slot, nxt, xs, w_gate, w_up, w_down)


def _combine_body(dest_hbm, os_hbm, x1_ref, hp_ref, wt_ref, g2_ref, lg_ref, lb_ref, wsg_ref, wsu_ref, wsd_ref,
                  y_ref, dsm0, dsm1, buf, sem_idx, sems, *, top_k):
    step = pl.program_id(0)
    tm = x1_ref.shape[0]
    slot = step % 2

    def gather(tile, s):
        dsm = (dsm0, dsm1)[s]
        fetch = pltpu.make_async_copy(dest_hbm.at[tile], dsm, sem_idx)
        fetch.start()
        fetch.wait()

        def issue(r8, c):
            base = pl.multiple_of(r8 * SUBLANES, SUBLANES)
            for j in range(SUBLANES):
                for k in range(top_k):
                    src_row = dsm[(base + j) * top_k + k]
                    _row_copy(os_hbm, src_row, buf.at[s, k], base + j, sems.at[s]).start(priority=k % 2)
            return c

        lax.fori_loop(0, tm // SUBLANES, issue, 0)

    @pl.when(step == 0)
    def _():
        gather(0, 0)

    has_next = step + 1 < pl.num_programs(0)
    for s in (0, 1):
        @pl.when(has_next & (slot == 1 - s))
        def _(s=s):
            gather(step + 1, s)

    lo, hi = _unpack_pair(hp_ref[...])
    lo = lo.astype(bf16)
    hi = hi.astype(bf16)
    half = lo.shape[1]
    g = (jnp.dot(lo, wsg_ref[0:half, :], preferred_element_type=f32)
         + jnp.dot(hi, wsg_ref[half:, :], preferred_element_type=f32))
    u = (jnp.dot(lo, wsu_ref[0:half, :], preferred_element_type=f32)
         + jnp.dot(hi, wsu_ref[half:, :], preferred_element_type=f32))
    shared = jnp.dot((_silu(g) * u).astype(bf16), wsd_ref[...], preferred_element_type=f32)

    for s in (0, 1):
        @pl.when(slot == s)
        def _(s=s):
            def drain(r, c):
                for k in range(top_k):
                    _row_copy(os_hbm, 0, buf.at[s, k], 0, sems.at[s]).wait()
                return c

            lax.fori_loop(0, tm, drain, 0)

    y_lo = shared[:, :half]
    y_hi = shared[:, half:]
    wt = wt_ref[...]
    for k in range(top_k):
        e_lo, e_hi = _unpack_pair(buf[slot, k])
        wk = wt[:, k:k + 1]
        y_lo = y_lo + wk * e_lo
        y_hi = y_hi + wk * e_hi
    ffn = jnp.concatenate([y_lo, y_hi], axis=1)
    y_ref[...] = _layer_norm(DEEPNORM_ALPHA * x1_ref[...] + (1.0 + g2_ref[...]) * ffn, lg_ref[...], lb_ref[...])


def _combine(dest2d, out_sorted, x1, hp, wts, mod, seq_len, ln_g, ln_b, ws_gate, ws_up, ws_down, tm):
    t, d = x1.shape
    half = d // 2
    sf = ws_gate.shape[1]
    (g2_spec,) = _mod_specs(mod, tm, seq_len, (5,), d)
    row = lambda m: (m, 0)
    const = lambda m: (0, 0)
    return pl.pallas_call(
        functools.partial(_combine_body, top_k=TOP_K),
        out_shape=jax.ShapeDtypeStruct((t, d), f32),
        grid=(t // tm,),
        in_specs=[pl.BlockSpec(memory_space=pl.ANY), pl.BlockSpec(memory_space=pl.ANY),
                  pl.BlockSpec((tm, d), row), pl.BlockSpec((tm, half), row), pl.BlockSpec((tm, LANES), row),
                  g2_spec, pl.BlockSpec((1, d), const), pl.BlockSpec((1, d), const),
                  pl.BlockSpec((d, sf), const), pl.BlockSpec((d, sf), const), pl.BlockSpec((sf, d), const)],
        out_specs=pl.BlockSpec((tm, d), row),
        scratch_shapes=[pltpu.SMEM((tm * TOP_K,), i32), pltpu.SMEM((tm * TOP_K,), i32),
                        pltpu.VMEM((2, TOP_K, tm, half), u32),
                        pltpu.SemaphoreType.DMA(()), pltpu.SemaphoreType.DMA((2,))],
        compiler_params=_cparams(("arbitrary",)),
        name="combine",
    )(dest2d, out_sorted, x1, hp, wts, mod, ln_g.reshape(1, d), ln_b.reshape(1, d), ws_gate, ws_up, ws_down)


def _pick(total, pref, unit=SUBLANES):
    if total <= pref:
        return total
    c = pref - pref % unit
    while total % c:
        c -= unit
    return c


def kernel(x_prompt, x_sample, state_dn_S, state_dn_conv, state_sc_conv, c_prompt, c_sample, w_ada, b_ada, w_in, w_dn_conv, dn_a_log, dn_dt_bias, dn_norm_w, w_sc_conv, w_dn_out, w_sc_out, w_o, ln1_g, ln1_b, w_router, router_bias, w_e_gate, w_e_up, w_e_down, w_s_gate, w_s_up, w_s_down, ln2_g, ln2_b):
    bp, lp, d = x_prompt.shape
    bs, ls, _ = x_sample.shape
    n_heads = dn_a_log.shape[0]
    dk = dn_norm_w.shape[0]
    assert dk == LANES
    dnw = n_heads * dk
    scw = w_sc_conv.shape[1]
    n_e = w_router.shape[1]
    tp, ts = bp * lp, bs * ls

    n_a = 4 * dnw
    o_sc = n_a + 2 * n_heads
    o_gd = o_sc + 3 * scw
    w_c = _regroup_cols(w_in, (o_gd, o_sc), (2 * d, 3 * scw), _pick(math.gcd(2 * d, 3 * scw), 512, LANES))
    c_gd = 0
    c_gs = d
    c_sc = 2 * d
    hg_p = min(16, n_heads)
    hg_s = n_heads

    def ba_weights(hg):
        n_g = n_heads // hg
        wb = w_in[:, n_a:n_a + n_heads].reshape(d, n_g, hg)
        wa = w_in[:, n_a + n_heads:n_a + 2 * n_heads].reshape(d, n_g, hg)
        pad = jnp.zeros((d, n_g, LANES - 2 * hg), w_in.dtype)
        w = jnp.concatenate([wb, wa, pad], axis=2).reshape(d, n_g * LANES).astype(bf16)
        zl = jnp.zeros((n_g, hg), f32)
        lane = lambda v: jnp.concatenate([zl, v.astype(f32).reshape(n_g, hg), jnp.zeros((n_g, LANES - 2 * hg), f32)], axis=1)
        par = jnp.stack([lane(dn_a_log), lane(dn_dt_bias)] + [jnp.zeros((n_g, LANES), f32)] * (SUBLANES - 2), axis=1)
        return w, par

    w_dn_b = w_dn_out.astype(bf16)
    w_sc_b = w_sc_out.astype(bf16)
    w_o_b = w_o.astype(bf16)
    ws_gate_b = w_s_gate.astype(bf16)
    ws_up_b = w_s_up.astype(bf16)
    ws_down_b = w_s_down.astype(bf16)

    n_c = bp + bs
    m_pad = -(-n_c // SUBLANES) * SUBLANES
    c_all = jnp.concatenate([c_prompt, c_sample, jnp.zeros((m_pad - n_c, d), f32)], axis=0)
    mod = _ada(c_all, w_ada, b_ada)
    mod_p = mod[:bp].reshape(bp, 1, 6 * d)
    mod_s = jnp.repeat(mod[bp:n_c], ls, axis=0)

    def group(x3, mod_g, s0, dn_hist, sc_hist, hg, chunk, lt, cnt_in):
        bn, seq, _ = x3.shape
        t = bn * seq
        x = x3.reshape(t, d)
        tile = (lambda pref: _pick(seq, pref)) if mod_g.ndim == 3 else (lambda pref: _pick(t, pref))
        proj_a = _modmm(x, mod_g, seq, w_in, n_a, bf16, tile(1024), 1024)
        proj = _modmm(x, mod_g, seq, w_c, w_c.shape[1], bf16, tile(1024), 1024)
        w_ba, par = ba_weights(hg)
        ba = _modmm(x, mod_g, seq, w_ba, w_ba.shape[1], f32, tile(1024), 512)
        hist8 = jnp.concatenate([jnp.zeros((bn, SUBLANES - dn_hist.shape[1], 3 * dnw), f32), dn_hist], axis=1)
        o_n, s_new, nbq, nbk, nbv = _deltanet(
            proj_a.reshape(bn, seq, -1), ba.reshape(bn, seq, -1), par, w_dn_conv, dn_norm_w, hist8, s0,
            n_heads=n_heads, heads_per_step=hg, chunk=chunk, lt=lt)
        keep = w_dn_conv.shape[0] - 1
        dn_buf = jnp.concatenate([nbq, nbk, nbv], axis=2)[:, SUBLANES - keep:, :]

        zrow = jnp.zeros((bn, SUBLANES - 2, scw), f32)
        hist_a = jnp.concatenate([sc_hist[:, 1:2], jnp.zeros((bn, 1, scw), f32), zrow], axis=1).reshape(bn * SUBLANES, scw)
        hist_b = jnp.concatenate([sc_hist[:, 0:1], sc_hist[:, 1:2], zrow], axis=1).reshape(bn * SUBLANES, scw)
        rows_sc = seq if seq > SUBLANES else _pick(t, 256)
        scin, tail = _short_conv(proj, c_sc, scw, w_sc_conv, hist_a, hist_b, seq, rows_sc, min(scw, 256))
        sc_keep = w_sc_conv.shape[0] - 1
        sc_buf = tail.reshape(bn, SUBLANES, scw)[:, SUBLANES - sc_keep:, :]

        merged = _merge(o_n.reshape(t, dnw), scin, proj, c_gd, c_gs, w_dn_b, w_sc_b, tile(256))
        x1, hp, idx, wts, pos, cnt = _mix(merged, x, mod_g, seq, w_o_b, ln1_g, ln1_b, w_router, router_bias,
                                          cnt_in, tile(256))
        return dict(x1=x1, hp=hp, idx=idx[:, :TOP_K], wts=wts, pos=pos[:, :TOP_K], cnt=cnt,
                    s=s_new, dn_buf=dn_buf, sc_buf=sc_buf, mod=mod_g, seq=seq, tm_d=tile(256))

    zeros_s = jnp.zeros((bp,) + state_dn_S.shape[1:], f32)
    zeros_dn = jnp.zeros((bp,) + state_dn_conv.shape[1:], f32)
    zeros_sc = jnp.zeros((bp,) + state_sc_conv.shape[1:], f32)
    chunk_p = math.gcd(DN_CHUNK, lp)
    chunk_s = math.gcd(DN_CHUNK, ls)
    gp = group(x_prompt, mod_p, zeros_s, zeros_dn, zeros_sc, hg_p, chunk_p, _pick(lp, 512),
               jnp.zeros((1, n_e), f32))
    gs = group(x_sample, mod_s, state_dn_S, state_dn_conv, state_sc_conv, hg_s, chunk_s, ls, gp["cnt"])

    tm_e = 256
    counts = gs["cnt"][0].astype(i32)
    tiles_per_e = (counts + tm_e - 1) // tm_e
    tile_end = jnp.cumsum(tiles_per_e)
    start_row = (tile_end - tiles_per_e) * tm_e
    n_rows = (tp + ts) * TOP_K
    max_tiles = n_rows // tm_e + n_e
    tile_ids = jnp.arange(max_tiles, dtype=i32)
    tile_expert = jnp.minimum(jnp.sum((tile_end[None, :] <= tile_ids[:, None]).astype(i32), axis=1), n_e - 1)
    n_tiles = tile_end[-1:].astype(i32)
    e_ids = jnp.arange(n_e, dtype=i32)
    nonempty = tiles_per_e > 0
    ordinal = jnp.cumsum(nonempty.astype(i32)) - 1
    later = (e_ids[None, :] > e_ids[:, None]) & nonempty[None, :]
    next_e = jnp.min(jnp.where(later, e_ids[None, :], n_e), axis=1)
    next_e = jnp.where(next_e == n_e, -1, next_e).astype(i32)
    onehot_te = tile_expert[:, None] == e_ids[None, :]
    lookup = lambda table: jnp.sum(jnp.where(onehot_te, table[None, :], 0), axis=1).astype(i32)
    first = (tile_ids == lookup(tile_end - tiles_per_e)).astype(i32)
    slot = lookup(ordinal) % 2
    nxt = lookup(next_e)

    for g in (gp, gs):
        t = g["x1"].shape[0]
        idx_flat = g["idx"].reshape(-1)
        start_flat = jnp.sum(jnp.where(idx_flat[:, None] == e_ids[None, :], start_row[None, :], 0), axis=1)
        g["dest"] = (start_flat + g["pos"].reshape(-1)).astype(i32).reshape(t // g["tm_d"], g["tm_d"] * TOP_K)
    assert gp["tm_d"] == gs["tm_d"]
    zero_flag = ((tile_ids >= n_tiles[0]) | (tile_ids == lookup(tile_end) - 1)).astype(i32)
    xs = _dispatch(zero_flag, jnp.concatenate([gp["dest"], gs["dest"]], axis=0), gp["hp"], gs["hp"],
                   max_tiles * tm_e, tm_e, gp["tm_d"])
    out_sorted = _experts(tile_expert, n_tiles, first, slot, nxt, xs, w_e_gate, w_e_up, w_e_down, tm_e)
    ys = [_combine(g["dest"], out_sorted, g["x1"], g["hp"], g["wts"], g["mod"], g["seq"], ln2_g, ln2_b,
                   ws_gate_b, ws_up_b, ws_down_b, g["tm_d"]) for g in (gp, gs)]

    y_p = ys[0].reshape(bp, lp, d)
    y_s = ys[1].reshape(bs, ls, d)
    return (y_p, y_s, gp["s"], gp["dn_buf"], gp["sc_buf"], gs["s"], gs["dn_buf"], gs["sc_buf"])
```

```python
import functools
import math

import jax
import jax.numpy as jnp
from jax import lax
from jax.experimental import pallas as pl
from jax.experimental.pallas import tpu as pltpu

f32 = jnp.float32
bf16 = jnp.bfloat16
i32 = jnp.int32
u32 = jnp.uint32

TOP_K = 8
ROUTED_SCALE = 2.5
DN_CHUNK = 64
DEPTH = 1
DEEPNORM_ALPHA = (2 * DEPTH) ** 0.25
LN_EPS = 1e-5
RMS_EPS = 1e-6

LANES = 128
SUBLANES = 8
VMEM_LIMIT = 56 * 1024 * 1024


def _cparams(sem, vmem=VMEM_LIMIT):
    return pltpu.CompilerParams(dimension_semantics=sem, vmem_limit_bytes=vmem)


def _silu(x):
    return x * jax.nn.sigmoid(x)


def _bdot(a, b):
    return jnp.dot(a.astype(bf16), b.astype(bf16), preferred_element_type=f32)


def _layer_norm(x, g, b):
    mu = jnp.mean(x, axis=-1, keepdims=True)
    xc = x - mu
    var = jnp.mean(xc * xc, axis=-1, keepdims=True)
    return xc * lax.rsqrt(var + LN_EPS) * g + b


def _pack_pair(a, b):
    ab = pltpu.bitcast(a.astype(bf16).astype(f32), u32)
    bb = pltpu.bitcast(b.astype(bf16).astype(f32), u32)
    return (ab >> 16) | (bb & jnp.uint32(0xFFFF0000))


def _unpack_pair(p):
    lo = pltpu.bitcast(p << 16, f32)
    hi = pltpu.bitcast(p & jnp.uint32(0xFFFF0000), f32)
    return lo, hi


def _ada_body(c_ref, w_ref, b_ref, o_ref):
    s = _silu(c_ref[...])
    o_ref[...] = _bdot(s, w_ref[...]) + b_ref[...]


def _ada(c_all, w_ada, b_ada):
    m, d = c_all.shape
    n = w_ada.shape[1]
    tn = _pick(n, 1024, LANES)
    return pl.pallas_call(
        _ada_body,
        out_shape=jax.ShapeDtypeStruct((m, n), f32),
        grid=(n // tn,),
        in_specs=[pl.BlockSpec((m, d), lambda j: (0, 0)),
                  pl.BlockSpec((d, tn), lambda j: (0, j)),
                  pl.BlockSpec((1, tn), lambda j: (0, j))],
        out_specs=pl.BlockSpec((m, tn), lambda j: (0, j)),
        compiler_params=_cparams(("arbitrary",)),
        name="ada",
    )(c_all, w_ada, b_ada.reshape(1, n))


def _modmm_body(off_ref, x_ref, sc_ref, sh_ref, w_ref, o_ref, h_scr):
    del off_ref

    @pl.when(pl.program_id(1) == 0)
    def _():
        h_scr[...] = (x_ref[...] * (1.0 + sc_ref[...]) + sh_ref[...]).astype(bf16)

    o_ref[...] = lax.dot_general(h_scr[...], w_ref[...].astype(bf16), (((1,), (1,)), ((), ())),
                                 preferred_element_type=f32).astype(o_ref.dtype)


def _mod_specs(mod, tm, seq_len, cols, d):
    if mod.ndim == 3:
        tiles_per_seq = seq_len // tm
        return [pl.BlockSpec((None, 1, d), lambda m, *_, c=c: (m // tiles_per_seq, 0, c)) for c in cols]
    return [pl.BlockSpec((tm, d), lambda m, *_, c=c: (m, c)) for c in cols]


def _modmm(x, mod, seq_len, w_t, row_starts, tn, out_dtype, tm):
    t, d = x.shape
    assert all(r % SUBLANES == 0 for r in row_starts)
    n_blocks = len(row_starts)
    sc_spec, sh_spec = _mod_specs(mod, tm, seq_len, (1, 0), d)
    return pl.pallas_call(
        _modmm_body,
        out_shape=jax.ShapeDtypeStruct((t, n_blocks * tn), out_dtype),
        grid_spec=pltpu.PrefetchScalarGridSpec(
            num_scalar_prefetch=1,
            grid=(t // tm, n_blocks),
            in_specs=[pl.BlockSpec((tm, d), lambda m, j, off: (m, 0)), sc_spec, sh_spec,
                      pl.BlockSpec((pl.Element(tn), pl.Element(d)), lambda m, j, off: (off[j] * SUBLANES, 0))],
            out_specs=pl.BlockSpec((tm, tn), lambda m, j, off: (m, j)),
            scratch_shapes=[pltpu.VMEM((tm, d), bf16)]),
        compiler_params=_cparams(("arbitrary", "arbitrary")),
        name="modmm",
    )(jnp.asarray([r // SUBLANES for r in row_starts], i32), x, mod, mod, w_t)


def _cumsum_rows(x, n):
    row = lax.broadcasted_iota(i32, x.shape, 0)
    s = 1
    while s < n:
        x = x + jnp.where(row >= s, pltpu.roll(x, s, axis=0), 0.0)
        s *= 2
    return x


def _conv_silu(x, prev, w):
    c = x.shape[0]
    taps = w.shape[0]
    xc = jnp.concatenate([prev, x], axis=0)
    y = w[taps - 1:taps, :] * x
    for i in range(taps - 1):
        back = taps - 1 - i
        y = y + w[i:i + 1, :] * pltpu.roll(xc, back, axis=0)[SUBLANES:SUBLANES + c, :]
    return _silu(y)


def _dn_body(q_ref, k_ref, v_ref, z_ref, ba_ref, par_ref, cwq_ref, cwk_ref, cwv_ref, nw_ref,
             hq_ref, hk_ref, hv_ref, s0_ref,
             o_ref, so_ref, nbq_ref, nbk_ref, nbv_ref,
             s_scr, pq, pk, pv, *, chunk, heads, n_chunks):
    t = pl.program_id(2)
    dk = LANES

    @pl.when(t == 0)
    def _init():
        s_scr[...] = s0_ref[...]
        pq[...] = hq_ref[...]
        pk[...] = hk_ref[...]
        pv[...] = hv_ref[...]

    neg_a = -jnp.exp(par_ref[0:1, :])
    dt_b = par_ref[1:2, :]
    nw = nw_ref[...]
    ii = lax.broadcasted_iota(i32, (chunk, chunk), 0)
    jj = lax.broadcasted_iota(i32, (chunk, chunk), 1)
    causal = ii >= jj
    strict = ii > jj
    eye = (ii == jj).astype(f32)
    zpad = jnp.zeros((LANES - chunk, LANES), f32)

    def do_chunk(ci):
        r0 = pl.multiple_of(ci * chunk, chunk)
        rows = pl.ds(r0, chunk)
        ba = ba_ref[rows, :]
        beta_all = jax.nn.sigmoid(ba)
        xs = ba + dt_b
        softplus = jnp.maximum(xs, 0.0) + jnp.log1p(jnp.exp(-jnp.abs(xs)))
        gc_all = _cumsum_rows(neg_a * softplus, chunk)
        gc_t = jnp.concatenate([gc_all, zpad], axis=0).T
        eg_all = jnp.exp(gc_all)

        xq = q_ref[rows, :].astype(f32)
        xk = k_ref[rows, :].astype(f32)
        xv = v_ref[rows, :].astype(f32)
        qc = _conv_silu(xq, pq[...], cwq_ref[...])
        kc = _conv_silu(xk, pk[...], cwk_ref[...])
        vc = _conv_silu(xv, pv[...], cwv_ref[...])
        pq[...] = xq[chunk - SUBLANES:, :]
        pk[...] = xk[chunk - SUBLANES:, :]
        pv[...] = xv[chunk - SUBLANES:, :]

        hs = range(heads)
        sls = [slice(h * dk, (h + 1) * dk) for h in hs]
        qn = [qc[:, sl] for sl in sls]
        kn = [kc[:, sl] for sl in sls]
        qn = [q * (lax.rsqrt(jnp.sum(q * q, axis=-1, keepdims=True) + RMS_EPS) * (dk ** -0.5)) for q in qn]
        kn = [k * lax.rsqrt(jnp.sum(k * k, axis=-1, keepdims=True) + RMS_EPS) for k in kn]
        beta = [beta_all[:, h:h + 1] for h in hs]
        gc = [gc_all[:, heads + h:heads + h + 1] for h in hs]
        eg = [eg_all[:, heads + h:heads + h + 1] for h in hs]
        qk_kk = [lax.dot_general(jnp.concatenate([qn[h], kn[h]], axis=0).astype(bf16), kn[h].astype(bf16),
                                 (((1,), (1,)), ((), ())), preferred_element_type=f32) for h in hs]
        decay = []
        for h in hs:
            diff = gc[h] - gc_t[heads + h:heads + h + 1, 0:chunk]
            decay.append(jnp.where(causal, jnp.exp(jnp.where(causal, diff, 0.0)), 0.0))
        qk = [qk_kk[h][:chunk, :] * decay[h] for h in hs]
        p = [-jnp.where(strict, qk_kk[h][chunk:, :] * beta[h] * decay[h], 0.0) for h in hs]
        tinv = [eye + p[h] for h in hs]
        n = 1
        while 2 * n < chunk:
            p = [_bdot(p[h], p[h]) for h in hs]
            tinv = [tinv[h] + _bdot(p[h], tinv[h]) for h in hs]
            n *= 2
        sol = [_bdot(tinv[h], jnp.concatenate([vc[:, sls[h]] * beta[h], kn[h] * (beta[h] * eg[h])], axis=1))
               for h in hs]
        s_old = [s_scr[h] for h in hs]
        ks_qs = [_bdot(jnp.concatenate([sol[h][:, dk:], qn[h] * eg[h]], axis=0), s_old[h]) for h in hs]
        u = [sol[h][:, :dk] - ks_qs[h][:chunk, :] for h in hs]
        o = [ks_qs[h][chunk:, :] + _bdot(qk[h], u[h]) for h in hs]
        for h in hs:
            gc_last = gc[h][chunk - 1:chunk, :]
            kd = kn[h] * jnp.exp(gc_last - gc[h])
            s_scr[h] = s_old[h] * jnp.exp(gc_last) + lax.dot_general(
                kd.astype(bf16), u[h].astype(bf16), (((0,), (0,)), ((), ())), preferred_element_type=f32)
        for h in hs:
            zh = z_ref[rows, sls[h]].astype(f32)
            on = o[h] * lax.rsqrt(jnp.mean(o[h] * o[h], axis=-1, keepdims=True) + RMS_EPS) * nw * _silu(zh)
            o_ref[rows, sls[h]] = on.astype(o_ref.dtype)

    if n_chunks == 1:
        do_chunk(0)
    else:
        def loop_body(ci, carry):
            do_chunk(ci)
            return carry
        lax.fori_loop(0, n_chunks, loop_body, 0)

    @pl.when(t == pl.num_programs(2) - 1)
    def _fin():
        so_ref[...] = s_scr[...]
        nbq_ref[...] = pq[...]
        nbk_ref[...] = pk[...]
        nbv_ref[...] = pv[...]


def _deltanet(proj3, ba3, par, w_conv, norm_w, hist8, s0, *, n_heads, heads_per_step, chunk, lt):
    bn, seq, _ = proj3.shape
    dk = LANES
    dnw = n_heads * dk
    hg = heads_per_step
    gw = hg * dk
    n_g = n_heads // hg
    n_t = seq // lt
    kern = functools.partial(_dn_body, chunk=chunk, heads=hg, n_chunks=lt // chunk)

    def col(off):
        return pl.BlockSpec((None, lt, gw), lambda b, g, t, off=off: (b, t, off * n_g + g))

    def cw(off):
        return pl.BlockSpec((w_conv.shape[0], gw), lambda b, g, t, off=off: (0, off * n_g + g))

    def hist(off):
        return pl.BlockSpec((None, SUBLANES, gw), lambda b, g, t, off=off: (b, 0, off * n_g + g))

    nb_spec = pl.BlockSpec((None, SUBLANES, gw), lambda b, g, t: (b, 0, g))
    nb_shape = jax.ShapeDtypeStruct((bn, SUBLANES, dnw), f32)
    return pl.pallas_call(
        kern,
        out_shape=(jax.ShapeDtypeStruct((bn, seq, dnw), bf16),
                   jax.ShapeDtypeStruct(s0.shape, f32), nb_shape, nb_shape, nb_shape),
        grid=(bn, n_g, n_t),
        in_specs=[col(0), col(1), col(2), col(3),
                  pl.BlockSpec((None, lt, LANES), lambda b, g, t: (b, t, g)),
                  pl.BlockSpec((None, SUBLANES, LANES), lambda b, g, t: (g, 0, 0)),
                  cw(0), cw(1), cw(2),
                  pl.BlockSpec((1, dk), lambda b, g, t: (0, 0)),
                  hist(0), hist(1), hist(2),
                  pl.BlockSpec((None, hg, dk, dk), lambda b, g, t: (b, g, 0, 0))],
        out_specs=(pl.BlockSpec((None, lt, gw), lambda b, g, t: (b, t, g)),
                   pl.BlockSpec((None, hg, dk, dk), lambda b, g, t: (b, g, 0, 0)),
                   nb_spec, nb_spec, nb_spec),
        scratch_shapes=[pltpu.VMEM((hg, dk, dk), f32),
                        pltpu.VMEM((SUBLANES, gw), f32), pltpu.VMEM((SUBLANES, gw), f32),
                        pltpu.VMEM((SUBLANES, gw), f32)],
        compiler_params=_cparams(("arbitrary", "arbitrary", "arbitrary")),
        name="deltanet",
    )(proj3, proj3, proj3, proj3, ba3, par, w_conv, w_conv, w_conv, norm_w.reshape(1, dk),
      hist8, hist8, hist8, s0)


def _sc_body(b_ref, c_ref, x_ref, w_ref, ha_ref, hb_ref, o_ref, tail_ref, *, seq_len):
    u = c_ref[...].astype(f32) * x_ref[...].astype(f32)
    rows = u.shape[0]
    tin = lax.broadcasted_iota(i32, u.shape, 0) % seq_len
    w = w_ref[...]
    conv = (w[2:3, :] * u
            + w[1:2, :] * jnp.where(tin >= 1, pltpu.roll(u, 1, axis=0), 0.0)
            + w[0:1, :] * jnp.where(tin >= 2, pltpu.roll(u, 2, axis=0), 0.0))
    corr = w[1:2, :] * ha_ref[...] + w[0:1, :] * hb_ref[...]
    bv = b_ref[...].astype(f32)
    if corr.shape[0] == rows:
        o_ref[...] = (bv * (conv + corr)).astype(o_ref.dtype)
        tail_ref[...] = u
    else:
        o_ref[...] = (bv * conv).astype(o_ref.dtype)
        o_ref[0:SUBLANES, :] = (bv[0:SUBLANES, :] * (conv[0:SUBLANES, :] + corr)).astype(o_ref.dtype)
        tail_ref[...] = u[rows - SUBLANES:, :]


def _short_conv(proj, col0, scw, w_sc, hist_a, hist_b, seq_len, rows_per_step, tw):
    t = proj.shape[0]
    n_seq = t // seq_len
    hr = SUBLANES if rows_per_step == seq_len else rows_per_step
    n_w = scw // tw
    cb = col0 // tw

    def pin(off):
        return pl.BlockSpec((rows_per_step, tw), lambda r, j, off=off: (r, cb + off * n_w + j))

    h_spec = pl.BlockSpec((hr, tw), lambda r, j: (r, j))
    return pl.pallas_call(
        functools.partial(_sc_body, seq_len=seq_len),
        out_shape=(jax.ShapeDtypeStruct((t, scw), bf16),
                   jax.ShapeDtypeStruct((n_seq * SUBLANES, scw), f32)),
        grid=(t // rows_per_step, n_w),
        in_specs=[pin(0), pin(1), pin(2), pl.BlockSpec((w_sc.shape[0], tw), lambda r, j: (0, j)),
                  h_spec, h_spec],
        out_specs=(pl.BlockSpec((rows_per_step, tw), lambda r, j: (r, j)),
                   pl.BlockSpec((hr, tw), lambda r, j: (r, j))),
        compiler_params=_cparams(("arbitrary", "arbitrary")),
        name="short_conv",
    )(proj, proj, proj, w_sc, hist_a, hist_b)


def _merge_body(o_ref, s_ref, gd_ref, gs_ref, wdn_ref, wsc_ref, m_ref):
    ydn = jnp.dot(o_ref[...], wdn_ref[...], preferred_element_type=f32)
    ysc = jnp.dot(s_ref[...], wsc_ref[...], preferred_element_type=f32)
    m = jax.nn.sigmoid(gd_ref[...].astype(f32)) * ydn + jax.nn.sigmoid(gs_ref[...].astype(f32)) * ysc
    m_ref[...] = m.astype(m_ref.dtype)


def _merge(o_n, scin, proj, gd_col, gs_col, w_dn, w_sc, tm):
    t, dnw = o_n.shape
    scw = scin.shape[1]
    d = w_dn.shape[1]
    return pl.pallas_call(
        _merge_body,
        out_shape=jax.ShapeDtypeStruct((t, d), bf16),
        grid=(t // tm,),
        in_specs=[pl.BlockSpec((tm, dnw), lambda m: (m, 0)),
                  pl.BlockSpec((tm, scw), lambda m: (m, 0)),
                  pl.BlockSpec((tm, d), lambda m: (m, gd_col // d)),
                  pl.BlockSpec((tm, d), lambda m: (m, gs_col // d)),
                  pl.BlockSpec((dnw, d), lambda m: (0, 0)),
                  pl.BlockSpec((scw, d), lambda m: (0, 0))],
        out_specs=pl.BlockSpec((tm, d), lambda m: (m, 0)),
        compiler_params=_cparams(("arbitrary",)),
        name="merge",
    )(o_n, scin, proj, proj, w_dn, w_sc)


def _split_dot3(a, b):
    a_hi = a.astype(bf16)
    a_lo = (a - a_hi.astype(f32)).astype(bf16)
    b_hi = b.astype(bf16)
    b_lo = (b - b_hi.astype(f32)).astype(bf16)
    return (jnp.dot(a_hi, b_hi, preferred_element_type=f32) + jnp.dot(a_hi, b_lo, preferred_element_type=f32)
            + jnp.dot(a_lo, b_hi, preferred_element_type=f32))


def _mix_body(m_ref, x_ref, g1_ref, sh2_ref, sc2_ref, wo_ref, lg_ref, lb_ref, wr_ref, rb_ref, cnt_in_ref,
              x1_ref, hp_ref, idx_ref, wt_ref, pos_ref, cnt_ref, cnt_scr, *, top_k):
    step = pl.program_id(0)

    @pl.when(step == 0)
    def _():
        cnt_scr[...] = cnt_in_ref[...]

    mix = jnp.dot(m_ref[...], wo_ref[...], preferred_element_type=f32)
    x1 = _layer_norm(DEEPNORM_ALPHA * x_ref[...] + (1.0 + g1_ref[...]) * mix, lg_ref[...], lb_ref[...])
    x1_ref[...] = x1
    h2 = x1 * (1.0 + sc2_ref[...]) + sh2_ref[...]
    half = h2.shape[1] // 2
    hp_ref[...] = _pack_pair(h2[:, :half], h2[:, half:])

    tm = h2.shape[0]
    n_e = wr_ref.shape[1]
    scores = jax.nn.sigmoid(_split_dot3(h2, wr_ref[...]))
    work = scores + rb_ref[...]
    lane_e = lax.broadcasted_iota(i32, (tm, n_e), 1).astype(f32)
    lane_o = lax.broadcasted_iota(i32, (tm, LANES), 1)
    sel = jnp.zeros((tm, n_e), f32)
    picks = []
    wsum = jnp.zeros((tm, 1), f32)
    for _ in range(top_k):
        mx = jnp.max(work, axis=-1, keepdims=True)
        idx = jnp.min(jnp.where(work == mx, lane_e, float(n_e)), axis=-1, keepdims=True)
        hit = lane_e == idx
        wk = jnp.sum(jnp.where(hit, scores, 0.0), axis=-1, keepdims=True)
        work = jnp.where(hit, -jnp.inf, work)
        sel = jnp.where(hit, 1.0, sel)
        wsum = wsum + wk
        picks.append((idx, hit, wk))

    ri = lax.broadcasted_iota(i32, (tm, tm), 0)
    ci = lax.broadcasted_iota(i32, (tm, tm), 1)
    tri = jnp.where(ri > ci, 1.0, 0.0).astype(bf16)
    before = jnp.dot(tri, sel.astype(bf16), preferred_element_type=f32) + cnt_scr[...]
    cnt_scr[...] = cnt_scr[...] + jnp.sum(sel, axis=0, keepdims=True)
    cnt_ref[...] = cnt_scr[...]

    idx_o = jnp.zeros((tm, LANES), i32)
    wt_o = jnp.zeros((tm, LANES), f32)
    pos_o = jnp.zeros((tm, LANES), i32)
    for k, (idx, hit, wk) in enumerate(picks):
        pk = jnp.sum(jnp.where(hit, before, 0.0), axis=-1, keepdims=True).astype(i32)
        idx_o = jnp.where(lane_o == k, idx.astype(i32), idx_o)
        wt_o = jnp.where(lane_o == k, wk / wsum * ROUTED_SCALE, wt_o)
        pos_o = jnp.where(lane_o == k, pk, pos_o)
    idx_ref[...] = idx_o
    wt_ref[...] = wt_o
    pos_ref[...] = pos_o


def _mix(merged, x, mod, seq_len, w_o, ln_g, ln_b, w_router, router_bias, cnt_in, tm):
    t, d = x.shape
    n_e = w_router.shape[1]
    g1_spec, sh2_spec, sc2_spec = _mod_specs(mod, tm, seq_len, (2, 3, 4), d)
    row = lambda m: (m, 0)
    const = lambda m: (0, 0)
    lane_out = jax.ShapeDtypeStruct((t, LANES), i32)
    return pl.pallas_call(
        functools.partial(_mix_body, top_k=TOP_K),
        out_shape=(jax.ShapeDtypeStruct((t, d), f32), jax.ShapeDtypeStruct((t, d // 2), u32),
                   lane_out, jax.ShapeDtypeStruct((t, LANES), f32), lane_out,
                   jax.ShapeDtypeStruct((1, n_e), f32)),
        grid=(t // tm,),
        in_specs=[pl.BlockSpec((tm, d), row), pl.BlockSpec((tm, d), row), g1_spec, sh2_spec, sc2_spec,
                  pl.BlockSpec((d, d), const), pl.BlockSpec((1, d), const), pl.BlockSpec((1, d), const),
                  pl.BlockSpec((d, n_e), const), pl.BlockSpec((1, n_e), const), pl.BlockSpec((1, n_e), const)],
        out_specs=(pl.BlockSpec((tm, d), row), pl.BlockSpec((tm, d // 2), row),
                   pl.BlockSpec((tm, LANES), row), pl.BlockSpec((tm, LANES), row), pl.BlockSpec((tm, LANES), row),
                   pl.BlockSpec((1, n_e), const)),
        scratch_shapes=[pltpu.VMEM((1, n_e), f32)],
        compiler_params=_cparams(("arbitrary",)),
        name="mix",
    )(merged, x, mod, mod, mod, w_o, ln_g.reshape(1, d), ln_b.reshape(1, d), w_router,
      router_bias.reshape(1, n_e), cnt_in)


def _row_copy(src, src_row, dst, dst_row, sem):
    return pltpu.make_async_copy(src.at[pl.ds(src_row, 1)], dst.at[pl.ds(dst_row, 1)], sem)


def _dispatch_body(flag_ref, dest_hbm, ha_ref, hb_ref, xs_out, dsm, zeros, sem_idx, sem_zero, sem,
                   *, top_k, steps_a, tile_rows, n_tiles):
    step = pl.program_id(0)
    tm = ha_ref.shape[0]

    @pl.when(step == 0)
    def _():
        zeros[...] = jnp.zeros(zeros.shape, zeros.dtype)

        def tile_copy(i):
            start = pl.multiple_of(i * tile_rows, tile_rows)
            return pltpu.make_async_copy(zeros, xs_out.at[pl.ds(start, tile_rows)], sem_zero)

        def issue_zero(i, c):
            @pl.when(flag_ref[i] == 1)
            def _():
                tile_copy(i).start()
            return c

        def drain_zero(i, c):
            @pl.when(flag_ref[i] == 1)
            def _():
                tile_copy(i).wait()
            return c

        lax.fori_loop(0, n_tiles, issue_zero, 0)
        lax.fori_loop(0, n_tiles, drain_zero, 0)

    fetch = pltpu.make_async_copy(dest_hbm.at[step], dsm, sem_idx)
    fetch.start()
    fetch.wait()

    def scatter(h_ref):
        def issue(r8, c):
            base = pl.multiple_of(r8 * SUBLANES, SUBLANES)
            for j in range(SUBLANES):
                for k in range(top_k):
                    _row_copy(h_ref, base + j, xs_out, dsm[(base + j) * top_k + k], sem).start(priority=k % 2)
            return c

        lax.fori_loop(0, tm // SUBLANES, issue, 0)

        def drain(r, c):
            for k in range(top_k):
                _row_copy(h_ref, 0, xs_out, 0, sem).wait()
            return c

        lax.fori_loop(0, tm, drain, 0)

    @pl.when(step < steps_a)
    def _():
        scatter(ha_ref)

    @pl.when(step >= steps_a)
    def _():
        scatter(hb_ref)


def _dispatch(flags, dest2d, hp_a, hp_b, rows, tile_rows, tm):
    half = hp_a.shape[1]
    steps_a = hp_a.shape[0] // tm
    steps_b = hp_b.shape[0] // tm
    n_tiles = rows // tile_rows
    kern = functools.partial(_dispatch_body, top_k=TOP_K, steps_a=steps_a, tile_rows=tile_rows, n_tiles=n_tiles)
    return pl.pallas_call(
        kern,
        out_shape=jax.ShapeDtypeStruct((rows, half), u32),
        grid_spec=pltpu.PrefetchScalarGridSpec(
            num_scalar_prefetch=1,
            grid=(steps_a + steps_b,),
            in_specs=[pl.BlockSpec(memory_space=pl.ANY),
                      pl.BlockSpec((tm, half), lambda m, fl: (jnp.minimum(m, steps_a - 1), 0)),
                      pl.BlockSpec((tm, half), lambda m, fl: (jnp.maximum(m - steps_a, 0), 0))],
            out_specs=pl.BlockSpec(memory_space=pl.ANY),
            scratch_shapes=[pltpu.SMEM((tm * TOP_K,), i32), pltpu.VMEM((tile_rows, half), u32),
                            pltpu.SemaphoreType.DMA(()), pltpu.SemaphoreType.DMA(()), pltpu.SemaphoreType.DMA(())]),
        compiler_params=_cparams(("arbitrary",)),
        name="dispatch",
    )(flags, dest2d, hp_a, hp_b)


def _expert_body(te_ref, nt_ref, first_ref, slot_ref, nxt_ref, x_ref, wg_hbm, wu_hbm, wd_hbm, o_ref,
                 wg_f, wu_f, wd_f, wg_s, wu_s, wd_s, sems):
    step = pl.program_id(0)

    def weight_copies(e, s):
        return (pltpu.make_async_copy(wg_hbm.at[e], wg_f.at[s], sems.at[s, 0]),
                pltpu.make_async_copy(wu_hbm.at[e], wu_f.at[s], sems.at[s, 1]),
                pltpu.make_async_copy(wd_hbm.at[e], wd_f.at[s], sems.at[s, 2]))

    @pl.when(step < nt_ref[0])
    def _():
        e = te_ref[step]
        s = slot_ref[step]

        @pl.when(first_ref[step] == 1)
        def _():
            @pl.when(step == 0)
            def _():
                for c in weight_copies(e, s):
                    c.start()

            for c in weight_copies(e, s):
                c.wait()
            nxt = nxt_ref[step]

            @pl.when(nxt >= 0)
            def _():
                for c in weight_copies(nxt, 1 - s):
                    c.start()

            wg_s[...] = wg_f[s].astype(bf16)
            wu_s[...] = wu_f[s].astype(bf16)
            wd_s[...] = wd_f[s].astype(bf16)

        lo, hi = _unpack_pair(x_ref[...])
        lo = lo.astype(bf16)
        hi = hi.astype(bf16)
        half = lo.shape[1]
        g = (jnp.dot(lo, wg_s[0:half, :], preferred_element_type=f32)
             + jnp.dot(hi, wg_s[half:, :], preferred_element_type=f32))
        u = (jnp.dot(lo, wu_s[0:half, :], preferred_element_type=f32)
             + jnp.dot(hi, wu_s[half:, :], preferred_element_type=f32))
        hid = (_silu(g) * u).astype(bf16)
        out = jnp.dot(hid, wd_s[...], preferred_element_type=f32)
        o_ref[...] = _pack_pair(out[:, :half], out[:, half:])


def _experts(tile_expert, n_tiles, first, slot, nxt, xs, w_gate, w_up, w_down, tm):
    rows, half = xs.shape
    n_e, d, f = w_gate.shape
    max_tiles = rows // tm

    def tile(i, te, nt, *_):
        return (jnp.minimum(i, nt[0] - 1), 0)

    hbm = pl.BlockSpec(memory_space=pl.ANY)
    return pl.pallas_call(
        _expert_body,
        out_shape=jax.ShapeDtypeStruct((rows, half), u32),
        grid_spec=pltpu.PrefetchScalarGridSpec(
            num_scalar_prefetch=5,
            grid=(max_tiles,),
            in_specs=[pl.BlockSpec((tm, half), tile), hbm, hbm, hbm],
            out_specs=pl.BlockSpec((tm, half), tile),
            scratch_shapes=[pltpu.VMEM((2, d, f), f32), pltpu.VMEM((2, d, f), f32), pltpu.VMEM((2, f, d), f32),
                            pltpu.VMEM((d, f), bf16), pltpu.VMEM((d, f), bf16), pltpu.VMEM((f, d), bf16),
                            pltpu.SemaphoreType.DMA((2, 3))]),
        input_output_aliases={5: 0},
        compiler_params=_cparams(("arbitrary",)),
        name="experts",
    )(tile_expert, n_tiles, first, slot, nxt, xs, w_gate, w_up, w_down)


def _combine_body(dest_hbm, os_hbm, x1_ref, hp_ref, wt_ref, g2_ref, lg_ref, lb_ref, wsg_ref, wsu_ref, wsd_ref,
                  y_ref, dsm0, dsm1, buf, sem_idx, sems, *, top_k):
    step = pl.program_id(0)
    tm = x1_ref.shape[0]
    slot = step % 2

    def gather(tile, s):
        dsm = (dsm0, dsm1)[s]
        fetch = pltpu.make_async_copy(dest_hbm.at[tile], dsm, sem_idx)
        fetch.start()
        fetch.wait()

        def issue(r8, c):
            base = pl.multiple_of(r8 * SUBLANES, SUBLANES)
            for j in range(SUBLANES):
                for k in range(top_k):
                    src_row = dsm[(base + j) * top_k + k]
                    _row_copy(os_hbm, src_row, buf.at[s, k], base + j, sems.at[s]).start(priority=k % 2)
            return c

        lax.fori_loop(0, tm // SUBLANES, issue, 0)

    @pl.when(step == 0)
    def _():
        gather(0, 0)

    has_next = step + 1 < pl.num_programs(0)
    for s in (0, 1):
        @pl.when(has_next & (slot == 1 - s))
        def _(s=s):
            gather(step + 1, s)

    lo, hi = _unpack_pair(hp_ref[...])
    lo = lo.astype(bf16)
    hi = hi.astype(bf16)
    half = lo.shape[1]
    g = (jnp.dot(lo, wsg_ref[0:half, :], preferred_element_type=f32)
         + jnp.dot(hi, wsg_ref[half:, :], preferred_element_type=f32))
    u = (jnp.dot(lo, wsu_ref[0:half, :], preferred_element_type=f32)
         + jnp.dot(hi, wsu_ref[half:, :], preferred_element_type=f32))
    shared = jnp.dot((_silu(g) * u).astype(bf16), wsd_ref[...], preferred_element_type=f32)

    for s in (0, 1):
        @pl.when(slot == s)
        def _(s=s):
            def drain(r, c):
                for k in range(top_k):
                    _row_copy(os_hbm, 0, buf.at[s, k], 0, sems.at[s]).wait()
                return c

            lax.fori_loop(0, tm, drain, 0)

    y_lo = shared[:, :half]
    y_hi = shared[:, half:]
    wt = wt_ref[...]
    for k in range(top_k):
        e_lo, e_hi = _unpack_pair(buf[slot, k])
        wk = wt[:, k:k + 1]
        y_lo = y_lo + wk * e_lo
        y_hi = y_hi + wk * e_hi
    ffn = jnp.concatenate([y_lo, y_hi], axis=1)
    y_ref[...] = _layer_norm(DEEPNORM_ALPHA * x1_ref[...] + (1.0 + g2_ref[...]) * ffn, lg_ref[...], lb_ref[...])


def _combine(dest2d, out_sorted, x1, hp, wts, mod, seq_len, ln_g, ln_b, ws_gate, ws_up, ws_down, tm):
    t, d = x1.shape
    half = d // 2
    sf = ws_gate.shape[1]
    (g2_spec,) = _mod_specs(mod, tm, seq_len, (5,), d)
    row = lambda m: (m, 0)
    const = lambda m: (0, 0)
    return pl.pallas_call(
        functools.partial(_combine_body, top_k=TOP_K),
        out_shape=jax.ShapeDtypeStruct((t, d), f32),
        grid=(t // tm,),
        in_specs=[pl.BlockSpec(memory_space=pl.ANY), pl.BlockSpec(memory_space=pl.ANY),
                  pl.BlockSpec((tm, d), row), pl.BlockSpec((tm, half), row), pl.BlockSpec((tm, LANES), row),
                  g2_spec, pl.BlockSpec((1, d), const), pl.BlockSpec((1, d), const),
                  pl.BlockSpec((d, sf), const), pl.BlockSpec((d, sf), const), pl.BlockSpec((sf, d), const)],
        out_specs=pl.BlockSpec((tm, d), row),
        scratch_shapes=[pltpu.SMEM((tm * TOP_K,), i32), pltpu.SMEM((tm * TOP_K,), i32),
                        pltpu.VMEM((2, TOP_K, tm, half), u32),
                        pltpu.SemaphoreType.DMA(()), pltpu.SemaphoreType.DMA((2,))],
        compiler_params=_cparams(("arbitrary",)),
        name="combine",
    )(dest2d, out_sorted, x1, hp, wts, mod, ln_g.reshape(1, d), ln_b.reshape(1, d), ws_gate, ws_up, ws_down)


def _pick(total, pref, unit=SUBLANES):
    if total <= pref:
        return total
    c = pref - pref % unit
    while total % c:
        c -= unit
    return c


def kernel(x_prompt, x_sample, state_dn_S, state_dn_conv, state_sc_conv, c_prompt, c_sample, w_ada, b_ada, w_in, w_dn_conv, dn_a_log, dn_dt_bias, dn_norm_w, w_sc_conv, w_dn_out, w_sc_out, w_o, ln1_g, ln1_b, w_router, router_bias, w_e_gate, w_e_up, w_e_down, w_s_gate, w_s_up, w_s_down, ln2_g, ln2_b):
    bp, lp, d = x_prompt.shape
    bs, ls, _ = x_sample.shape
    n_heads = dn_a_log.shape[0]
    dk = dn_norm_w.shape[0]
    assert dk == LANES
    dnw = n_heads * dk
    scw = w_sc_conv.shape[1]
    n_e = w_router.shape[1]
    tp, ts = bp * lp, bs * ls

    w_in_t = w_in.T
    n_a = 4 * dnw
    o_sc = n_a + 2 * n_heads
    o_gd = o_sc + 3 * scw
    tn_a = _pick(n_a, 1024, LANES)
    tn_c = _pick(math.gcd(2 * d, 3 * scw), 1024, LANES)
    rows_a = [i * tn_a for i in range(n_a // tn_a)]
    rows_c = [o_gd + i * tn_c for i in range(2 * d // tn_c)] + [o_sc + i * tn_c for i in range(3 * scw // tn_c)]
    c_gd = 0
    c_gs = d
    c_sc = 2 * d
    assert 2 * n_heads <= LANES and n_a + LANES <= w_in.shape[1]
    lane_pad = lambda v: jnp.concatenate([jnp.zeros((n_heads,), f32), v.astype(f32),
                                          jnp.zeros((LANES - 2 * n_heads,), f32)])[None, :]
    par = jnp.concatenate([lane_pad(dn_a_log), lane_pad(dn_dt_bias), jnp.zeros((SUBLANES - 2, LANES), f32)],
                          axis=0)[None]

    w_dn_b = w_dn_out.astype(bf16)
    w_sc_b = w_sc_out.astype(bf16)
    w_o_b = w_o.astype(bf16)
    ws_gate_b = w_s_gate.astype(bf16)
    ws_up_b = w_s_up.astype(bf16)
    ws_down_b = w_s_down.astype(bf16)

    n_c = bp + bs
    m_pad = -(-n_c // SUBLANES) * SUBLANES
    c_all = jnp.concatenate([c_prompt, c_sample, jnp.zeros((m_pad - n_c, d), f32)], axis=0)
    mod = _ada(c_all, w_ada, b_ada)
    mod_p = mod[:bp].reshape(bp, 1, 6 * d)
    mod_s = jnp.repeat(mod[bp:n_c], ls, axis=0)

    def group(x3, mod_g, s0, dn_hist, sc_hist, chunk, lt, cnt_in):
        bn, seq, _ = x3.shape
        t = bn * seq
        x = x3.reshape(t, d)
        tile = (lambda pref: _pick(seq, pref)) if mod_g.ndim == 3 else (lambda pref: _pick(t, pref))
        proj_a = _modmm(x, mod_g, seq, w_in_t, rows_a, tn_a, bf16, tile(1024))
        proj = _modmm(x, mod_g, seq, w_in_t, rows_c, tn_c, bf16, tile(1024))
        ba = _modmm(x, mod_g, seq, w_in_t, [n_a], LANES, f32, tile(1024))
        hist8 = jnp.concatenate([jnp.zeros((bn, SUBLANES - dn_hist.shape[1], 3 * dnw), f32), dn_hist], axis=1)
        o_n, s_new, nbq, nbk, nbv = _deltanet(
            proj_a.reshape(bn, seq, -1), ba.reshape(bn, seq, -1), par, w_dn_conv, dn_norm_w, hist8, s0,
            n_heads=n_heads, heads_per_step=n_heads, chunk=chunk, lt=lt)
        keep = w_dn_conv.shape[0] - 1
        dn_buf = jnp.concatenate([nbq, nbk, nbv], axis=2)[:, SUBLANES - keep:, :]

        zrow = jnp.zeros((bn, SUBLANES - 2, scw), f32)
        hist_a = jnp.concatenate([sc_hist[:, 1:2], jnp.zeros((bn, 1, scw), f32), zrow], axis=1).reshape(bn * SUBLANES, scw)
        hist_b = jnp.concatenate([sc_hist[:, 0:1], sc_hist[:, 1:2], zrow], axis=1).reshape(bn * SUBLANES, scw)
        rows_sc = seq if seq > SUBLANES else _pick(t, 256)
        scin, tail = _short_conv(proj, c_sc, scw, w_sc_conv, hist_a, hist_b, seq, rows_sc, min(scw, 256))
        sc_keep = w_sc_conv.shape[0] - 1
        sc_buf = tail.reshape(bn, SUBLANES, scw)[:, SUBLANES - sc_keep:, :]

        merged = _merge(o_n.reshape(t, dnw), scin, proj, c_gd, c_gs, w_dn_b, w_sc_b, tile(256))
        x1, hp, idx, wts, pos, cnt = _mix(merged, x, mod_g, seq, w_o_b, ln1_g, ln1_b, w_router, router_bias,
                                          cnt_in, tile(256))
        return dict(x1=x1, hp=hp, idx=idx[:, :TOP_K], wts=wts, pos=pos[:, :TOP_K], cnt=cnt,
                    s=s_new, dn_buf=dn_buf, sc_buf=sc_buf, mod=mod_g, seq=seq, tm_d=tile(256))

    zeros_s = jnp.zeros((bp,) + state_dn_S.shape[1:], f32)
    zeros_dn = jnp.zeros((bp,) + state_dn_conv.shape[1:], f32)
    zeros_sc = jnp.zeros((bp,) + state_sc_conv.shape[1:], f32)
    chunk_p = math.gcd(DN_CHUNK, lp)
    chunk_s = math.gcd(DN_CHUNK, ls)
    gp = group(x_prompt, mod_p, zeros_s, zeros_dn, zeros_sc, chunk_p, _pick(lp, 512), jnp.zeros((1, n_e), f32))
    gs = group(x_sample, mod_s, state_dn_S, state_dn_conv, state_sc_conv, chunk_s, ls, gp["cnt"])

    tm_e = 256
    counts = gs["cnt"][0].astype(i32)
    tiles_per_e = (counts + tm_e - 1) // tm_e
    tile_end = jnp.cumsum(tiles_per_e)
    start_row = (tile_end - tiles_per_e) * tm_e
    n_rows = (tp + ts) * TOP_K
    max_tiles = n_rows // tm_e + n_e
    tile_ids = jnp.arange(max_tiles, dtype=i32)
    tile_expert = jnp.minimum(jnp.sum((tile_end[None, :] <= tile_ids[:, None]).astype(i32), axis=1), n_e - 1)
    n_tiles = tile_end[-1:].astype(i32)
    e_ids = jnp.arange(n_e, dtype=i32)
    nonempty = tiles_per_e > 0
    ordinal = jnp.cumsum(nonempty.astype(i32)) - 1
    later = (e_ids[None, :] > e_ids[:, None]) & nonempty[None, :]
    next_e = jnp.min(jnp.where(later, e_ids[None, :], n_e), axis=1)
    next_e = jnp.where(next_e == n_e, -1, next_e).astype(i32)
    onehot_te = tile_expert[:, None] == e_ids[None, :]
    lookup = lambda table: jnp.sum(jnp.where(onehot_te, table[None, :], 0), axis=1).astype(i32)
    first = (tile_ids == lookup(tile_end - tiles_per_e)).astype(i32)
    slot = lookup(ordinal) % 2
    nxt = lookup(next_e)

    for g in (gp, gs):
        t = g["x1"].shape[0]
        idx_flat = g["idx"].reshape(-1)
        start_flat = jnp.sum(jnp.where(idx_flat[:, None] == e_ids[None, :], start_row[None, :], 0), axis=1)
        g["dest"] = (start_flat + g["pos"].reshape(-1)).astype(i32).reshape(t // g["tm_d"], g["tm_d"] * TOP_K)
    assert gp["tm_d"] == gs["tm_d"]
    zero_flag = ((tile_ids >= n_tiles[0]) | (tile_ids == lookup(tile_end) - 1)).astype(i32)
    xs = _dispatch(zero_flag, jnp.concatenate([gp["dest"], gs["dest"]], axis=0), gp["hp"], gs["hp"],
                   max_tiles * tm_e, tm_e, gp["tm_d"])
    out_sorted = _experts(tile_expert, n_tiles, first, slot, nxt, xs, w_e_gate, w_e_up, w_e_down, tm_e)
    ys = [_combine(g["dest"], out_sorted, g["x1"], g["hp"], g["wts"], g["mod"], g["seq"], ln2_g, ln2_b,
                   ws_gate_b, ws_up_b, ws_down_b, g["tm_d"]) for g in (gp, gs)]

    y_p = ys[0].reshape(bp, lp, d)
    y_s = ys[1].reshape(bs, ls, d)
    return (y_p, y_s, gp["s"], gp["dn_buf"], gp["sc_buf"], gs["s"], gs["dn_buf"], gs["sc_buf"])
```

```python
import functools
import math

import jax
import jax.numpy as jnp
from jax import lax
from jax.experimental import pallas as pl
from jax.experimental.pallas import tpu as pltpu

f32 = jnp.float32
bf16 = jnp.bfloat16
i32 = jnp.int32
u32 = jnp.uint32

TOP_K = 8
ROUTED_SCALE = 2.5
DN_CHUNK = 64
DEPTH = 1
DEEPNORM_ALPHA = (2 * DEPTH) ** 0.25
LN_EPS = 1e-5
RMS_EPS = 1e-6

LANES = 128
SUBLANES = 8
VMEM_LIMIT = 56 * 1024 * 1024


def _cparams(sem, vmem=VMEM_LIMIT):
    return pltpu.CompilerParams(dimension_semantics=sem, vmem_limit_bytes=vmem)


def _silu(x):
    return x * jax.nn.sigmoid(x)


def _bdot(a, b):
    return jnp.dot(a.astype(bf16), b.astype(bf16), preferred_element_type=f32)


def _layer_norm(x, g, b):
    mu = jnp.mean(x, axis=-1, keepdims=True)
    xc = x - mu
    var = jnp.mean(xc * xc, axis=-1, keepdims=True)
    return xc * lax.rsqrt(var + LN_EPS) * g + b


def _pack_pair(a, b):
    ab = pltpu.bitcast(a.astype(bf16).astype(f32), u32)
    bb = pltpu.bitcast(b.astype(bf16).astype(f32), u32)
    return (ab >> 16) | (bb & jnp.uint32(0xFFFF0000))


def _unpack_pair(p):
    lo = pltpu.bitcast(p << 16, f32)
    hi = pltpu.bitcast(p & jnp.uint32(0xFFFF0000), f32)
    return lo, hi


def _ada_body(c_ref, w_ref, b_ref, o_ref):
    s = _silu(c_ref[...])
    o_ref[...] = _bdot(s, w_ref[...]) + b_ref[...]


def _ada(c_all, w_ada, b_ada):
    m, d = c_all.shape
    n = w_ada.shape[1]
    tn = _pick(n, 1024, LANES)
    return pl.pallas_call(
        _ada_body,
        out_shape=jax.ShapeDtypeStruct((m, n), f32),
        grid=(n // tn,),
        in_specs=[pl.BlockSpec((m, d), lambda j: (0, 0)),
                  pl.BlockSpec((d, tn), lambda j: (0, j)),
                  pl.BlockSpec((1, tn), lambda j: (0, j))],
        out_specs=pl.BlockSpec((m, tn), lambda j: (0, j)),
        compiler_params=_cparams(("arbitrary",)),
        name="ada",
    )(c_all, w_ada, b_ada.reshape(1, n))


def _modmm_body(off_ref, x_ref, sc_ref, sh_ref, w_ref, o_ref, h_scr):
    del off_ref

    @pl.when(pl.program_id(1) == 0)
    def _():
        h_scr[...] = (x_ref[...] * (1.0 + sc_ref[...]) + sh_ref[...]).astype(bf16)

    o_ref[...] = lax.dot_general(h_scr[...], w_ref[...].astype(bf16), (((1,), (1,)), ((), ())),
                                 preferred_element_type=f32).astype(o_ref.dtype)


def _mod_specs(mod, tm, seq_len, cols, d):
    if mod.ndim == 3:
        tiles_per_seq = seq_len // tm
        return [pl.BlockSpec((None, 1, d), lambda m, *_, c=c: (m // tiles_per_seq, 0, c)) for c in cols]
    return [pl.BlockSpec((tm, d), lambda m, *_, c=c: (m, c)) for c in cols]


def _modmm(x, mod, seq_len, w_t, row_starts, tn, out_dtype, tm):
    t, d = x.shape
    assert all(r % SUBLANES == 0 for r in row_starts)
    n_blocks = len(row_starts)
    sc_spec, sh_spec = _mod_specs(mod, tm, seq_len, (1, 0), d)
    return pl.pallas_call(
        _modmm_body,
        out_shape=jax.ShapeDtypeStruct((t, n_blocks * tn), out_dtype),
        grid_spec=pltpu.PrefetchScalarGridSpec(
            num_scalar_prefetch=1,
            grid=(t // tm, n_blocks),
            in_specs=[pl.BlockSpec((tm, d), lambda m, j, off: (m, 0)), sc_spec, sh_spec,
                      pl.BlockSpec((pl.Element(tn), pl.Element(d)), lambda m, j, off: (off[j] * SUBLANES, 0))],
            out_specs=pl.BlockSpec((tm, tn), lambda m, j, off: (m, j)),
            scratch_shapes=[pltpu.VMEM((tm, d), bf16)]),
        compiler_params=_cparams(("arbitrary", "arbitrary")),
        name="modmm",
    )(jnp.asarray([r // SUBLANES for r in row_starts], i32), x, mod, mod, w_t)


def _cumsum_rows(x, n):
    row = lax.broadcasted_iota(i32, x.shape, 0)
    s = 1
    while s < n:
        x = x + jnp.where(row >= s, pltpu.roll(x, s, axis=0), 0.0)
        s *= 2
    return x


def _conv_silu(x, prev, w):
    c = x.shape[0]
    taps = w.shape[0]
    xc = jnp.concatenate([prev, x], axis=0)
    y = w[taps - 1:taps, :] * x
    for i in range(taps - 1):
        back = taps - 1 - i
        y = y + w[i:i + 1, :] * pltpu.roll(xc, back, axis=0)[SUBLANES:SUBLANES + c, :]
    return _silu(y)


def _dn_body(q_ref, k_ref, v_ref, z_ref, ba_ref, par_ref, cwq_ref, cwk_ref, cwv_ref, nw_ref,
             hq_ref, hk_ref, hv_ref, s0_ref,
             o_ref, so_ref, nbq_ref, nbk_ref, nbv_ref,
             s_scr, pq, pk, pv, *, chunk, heads, n_chunks, n_seq):
    t = pl.program_id(2)
    dk = LANES

    @pl.when(t == 0)
    def _init():
        s_scr[...] = s0_ref[...]
        pq[...] = hq_ref[...]
        pk[...] = hk_ref[...]
        pv[...] = hv_ref[...]

    neg_a = -jnp.exp(par_ref[0:1, :])
    dt_b = par_ref[1:2, :]
    nw = nw_ref[...]
    ii = lax.broadcasted_iota(i32, (chunk, chunk), 0)
    jj = lax.broadcasted_iota(i32, (chunk, chunk), 1)
    causal = ii >= jj
    strict = ii > jj
    eye = (ii == jj).astype(f32)
    zpad = jnp.zeros((LANES - chunk, LANES), f32)
    sls = [slice(h * dk, (h + 1) * dk) for h in range(heads)]
    pairs = [(s, h) for s in range(n_seq) for h in range(heads)]
    ps = range(len(pairs))

    def do_chunk(ci):
        r0 = pl.multiple_of(ci * chunk, chunk)
        rows = pl.ds(r0, chunk)
        beta_all, gc_all, gc_t, eg_all, qc, kc, vc = [], [], [], [], [], [], []
        for s in range(n_seq):
            ba = ba_ref[s, rows, :]
            beta_all.append(jax.nn.sigmoid(ba))
            xs = ba + dt_b
            softplus = jnp.maximum(xs, 0.0) + jnp.log1p(jnp.exp(-jnp.abs(xs)))
            g = _cumsum_rows(neg_a * softplus, chunk)
            gc_all.append(g)
            gc_t.append(jnp.concatenate([g, zpad], axis=0).T)
            eg_all.append(jnp.exp(g))
            xq = q_ref[s, rows, :].astype(f32)
            xk = k_ref[s, rows, :].astype(f32)
            xv = v_ref[s, rows, :].astype(f32)
            qc.append(_conv_silu(xq, pq[s], cwq_ref[...]))
            kc.append(_conv_silu(xk, pk[s], cwk_ref[...]))
            vc.append(_conv_silu(xv, pv[s], cwv_ref[...]))
            pq[s] = xq[chunk - SUBLANES:, :]
            pk[s] = xk[chunk - SUBLANES:, :]
            pv[s] = xv[chunk - SUBLANES:, :]

        qn = [qc[s][:, sls[h]] for s, h in pairs]
        kn = [kc[s][:, sls[h]] for s, h in pairs]
        vn = [vc[s][:, sls[h]] for s, h in pairs]
        qn = [q * (lax.rsqrt(jnp.sum(q * q, axis=-1, keepdims=True) + RMS_EPS) * (dk ** -0.5)) for q in qn]
        kn = [k * lax.rsqrt(jnp.sum(k * k, axis=-1, keepdims=True) + RMS_EPS) for k in kn]
        beta = [beta_all[s][:, h:h + 1] for s, h in pairs]
        gc = [gc_all[s][:, heads + h:heads + h + 1] for s, h in pairs]
        eg = [eg_all[s][:, heads + h:heads + h + 1] for s, h in pairs]
        qk_kk = [lax.dot_general(jnp.concatenate([qn[i], kn[i]], axis=0).astype(bf16), kn[i].astype(bf16),
                                 (((1,), (1,)), ((), ())), preferred_element_type=f32) for i in ps]
        decay = []
        for i, (s, h) in enumerate(pairs):
            diff = gc[i] - gc_t[s][heads + h:heads + h + 1, 0:chunk]
            decay.append(jnp.where(causal, jnp.exp(jnp.where(causal, diff, 0.0)), 0.0))
        qk = [qk_kk[i][:chunk, :] * decay[i] for i in ps]
        p = [-jnp.where(strict, qk_kk[i][chunk:, :] * beta[i] * decay[i], 0.0) for i in ps]
        tinv = [eye + p[i] for i in ps]
        n = 1
        while 2 * n < chunk:
            p = [_bdot(p[i], p[i]) for i in ps]
            tinv = [tinv[i] + _bdot(p[i], tinv[i]) for i in ps]
            n *= 2
        sol = [_bdot(tinv[i], jnp.concatenate([vn[i] * beta[i], kn[i] * (beta[i] * eg[i])], axis=1))
               for i in ps]
        s_old = [s_scr[s, h] for s, h in pairs]
        ks_qs = [_bdot(jnp.concatenate([sol[i][:, dk:], qn[i] * eg[i]], axis=0), s_old[i]) for i in ps]
        u = [sol[i][:, :dk] - ks_qs[i][:chunk, :] for i in ps]
        o = [ks_qs[i][chunk:, :] + _bdot(qk[i], u[i]) for i in ps]
        for i, (s, h) in enumerate(pairs):
            gc_last = gc[i][chunk - 1:chunk, :]
            kd = kn[i] * jnp.exp(gc_last - gc[i])
            s_scr[s, h] = s_old[i] * jnp.exp(gc_last) + lax.dot_general(
                kd.astype(bf16), u[i].astype(bf16), (((0,), (0,)), ((), ())), preferred_element_type=f32)
        for i, (s, h) in enumerate(pairs):
            zh = z_ref[s, rows, sls[h]].astype(f32)
            on = o[i] * lax.rsqrt(jnp.mean(o[i] * o[i], axis=-1, keepdims=True) + RMS_EPS) * nw * _silu(zh)
            o_ref[s, rows, sls[h]] = on.astype(o_ref.dtype)

    if n_chunks == 1:
        do_chunk(0)
    else:
        def loop_body(ci, carry):
            do_chunk(ci)
            return carry
        lax.fori_loop(0, n_chunks, loop_body, 0)

    @pl.when(t == pl.num_programs(2) - 1)
    def _fin():
        so_ref[...] = s_scr[...]
        nbq_ref[...] = pq[...]
        nbk_ref[...] = pk[...]
        nbv_ref[...] = pv[...]


def _deltanet(proj3, ba3, par, w_conv, norm_w, hist8, s0, *, n_heads, heads_per_step, seqs_per_step, chunk, lt):
    bn, seq, _ = proj3.shape
    dk = LANES
    dnw = n_heads * dk
    hg = heads_per_step
    nb = seqs_per_step
    gw = hg * dk
    n_g = n_heads // hg
    n_t = seq // lt
    kern = functools.partial(_dn_body, chunk=chunk, heads=hg, n_chunks=lt // chunk, n_seq=nb)

    def col(off):
        return pl.BlockSpec((nb, lt, gw), lambda b, g, t, off=off: (b, t, off * n_g + g))

    def cw(off):
        return pl.BlockSpec((w_conv.shape[0], gw), lambda b, g, t, off=off: (0, off * n_g + g))

    def hist(off):
        return pl.BlockSpec((nb, SUBLANES, gw), lambda b, g, t, off=off: (b, 0, off * n_g + g))

    nb_spec = pl.BlockSpec((nb, SUBLANES, gw), lambda b, g, t: (b, 0, g))
    nb_shape = jax.ShapeDtypeStruct((bn, SUBLANES, dnw), f32)
    return pl.pallas_call(
        kern,
        out_shape=(jax.ShapeDtypeStruct((bn, seq, dnw), bf16),
                   jax.ShapeDtypeStruct(s0.shape, f32), nb_shape, nb_shape, nb_shape),
        grid=(bn // nb, n_g, n_t),
        in_specs=[col(0), col(1), col(2), col(3),
                  pl.BlockSpec((nb, lt, LANES), lambda b, g, t: (b, t, g)),
                  pl.BlockSpec((None, SUBLANES, LANES), lambda b, g, t: (g, 0, 0)),
                  cw(0), cw(1), cw(2),
                  pl.BlockSpec((1, dk), lambda b, g, t: (0, 0)),
                  hist(0), hist(1), hist(2),
                  pl.BlockSpec((nb, hg, dk, dk), lambda b, g, t: (b, g, 0, 0))],
        out_specs=(pl.BlockSpec((nb, lt, gw), lambda b, g, t: (b, t, g)),
                   pl.BlockSpec((nb, hg, dk, dk), lambda b, g, t: (b, g, 0, 0)),
                   nb_spec, nb_spec, nb_spec),
        scratch_shapes=[pltpu.VMEM((nb, hg, dk, dk), f32),
                        pltpu.VMEM((nb, SUBLANES, gw), f32), pltpu.VMEM((nb, SUBLANES, gw), f32),
                        pltpu.VMEM((nb, SUBLANES, gw), f32)],
        compiler_params=_cparams(("arbitrary", "arbitrary", "arbitrary")),
        name="deltanet",
    )(proj3, proj3, proj3, proj3, ba3, par, w_conv, w_conv, w_conv, norm_w.reshape(1, dk),
      hist8, hist8, hist8, s0)


def _sc_body(b_ref, c_ref, x_ref, w_ref, ha_ref, hb_ref, o_ref, tail_ref, *, seq_len):
    u = c_ref[...].astype(f32) * x_ref[...].astype(f32)
    rows = u.shape[0]
    tin = lax.broadcasted_iota(i32, u.shape, 0) % seq_len
    w = w_ref[...]
    conv = (w[2:3, :] * u
            + w[1:2, :] * jnp.where(tin >= 1, pltpu.roll(u, 1, axis=0), 0.0)
            + w[0:1, :] * jnp.where(tin >= 2, pltpu.roll(u, 2, axis=0), 0.0))
    corr = w[1:2, :] * ha_ref[...] + w[0:1, :] * hb_ref[...]
    bv = b_ref[...].astype(f32)
    if corr.shape[0] == rows:
        o_ref[...] = (bv * (conv + corr)).astype(o_ref.dtype)
        tail_ref[...] = u
    else:
        o_ref[...] = (bv * conv).astype(o_ref.dtype)
        o_ref[0:SUBLANES, :] = (bv[0:SUBLANES, :] * (conv[0:SUBLANES, :] + corr)).astype(o_ref.dtype)
        tail_ref[...] = u[rows - SUBLANES:, :]


def _short_conv(proj, col0, scw, w_sc, hist_a, hist_b, seq_len, rows_per_step, tw):
    t = proj.shape[0]
    n_seq = t // seq_len
    hr = SUBLANES if rows_per_step == seq_len else rows_per_step
    n_w = scw // tw
    cb = col0 // tw

    def pin(off):
        return pl.BlockSpec((rows_per_step, tw), lambda r, j, off=off: (r, cb + off * n_w + j))

    h_spec = pl.BlockSpec((hr, tw), lambda r, j: (r, j))
    return pl.pallas_call(
        functools.partial(_sc_body, seq_len=seq_len),
        out_shape=(jax.ShapeDtypeStruct((t, scw), bf16),
                   jax.ShapeDtypeStruct((n_seq * SUBLANES, scw), f32)),
        grid=(t // rows_per_step, n_w),
        in_specs=[pin(0), pin(1), pin(2), pl.BlockSpec((w_sc.shape[0], tw), lambda r, j: (0, j)),
                  h_spec, h_spec],
        out_specs=(pl.BlockSpec((rows_per_step, tw), lambda r, j: (r, j)),
                   pl.BlockSpec((hr, tw), lambda r, j: (r, j))),
        compiler_params=_cparams(("arbitrary", "arbitrary")),
        name="short_conv",
    )(proj, proj, proj, w_sc, hist_a, hist_b)


def _merge_body(o_ref, s_ref, gd_ref, gs_ref, wdn_ref, wsc_ref, m_ref):
    ydn = jnp.dot(o_ref[...], wdn_ref[...], preferred_element_type=f32)
    ysc = jnp.dot(s_ref[...], wsc_ref[...], preferred_element_type=f32)
    m = jax.nn.sigmoid(gd_ref[...].astype(f32)) * ydn + jax.nn.sigmoid(gs_ref[...].astype(f32)) * ysc
    m_ref[...] = m.astype(m_ref.dtype)


def _merge(o_n, scin, proj, gd_col, gs_col, w_dn, w_sc, tm):
    t, dnw = o_n.shape
    scw = scin.shape[1]
    d = w_dn.shape[1]
    return pl.pallas_call(
        _merge_body,
        out_shape=jax.ShapeDtypeStruct((t, d), bf16),
        grid=(t // tm,),
        in_specs=[pl.BlockSpec((tm, dnw), lambda m: (m, 0)),
                  pl.BlockSpec((tm, scw), lambda m: (m, 0)),
                  pl.BlockSpec((tm, d), lambda m: (m, gd_col // d)),
                  pl.BlockSpec((tm, d), lambda m: (m, gs_col // d)),
                  pl.BlockSpec((dnw, d), lambda m: (0, 0)),
                  pl.BlockSpec((scw, d), lambda m: (0, 0))],
        out_specs=pl.BlockSpec((tm, d), lambda m: (m, 0)),
        compiler_params=_cparams(("arbitrary",)),
        name="merge",
    )(o_n, scin, proj, proj, w_dn, w_sc)


def _split_dot3(a, b):
    a_hi = a.astype(bf16)
    a_lo = (a - a_hi.astype(f32)).astype(bf16)
    b_hi = b.astype(bf16)
    b_lo = (b - b_hi.astype(f32)).astype(bf16)
    return (jnp.dot(a_hi, b_hi, preferred_element_type=f32) + jnp.dot(a_hi, b_lo, preferred_element_type=f32)
            + jnp.dot(a_lo, b_hi, preferred_element_type=f32))


def _mix_body(m_ref, x_ref, g1_ref, sh2_ref, sc2_ref, wo_ref, lg_ref, lb_ref, wr_ref, rb_ref, cnt_in_ref,
              x1_ref, hp_ref, idx_ref, wt_ref, pos_ref, cnt_ref, cnt_scr, *, top_k):
    step = pl.program_id(0)

    @pl.when(step == 0)
    def _():
        cnt_scr[...] = cnt_in_ref[...]

    mix = jnp.dot(m_ref[...], wo_ref[...], preferred_element_type=f32)
    x1 = _layer_norm(DEEPNORM_ALPHA * x_ref[...] + (1.0 + g1_ref[...]) * mix, lg_ref[...], lb_ref[...])
    x1_ref[...] = x1
    h2 = x1 * (1.0 + sc2_ref[...]) + sh2_ref[...]
    half = h2.shape[1] // 2
    hp_ref[...] = _pack_pair(h2[:, :half], h2[:, half:])

    tm = h2.shape[0]
    n_e = wr_ref.shape[1]
    scores = jax.nn.sigmoid(_split_dot3(h2, wr_ref[...]))
    work = scores + rb_ref[...]
    lane_e = lax.broadcasted_iota(i32, (tm, n_e), 1).astype(f32)
    lane_o = lax.broadcasted_iota(i32, (tm, LANES), 1)
    sel = jnp.zeros((tm, n_e), f32)
    picks = []
    wsum = jnp.zeros((tm, 1), f32)
    for _ in range(top_k):
        mx = jnp.max(work, axis=-1, keepdims=True)
        idx = jnp.min(jnp.where(work == mx, lane_e, float(n_e)), axis=-1, keepdims=True)
        hit = lane_e == idx
        wk = jnp.sum(jnp.where(hit, scores, 0.0), axis=-1, keepdims=True)
        work = jnp.where(hit, -jnp.inf, work)
        sel = jnp.where(hit, 1.0, sel)
        wsum = wsum + wk
        picks.append((idx, hit, wk))

    ri = lax.broadcasted_iota(i32, (tm, tm), 0)
    ci = lax.broadcasted_iota(i32, (tm, tm), 1)
    tri = jnp.where(ri > ci, 1.0, 0.0).astype(bf16)
    before = jnp.dot(tri, sel.astype(bf16), preferred_element_type=f32) + cnt_scr[...]
    cnt_scr[...] = cnt_scr[...] + jnp.sum(sel, axis=0, keepdims=True)
    cnt_ref[...] = cnt_scr[...]

    idx_o = jnp.zeros((tm, LANES), i32)
    wt_o = jnp.zeros((tm, LANES), f32)
    pos_o = jnp.zeros((tm, LANES), i32)
    for k, (idx, hit, wk) in enumerate(picks):
        pk = jnp.sum(jnp.where(hit, before, 0.0), axis=-1, keepdims=True).astype(i32)
        idx_o = jnp.where(lane_o == k, idx.astype(i32), idx_o)
        wt_o = jnp.where(lane_o == k, wk / wsum * ROUTED_SCALE, wt_o)
        pos_o = jnp.where(lane_o == k, pk, pos_o)
    idx_ref[...] = idx_o
    wt_ref[...] = wt_o
    pos_ref[...] = pos_o


def _mix(merged, x, mod, seq_len, w_o, ln_g, ln_b, w_router, router_bias, cnt_in, tm):
    t, d = x.shape
    n_e = w_router.shape[1]
    g1_spec, sh2_spec, sc2_spec = _mod_specs(mod, tm, seq_len, (2, 3, 4), d)
    row = lambda m: (m, 0)
    const = lambda m: (0, 0)
    lane_out = jax.ShapeDtypeStruct((t, LANES), i32)
    return pl.pallas_call(
        functools.partial(_mix_body, top_k=TOP_K),
        out_shape=(jax.ShapeDtypeStruct((t, d), f32), jax.ShapeDtypeStruct((t, d // 2), u32),
                   lane_out, jax.ShapeDtypeStruct((t, LANES), f32), lane_out,
                   jax.ShapeDtypeStruct((1, n_e), f32)),
        grid=(t // tm,),
        in_specs=[pl.BlockSpec((tm, d), row), pl.BlockSpec((tm, d), row), g1_spec, sh2_spec, sc2_spec,
                  pl.BlockSpec((d, d), const), pl.BlockSpec((1, d), const), pl.BlockSpec((1, d), const),
                  pl.BlockSpec((d, n_e), const), pl.BlockSpec((1, n_e), const), pl.BlockSpec((1, n_e), const)],
        out_specs=(pl.BlockSpec((tm, d), row), pl.BlockSpec((tm, d // 2), row),
                   pl.BlockSpec((tm, LANES), row), pl.BlockSpec((tm, LANES), row), pl.BlockSpec((tm, LANES), row),
                   pl.BlockSpec((1, n_e), const)),
        scratch_shapes=[pltpu.VMEM((1, n_e), f32)],
        compiler_params=_cparams(("arbitrary",)),
        name="mix",
    )(merged, x, mod, mod, mod, w_o, ln_g.reshape(1, d), ln_b.reshape(1, d), w_router,
      router_bias.reshape(1, n_e), cnt_in)


def _row_copy(src, src_row, dst, dst_row, sem):
    return pltpu.make_async_copy(src.at[pl.ds(src_row, 1)], dst.at[pl.ds(dst_row, 1)], sem)


def _dispatch_body(flag_ref, dest_hbm, ha_ref, hb_ref, xs_out, dsm0, dsm1, zeros, sem_idx, sem_zero, sem,
                   *, top_k, steps_a, tile_rows, n_tiles):
    step = pl.program_id(0)
    tm = ha_ref.shape[0]

    @pl.when(step == 0)
    def _():
        zeros[...] = jnp.zeros(zeros.shape, zeros.dtype)

        def tile_copy(i):
            start = pl.multiple_of(i * tile_rows, tile_rows)
            return pltpu.make_async_copy(zeros, xs_out.at[pl.ds(start, tile_rows)], sem_zero)

        def issue_zero(i, c):
            @pl.when(flag_ref[i] == 1)
            def _():
                tile_copy(i).start()
            return c

        def drain_zero(i, c):
            @pl.when(flag_ref[i] == 1)
            def _():
                tile_copy(i).wait()
            return c

        lax.fori_loop(0, n_tiles, issue_zero, 0)
        lax.fori_loop(0, n_tiles, drain_zero, 0)

    dsms = (dsm0, dsm1)
    slot = step % 2
    has_next = step + 1 < pl.num_programs(0)

    def table_fetch(tile, s):
        return pltpu.make_async_copy(dest_hbm.at[tile], dsms[s], sem_idx)

    @pl.when(step == 0)
    def _():
        first = table_fetch(0, 0)
        first.start()
        first.wait()

    for s in (0, 1):
        @pl.when(has_next & (slot == 1 - s))
        def _(s=s):
            table_fetch(step + 1, s).start()

    def scatter(h_ref, dsm):
        def issue(r8, c):
            base = pl.multiple_of(r8 * SUBLANES, SUBLANES)
            for j in range(SUBLANES):
                for k in range(top_k):
                    _row_copy(h_ref, base + j, xs_out, dsm[(base + j) * top_k + k], sem).start(priority=k % 2)
            return c

        lax.fori_loop(0, tm // SUBLANES, issue, 0)

        def drain(r, c):
            for k in range(top_k):
                _row_copy(h_ref, 0, xs_out, 0, sem).wait()
            return c

        lax.fori_loop(0, tm, drain, 0)

    for s in (0, 1):
        @pl.when((step < steps_a) & (slot == s))
        def _(s=s):
            scatter(ha_ref, dsms[s])

        @pl.when((step >= steps_a) & (slot == s))
        def _(s=s):
            scatter(hb_ref, dsms[s])

    for s in (0, 1):
        @pl.when(has_next & (slot == 1 - s))
        def _(s=s):
            table_fetch(step + 1, s).wait()


def _dispatch(flags, dest2d, hp_a, hp_b, rows, tile_rows, tm):
    half = hp_a.shape[1]
    steps_a = hp_a.shape[0] // tm
    steps_b = hp_b.shape[0] // tm
    n_tiles = rows // tile_rows
    kern = functools.partial(_dispatch_body, top_k=TOP_K, steps_a=steps_a, tile_rows=tile_rows, n_tiles=n_tiles)
    return pl.pallas_call(
        kern,
        out_shape=jax.ShapeDtypeStruct((rows, half), u32),
        grid_spec=pltpu.PrefetchScalarGridSpec(
            num_scalar_prefetch=1,
            grid=(steps_a + steps_b,),
            in_specs=[pl.BlockSpec(memory_space=pl.ANY),
                      pl.BlockSpec((tm, half), lambda m, fl: (jnp.minimum(m, steps_a - 1), 0)),
                      pl.BlockSpec((tm, half), lambda m, fl: (jnp.maximum(m - steps_a, 0), 0))],
            out_specs=pl.BlockSpec(memory_space=pl.ANY),
            scratch_shapes=[pltpu.SMEM((tm * TOP_K,), i32), pltpu.SMEM((tm * TOP_K,), i32),
                            pltpu.VMEM((tile_rows, half), u32),
                            pltpu.SemaphoreType.DMA(()), pltpu.SemaphoreType.DMA(()), pltpu.SemaphoreType.DMA(())]),
        compiler_params=_cparams(("arbitrary",)),
        name="dispatch",
    )(flags, dest2d, hp_a, hp_b)


def _expert_body(te_ref, nt_ref, first_ref, slot_ref, nxt_ref, x_ref, wg_hbm, wu_hbm, wd_hbm, o_ref,
                 wg_f, wu_f, wd_f, wg_s, wu_s, wd_s, sems):
    step = pl.program_id(0)

    def weight_copies(e, s):
        return (pltpu.make_async_copy(wg_hbm.at[e], wg_f.at[s], sems.at[s, 0]),
                pltpu.make_async_copy(wu_hbm.at[e], wu_f.at[s], sems.at[s, 1]),
                pltpu.make_async_copy(wd_hbm.at[e], wd_f.at[s], sems.at[s, 2]))

    @pl.when(step < nt_ref[0])
    def _():
        e = te_ref[step]
        s = slot_ref[step]

        @pl.when(first_ref[step] == 1)
        def _():
            @pl.when(step == 0)
            def _():
                for c in weight_copies(e, s):
                    c.start()

            for c in weight_copies(e, s):
                c.wait()
            nxt = nxt_ref[step]

            @pl.when(nxt >= 0)
            def _():
                for c in weight_copies(nxt, 1 - s):
                    c.start()

            wg_s[...] = wg_f[s].astype(bf16)
            wu_s[...] = wu_f[s].astype(bf16)
            wd_s[...] = wd_f[s].astype(bf16)

        lo, hi = _unpack_pair(x_ref[...])
        lo = lo.astype(bf16)
        hi = hi.astype(bf16)
        half = lo.shape[1]
        g = (jnp.dot(lo, wg_s[0:half, :], preferred_element_type=f32)
             + jnp.dot(hi, wg_s[half:, :], preferred_element_type=f32))
        u = (jnp.dot(lo, wu_s[0:half, :], preferred_element_type=f32)
             + jnp.dot(hi, wu_s[half:, :], preferred_element_type=f32))
        hid = (_silu(g) * u).astype(bf16)
        out = jnp.dot(hid, wd_s[...], preferred_element_type=f32)
        o_ref[...] = _pack_pair(out[:, :half], out[:, half:])


def _experts(tile_expert, n_tiles, first, slot, nxt, xs, w_gate, w_up, w_down, tm):
    rows, half = xs.shape
    n_e, d, f = w_gate.shape
    max_tiles = rows // tm

    def tile(i, te, nt, *_):
        return (jnp.minimum(i, nt[0] - 1), 0)

    hbm = pl.BlockSpec(memory_space=pl.ANY)
    return pl.pallas_call(
        _expert_body,
        out_shape=jax.ShapeDtypeStruct((rows, half), u32),
        grid_spec=pltpu.PrefetchScalarGridSpec(
            num_scalar_prefetch=5,
            grid=(max_tiles,),
            in_specs=[pl.BlockSpec((tm, half), tile), hbm, hbm, hbm],
            out_specs=pl.BlockSpec((tm, half), tile),
            scratch_shapes=[pltpu.VMEM((2, d, f), f32), pltpu.VMEM((2, d, f), f32), pltpu.VMEM((2, f, d), f32),
                            pltpu.VMEM((d, f), bf16), pltpu.VMEM((d, f), bf16), pltpu.VMEM((f, d), bf16),
                            pltpu.SemaphoreType.DMA((2, 3))]),
        input_output_aliases={5: 0},
        compiler_params=_cparams(("arbitrary",)),
        name="experts",
    )(tile_expert, n_tiles, first, slot, nxt, xs, w_gate, w_up, w_down)


def _combine_body(dest_hbm, os_hbm, x1_ref, hp_ref, wt_ref, g2_ref, lg_ref, lb_ref, wsg_ref, wsu_ref, wsd_ref,
                  y_ref, dsm0, dsm1, buf, sem_idx, sems, *, top_k, n_steps):
    step = pl.program_id(0)
    tm = x1_ref.shape[0]
    slot = step % 2

    dsms = (dsm0, dsm1)

    def table_fetch(tile, s):
        return pltpu.make_async_copy(dest_hbm.at[tile], dsms[s], sem_idx)

    def gather(s):
        def issue(r8, c):
            base = pl.multiple_of(r8 * SUBLANES, SUBLANES)
            for j in range(SUBLANES):
                for k in range(top_k):
                    src_row = dsms[s][(base + j) * top_k + k]
                    _row_copy(os_hbm, src_row, buf.at[s, k], base + j, sems.at[s]).start(priority=k % 2)
            return c

        lax.fori_loop(0, tm // SUBLANES, issue, 0)

    @pl.when(step == 0)
    def _():
        first = table_fetch(0, 0)
        first.start()
        first.wait()
        gather(0)
        if n_steps > 1:
            second = table_fetch(1, 1)
            second.start()
            second.wait()

    has_next = step + 1 < n_steps
    has_next2 = step + 2 < n_steps
    for s in (0, 1):
        @pl.when(has_next & (slot == 1 - s))
        def _(s=s):
            gather(s)

        @pl.when(has_next2 & (slot == s))
        def _(s=s):
            table_fetch(step + 2, s).start()

    lo, hi = _unpack_pair(hp_ref[...])
    lo = lo.astype(bf16)
    hi = hi.astype(bf16)
    half = lo.shape[1]
    g = (jnp.dot(lo, wsg_ref[0:half, :], preferred_element_type=f32)
         + jnp.dot(hi, wsg_ref[half:, :], preferred_element_type=f32))
    u = (jnp.dot(lo, wsu_ref[0:half, :], preferred_element_type=f32)
         + jnp.dot(hi, wsu_ref[half:, :], preferred_element_type=f32))
    shared = jnp.dot((_silu(g) * u).astype(bf16), wsd_ref[...], preferred_element_type=f32)

    for s in (0, 1):
        @pl.when(has_next2 & (slot == s))
        def _(s=s):
            table_fetch(step + 2, s).wait()

        @pl.when(slot == s)
        def _(s=s):
            def drain(r, c):
                for k in range(top_k):
                    _row_copy(os_hbm, 0, buf.at[s, k], 0, sems.at[s]).wait()
                return c

            lax.fori_loop(0, tm, drain, 0)

    y_lo = shared[:, :half]
    y_hi = shared[:, half:]
    wt = wt_ref[...]
    for k in range(top_k):
        e_lo, e_hi = _unpack_pair(buf[slot, k])
        wk = wt[:, k:k + 1]
        y_lo = y_lo + wk * e_lo
        y_hi = y_hi + wk * e_hi
    ffn = jnp.concatenate([y_lo, y_hi], axis=1)
    y_ref[...] = _layer_norm(DEEPNORM_ALPHA * x1_ref[...] + (1.0 + g2_ref[...]) * ffn, lg_ref[...], lb_ref[...])


def _combine(dest2d, out_sorted, x1, hp, wts, mod, seq_len, ln_g, ln_b, ws_gate, ws_up, ws_down, tm):
    t, d = x1.shape
    half = d // 2
    sf = ws_gate.shape[1]
    (g2_spec,) = _mod_specs(mod, tm, seq_len, (5,), d)
    row = lambda m: (m, 0)
    const = lambda m: (0, 0)
    return pl.pallas_call(
        functools.partial(_combine_body, top_k=TOP_K, n_steps=t // tm),
        out_shape=jax.ShapeDtypeStruct((t, d), f32),
        grid=(t // tm,),
        in_specs=[pl.BlockSpec(memory_space=pl.ANY), pl.BlockSpec(memory_space=pl.ANY),
                  pl.BlockSpec((tm, d), row), pl.BlockSpec((tm, half), row), pl.BlockSpec((tm, LANES), row),
                  g2_spec, pl.BlockSpec((1, d), const), pl.BlockSpec((1, d), const),
                  pl.BlockSpec((d, sf), const), pl.BlockSpec((d, sf), const), pl.BlockSpec((sf, d), const)],
        out_specs=pl.BlockSpec((tm, d), row),
        scratch_shapes=[pltpu.SMEM((tm * TOP_K,), i32), pltpu.SMEM((tm * TOP_K,), i32),
                        pltpu.VMEM((2, TOP_K, tm, half), u32),
                        pltpu.SemaphoreType.DMA(()), pltpu.SemaphoreType.DMA((2,))],
        compiler_params=_cparams(("arbitrary",)),
        name="combine",
    )(dest2d, out_sorted, x1, hp, wts, mod, ln_g.reshape(1, d), ln_b.reshape(1, d), ws_gate, ws_up, ws_down)


def _pick(total, pref, unit=SUBLANES):
    if total <= pref:
        return total
    c = pref - pref % unit
    while total % c:
        c -= unit
    return c


def kernel(x_prompt, x_sample, state_dn_S, state_dn_conv, state_sc_conv, c_prompt, c_sample, w_ada, b_ada, w_in, w_dn_conv, dn_a_log, dn_dt_bias, dn_norm_w, w_sc_conv, w_dn_out, w_sc_out, w_o, ln1_g, ln1_b, w_router, router_bias, w_e_gate, w_e_up, w_e_down, w_s_gate, w_s_up, w_s_down, ln2_g, ln2_b):
    bp, lp, d = x_prompt.shape
    bs, ls, _ = x_sample.shape
    n_heads = dn_a_log.shape[0]
    dk = dn_norm_w.shape[0]
    assert dk == LANES
    dnw = n_heads * dk
    scw = w_sc_conv.shape[1]
    n_e = w_router.shape[1]
    tp, ts = bp * lp, bs * ls

    w_in_t = w_in.T
    n_a = 4 * dnw
    o_sc = n_a + 2 * n_heads
    o_gd = o_sc + 3 * scw
    tn_a = _pick(n_a, 1024, LANES)
    tn_c = _pick(math.gcd(2 * d, 3 * scw), 1024, LANES)
    rows_a = [i * tn_a for i in range(n_a // tn_a)]
    rows_c = [o_gd + i * tn_c for i in range(2 * d // tn_c)] + [o_sc + i * tn_c for i in range(3 * scw // tn_c)]
    c_gd = 0
    c_gs = d
    c_sc = 2 * d
    assert 2 * n_heads <= LANES and n_a + LANES <= w_in.shape[1]
    lane_pad = lambda v: jnp.concatenate([jnp.zeros((n_heads,), f32), v.astype(f32),
                                          jnp.zeros((LANES - 2 * n_heads,), f32)])[None, :]
    par = jnp.concatenate([lane_pad(dn_a_log), lane_pad(dn_dt_bias), jnp.zeros((SUBLANES - 2, LANES), f32)],
                          axis=0)[None]

    w_dn_b = w_dn_out.astype(bf16)
    w_sc_b = w_sc_out.astype(bf16)
    w_o_b = w_o.astype(bf16)
    ws_gate_b = w_s_gate.astype(bf16)
    ws_up_b = w_s_up.astype(bf16)
    ws_down_b = w_s_down.astype(bf16)

    n_c = bp + bs
    m_pad = -(-n_c // SUBLANES) * SUBLANES
    c_all = jnp.concatenate([c_prompt, c_sample, jnp.zeros((m_pad - n_c, d), f32)], axis=0)
    mod = _ada(c_all, w_ada, b_ada)
    mod_p = mod[:bp].reshape(bp, 1, 6 * d)
    mod_s = jnp.repeat(mod[bp:n_c], ls, axis=0)

    def group(x3, mod_g, s0, dn_hist, sc_hist, chunk, lt, cnt_in):
        bn, seq, _ = x3.shape
        t = bn * seq
        x = x3.reshape(t, d)
        tile = (lambda pref: _pick(seq, pref)) if mod_g.ndim == 3 else (lambda pref: _pick(t, pref))
        proj_a = _modmm(x, mod_g, seq, w_in_t, rows_a, tn_a, bf16, tile(1024))
        proj = _modmm(x, mod_g, seq, w_in_t, rows_c, tn_c, bf16, tile(1024))
        ba = _modmm(x, mod_g, seq, w_in_t, [n_a], LANES, f32, tile(1024))
        hist8 = jnp.concatenate([jnp.zeros((bn, SUBLANES - dn_hist.shape[1], 3 * dnw), f32), dn_hist], axis=1)
        o_n, s_new, nbq, nbk, nbv = _deltanet(
            proj_a.reshape(bn, seq, -1), ba.reshape(bn, seq, -1), par, w_dn_conv, dn_norm_w, hist8, s0,
            n_heads=n_heads, heads_per_step=n_heads, chunk=chunk, lt=lt,
            seqs_per_step=4 if (seq == chunk and bn % 4 == 0) else 1)
        keep = w_dn_conv.shape[0] - 1
        dn_buf = jnp.concatenate([nbq, nbk, nbv], axis=2)[:, SUBLANES - keep:, :]

        zrow = jnp.zeros((bn, SUBLANES - 2, scw), f32)
        hist_a = jnp.concatenate([sc_hist[:, 1:2], jnp.zeros((bn, 1, scw), f32), zrow], axis=1).reshape(bn * SUBLANES, scw)
        hist_b = jnp.concatenate([sc_hist[:, 0:1], sc_hist[:, 1:2], zrow], axis=1).reshape(bn * SUBLANES, scw)
        rows_sc = seq if seq > SUBLANES else _pick(t, 256)
        scin, tail = _short_conv(proj, c_sc, scw, w_sc_conv, hist_a, hist_b, seq, rows_sc, min(scw, 256))
        sc_keep = w_sc_conv.shape[0] - 1
        sc_buf = tail.reshape(bn, SUBLANES, scw)[:, SUBLANES - sc_keep:, :]

        merged = _merge(o_n.reshape(t, dnw), scin, proj, c_gd, c_gs, w_dn_b, w_sc_b, tile(256))
        x1, hp, idx, wts, pos, cnt = _mix(merged, x, mod_g, seq, w_o_b, ln1_g, ln1_b, w_router, router_bias,
                                          cnt_in, tile(256))
        return dict(x1=x1, hp=hp, idx=idx[:, :TOP_K], wts=wts, pos=pos[:, :TOP_K], cnt=cnt,
                    s=s_new, dn_buf=dn_buf, sc_buf=sc_buf, mod=mod_g, seq=seq, tm_d=tile(256))

    zeros_s = jnp.zeros((bp,) + state_dn_S.shape[1:], f32)
    zeros_dn = jnp.zeros((bp,) + state_dn_conv.shape[1:], f32)
    zeros_sc = jnp.zeros((bp,) + state_sc_conv.shape[1:], f32)
    chunk_p = math.gcd(DN_CHUNK, lp)
    chunk_s = math.gcd(DN_CHUNK, ls)
    gp = group(x_prompt, mod_p, zeros_s, zeros_dn, zeros_sc, chunk_p, _pick(lp, 512), jnp.zeros((1, n_e), f32))
    gs = group(x_sample, mod_s, state_dn_S, state_dn_conv, state_sc_conv, chunk_s, ls, gp["cnt"])

    tm_e = 256
    counts = gs["cnt"][0].astype(i32)
    tiles_per_e = (counts + tm_e - 1) // tm_e
    tile_end = jnp.cumsum(tiles_per_e)
    start_row = (tile_end - tiles_per_e) * tm_e
    n_rows = (tp + ts) * TOP_K
    max_tiles = n_rows // tm_e + n_e
    tile_ids = jnp.arange(max_tiles, dtype=i32)
    tile_expert = jnp.minimum(jnp.sum((tile_end[None, :] <= tile_ids[:, None]).astype(i32), axis=1), n_e - 1)
    n_tiles = tile_end[-1:].astype(i32)
    e_ids = jnp.arange(n_e, dtype=i32)
    nonempty = tiles_per_e > 0
    ordinal = jnp.cumsum(nonempty.astype(i32)) - 1
    later = (e_ids[None, :] > e_ids[:, None]) & nonempty[None, :]
    next_e = jnp.min(jnp.where(later, e_ids[None, :], n_e), axis=1)
    next_e = jnp.where(next_e == n_e, -1, next_e).astype(i32)
    onehot_te = tile_expert[:, None] == e_ids[None, :]
    lookup = lambda table: jnp.sum(jnp.where(onehot_te, table[None, :], 0), axis=1).astype(i32)
    first = (tile_ids == lookup(tile_end - tiles_per_e)).astype(i32)
    slot = lookup(ordinal) % 2
    nxt = lookup(next_e)

    for g in (gp, gs):
        t = g["x1"].shape[0]
        idx_flat = g["idx"].reshape(-1)
        start_flat = jnp.sum(jnp.where(idx_flat[:, None] == e_ids[None, :], start_row[None, :], 0), axis=1)
        g["dest"] = (start_flat + g["pos"].reshape(-1)).astype(i32).reshape(t // g["tm_d"], g["tm_d"] * TOP_K)
    assert gp["tm_d"] == gs["tm_d"]
    zero_flag = ((tile_ids >= n_tiles[0]) | (tile_ids == lookup(tile_end) - 1)).astype(i32)
    xs = _dispatch(zero_flag, jnp.concatenate([gp["dest"], gs["dest"]], axis=0), gp["hp"], gs["hp"],
                   max_tiles * tm_e, tm_e, gp["tm_d"])
    out_sorted = _experts(tile_expert, n_tiles, first, slot, nxt, xs, w_e_gate, w_e_up, w_e_down, tm_e)
    ys = [_combine(g["dest"], out_sorted, g["x1"], g["hp"], g["wts"], g["mod"], g["seq"], ln2_g, ln2_b,
                   ws_gate_b, ws_up_b, ws_down_b, g["tm_d"]) for g in (gp, gs)]

    y_p = ys[0].reshape(bp, lp, d)
    y_s = ys[1].reshape(bs, ls, d)
    return (y_p, y_s, gp["s"], gp["dn_buf"], gp["sc_buf"], gs["s"], gs["dn_buf"], gs["sc_buf"])
```

```python
import functools
import math

import jax
import jax.numpy as jnp
from jax import lax
from jax.experimental import pallas as pl
from jax.experimental.pallas import tpu as pltpu

f32 = jnp.float32
bf16 = jnp.bfloat16
i32 = jnp.int32
u32 = jnp.uint32

TOP_K = 8
ROUTED_SCALE = 2.5
DN_CHUNK = 64
DEPTH = 1
DEEPNORM_ALPHA = (2 * DEPTH) ** 0.25
LN_EPS = 1e-5
RMS_EPS = 1e-6

LANES = 128
SUBLANES = 8
VMEM_LIMIT = 56 * 1024 * 1024


def _cparams(sem, vmem=VMEM_LIMIT):
    return pltpu.CompilerParams(dimension_semantics=sem, vmem_limit_bytes=vmem)


def _silu(x):
    return x * jax.nn.sigmoid(x)


def _bdot(a, b):
    return jnp.dot(a.astype(bf16), b.astype(bf16), preferred_element_type=f32)


def _layer_norm(x, g, b):
    mu = jnp.mean(x, axis=-1, keepdims=True)
    xc = x - mu
    var = jnp.mean(xc * xc, axis=-1, keepdims=True)
    return xc * lax.rsqrt(var + LN_EPS) * g + b


def _pack_pair(a, b):
    ab = pltpu.bitcast(a.astype(bf16).astype(f32), u32)
    bb = pltpu.bitcast(b.astype(bf16).astype(f32), u32)
    return (ab >> 16) | (bb & jnp.uint32(0xFFFF0000))


def _unpack_pair(p):
    lo = pltpu.bitcast(p << 16, f32)
    hi = pltpu.bitcast(p & jnp.uint32(0xFFFF0000), f32)
    return lo, hi


def _ada_body(c_ref, w_ref, b_ref, o_ref):
    s = _silu(c_ref[...])
    o_ref[...] = _bdot(s, w_ref[...]) + b_ref[...]


def _ada(c_all, w_ada, b_ada):
    m, d = c_all.shape
    n = w_ada.shape[1]
    tn = _pick(n, 1024, LANES)
    return pl.pallas_call(
        _ada_body,
        out_shape=jax.ShapeDtypeStruct((m, n), f32),
        grid=(n // tn,),
        in_specs=[pl.BlockSpec((m, d), lambda j: (0, 0)),
                  pl.BlockSpec((d, tn), lambda j: (0, j)),
                  pl.BlockSpec((1, tn), lambda j: (0, j))],
        out_specs=pl.BlockSpec((m, tn), lambda j: (0, j)),
        compiler_params=_cparams(("arbitrary",)),
        name="ada",
    )(c_all, w_ada, b_ada.reshape(1, n))


def _modmm_body(off_ref, x_ref, sc_ref, sh_ref, w_ref, o_ref, h_scr):
    del off_ref

    @pl.when(pl.program_id(1) == 0)
    def _():
        h_scr[...] = (x_ref[...] * (1.0 + sc_ref[...]) + sh_ref[...]).astype(bf16)

    o_ref[...] = lax.dot_general(h_scr[...], w_ref[...].astype(bf16), (((1,), (1,)), ((), ())),
                                 preferred_element_type=f32).astype(o_ref.dtype)


def _mod_specs(mod, tm, seq_len, cols, d):
    if mod.ndim == 3:
        tiles_per_seq = seq_len // tm
        return [pl.BlockSpec((None, 1, d), lambda m, *_, c=c: (m // tiles_per_seq, 0, c)) for c in cols]
    return [pl.BlockSpec((tm, d), lambda m, *_, c=c: (m, c)) for c in cols]


def _modmm(x, mod, seq_len, w_t, row_starts, tn, out_dtype, tm):
    t, d = x.shape
    assert all(r % SUBLANES == 0 for r in row_starts)
    n_blocks = len(row_starts)
    sc_spec, sh_spec = _mod_specs(mod, tm, seq_len, (1, 0), d)
    return pl.pallas_call(
        _modmm_body,
        out_shape=jax.ShapeDtypeStruct((t, n_blocks * tn), out_dtype),
        grid_spec=pltpu.PrefetchScalarGridSpec(
            num_scalar_prefetch=1,
            grid=(t // tm, n_blocks),
            in_specs=[pl.BlockSpec((tm, d), lambda m, j, off: (m, 0)), sc_spec, sh_spec,
                      pl.BlockSpec((pl.Element(tn), pl.Element(d)), lambda m, j, off: (off[j] * SUBLANES, 0))],
            out_specs=pl.BlockSpec((tm, tn), lambda m, j, off: (m, j)),
            scratch_shapes=[pltpu.VMEM((tm, d), bf16)]),
        compiler_params=_cparams(("arbitrary", "arbitrary")),
        name="modmm",
    )(jnp.asarray([r // SUBLANES for r in row_starts], i32), x, mod, mod, w_t)


def _cumsum_rows(x, n):
    row = lax.broadcasted_iota(i32, x.shape, 0)
    s = 1
    while s < n:
        x = x + jnp.where(row >= s, pltpu.roll(x, s, axis=0), 0.0)
        s *= 2
    return x


def _conv_silu(x, prev, w):
    c = x.shape[0]
    taps = w.shape[0]
    xc = jnp.concatenate([prev, x], axis=0)
    y = w[taps - 1:taps, :] * x
    for i in range(taps - 1):
        back = taps - 1 - i
        y = y + w[i:i + 1, :] * pltpu.roll(xc, back, axis=0)[SUBLANES:SUBLANES + c, :]
    return _silu(y)


def _dn_body(q_ref, k_ref, v_ref, z_ref, ba_ref, par_ref, cwq_ref, cwk_ref, cwv_ref, nw_ref,
             hq_ref, hk_ref, hv_ref, s0_ref,
             o_ref, so_ref, nbq_ref, nbk_ref, nbv_ref,
             s_scr, pq, pk, pv, *, chunk, heads, n_chunks, n_seq):
    t = pl.program_id(2)
    dk = LANES

    @pl.when(t == 0)
    def _init():
        s_scr[...] = s0_ref[...]
        pq[...] = hq_ref[...]
        pk[...] = hk_ref[...]
        pv[...] = hv_ref[...]

    neg_a = -jnp.exp(par_ref[0:1, :])
    dt_b = par_ref[1:2, :]
    nw = nw_ref[...]
    ii = lax.broadcasted_iota(i32, (chunk, chunk), 0)
    jj = lax.broadcasted_iota(i32, (chunk, chunk), 1)
    causal = ii >= jj
    strict = ii > jj
    eye = (ii == jj).astype(f32)
    zpad = jnp.zeros((LANES - chunk, LANES), f32)
    sls = [slice(h * dk, (h + 1) * dk) for h in range(heads)]
    pairs = [(s, h) for s in range(n_seq) for h in range(heads)]
    ps = range(len(pairs))

    def do_chunk(ci):
        r0 = pl.multiple_of(ci * chunk, chunk)
        rows = pl.ds(r0, chunk)
        beta_all, gc_all, gc_t, eg_all, qc, kc, vc = [], [], [], [], [], [], []
        for s in range(n_seq):
            ba = ba_ref[s, rows, :]
            beta_all.append(jax.nn.sigmoid(ba))
            xs = ba + dt_b
            softplus = jnp.maximum(xs, 0.0) + jnp.log1p(jnp.exp(-jnp.abs(xs)))
            g = _cumsum_rows(neg_a * softplus, chunk)
            gc_all.append(g)
            gc_t.append(jnp.concatenate([g, zpad], axis=0).T)
            eg_all.append(jnp.exp(g))
            xq = q_ref[s, rows, :].astype(f32)
            xk = k_ref[s, rows, :].astype(f32)
            xv = v_ref[s, rows, :].astype(f32)
            qc.append(_conv_silu(xq, pq[s], cwq_ref[...]))
            kc.append(_conv_silu(xk, pk[s], cwk_ref[...]))
            vc.append(_conv_silu(xv, pv[s], cwv_ref[...]))
            pq[s] = xq[chunk - SUBLANES:, :]
            pk[s] = xk[chunk - SUBLANES:, :]
            pv[s] = xv[chunk - SUBLANES:, :]

        qn = [qc[s][:, sls[h]] for s, h in pairs]
        kn = [kc[s][:, sls[h]] for s, h in pairs]
        vn = [vc[s][:, sls[h]] for s, h in pairs]
        qn = [q * (lax.rsqrt(jnp.sum(q * q, axis=-1, keepdims=True) + RMS_EPS) * (dk ** -0.5)) for q in qn]
        kn = [k * lax.rsqrt(jnp.sum(k * k, axis=-1, keepdims=True) + RMS_EPS) for k in kn]
        beta = [beta_all[s][:, h:h + 1] for s, h in pairs]
        gc = [gc_all[s][:, heads + h:heads + h + 1] for s, h in pairs]
        eg = [eg_all[s][:, heads + h:heads + h + 1] for s, h in pairs]
        qk_kk = [lax.dot_general(jnp.concatenate([qn[i], kn[i]], axis=0).astype(bf16), kn[i].astype(bf16),
                                 (((1,), (1,)), ((), ())), preferred_element_type=f32) for i in ps]
        decay = []
        for i, (s, h) in enumerate(pairs):
            diff = gc[i] - gc_t[s][heads + h:heads + h + 1, 0:chunk]
            decay.append(jnp.where(causal, jnp.exp(jnp.where(causal, diff, 0.0)), 0.0))
        qk = [qk_kk[i][:chunk, :] * decay[i] for i in ps]
        p = [-jnp.where(strict, qk_kk[i][chunk:, :] * beta[i] * decay[i], 0.0) for i in ps]
        tinv = [eye + p[i] for i in ps]
        n = 1
        while 2 * n < chunk:
            p = [_bdot(p[i], p[i]) for i in ps]
            tinv = [tinv[i] + _bdot(p[i], tinv[i]) for i in ps]
            n *= 2
        sol = [_bdot(tinv[i], jnp.concatenate([vn[i] * beta[i], kn[i] * (beta[i] * eg[i])], axis=1))
               for i in ps]
        s_old = [s_scr[s, h] for s, h in pairs]
        ks_qs = [_bdot(jnp.concatenate([sol[i][:, dk:], qn[i] * eg[i]], axis=0), s_old[i]) for i in ps]
        u = [sol[i][:, :dk] - ks_qs[i][:chunk, :] for i in ps]
        o = [ks_qs[i][chunk:, :] + _bdot(qk[i], u[i]) for i in ps]
        for i, (s, h) in enumerate(pairs):
            gc_last = gc[i][chunk - 1:chunk, :]
            kd = kn[i] * jnp.exp(gc_last - gc[i])
            s_scr[s, h] = s_old[i] * jnp.exp(gc_last) + lax.dot_general(
                kd.astype(bf16), u[i].astype(bf16), (((0,), (0,)), ((), ())), preferred_element_type=f32)
        for i, (s, h) in enumerate(pairs):
            zh = z_ref[s, rows, sls[h]].astype(f32)
            on = o[i] * lax.rsqrt(jnp.mean(o[i] * o[i], axis=-1, keepdims=True) + RMS_EPS) * nw * _silu(zh)
            o_ref[s, rows, sls[h]] = on.astype(o_ref.dtype)

    if n_chunks == 1:
        do_chunk(0)
    else:
        def loop_body(ci, carry):
            do_chunk(ci)
            return carry
        lax.fori_loop(0, n_chunks, loop_body, 0)

    @pl.when(t == pl.num_programs(2) - 1)
    def _fin():
        so_ref[...] = s_scr[...]
        nbq_ref[...] = pq[...]
        nbk_ref[...] = pk[...]
        nbv_ref[...] = pv[...]


def _deltanet(proj3, ba3, par, w_conv, norm_w, hist8, s0, *, n_heads, heads_per_step, seqs_per_step, chunk, lt):
    bn, seq, _ = proj3.shape
    dk = LANES
    dnw = n_heads * dk
    hg = heads_per_step
    nb = seqs_per_step
    gw = hg * dk
    n_g = n_heads // hg
    n_t = seq // lt
    kern = functools.partial(_dn_body, chunk=chunk, heads=hg, n_chunks=lt // chunk, n_seq=nb)

    def col(off):
        return pl.BlockSpec((nb, lt, gw), lambda b, g, t, off=off: (b, t, off * n_g + g))

    def cw(off):
        return pl.BlockSpec((w_conv.shape[0], gw), lambda b, g, t, off=off: (0, off * n_g + g))

    def hist(off):
        return pl.BlockSpec((nb, SUBLANES, gw), lambda b, g, t, off=off: (b, 0, off * n_g + g))

    nb_spec = pl.BlockSpec((nb, SUBLANES, gw), lambda b, g, t: (b, 0, g))
    nb_shape = jax.ShapeDtypeStruct((bn, SUBLANES, dnw), f32)
    return pl.pallas_call(
        kern,
        out_shape=(jax.ShapeDtypeStruct((bn, seq, dnw), bf16),
                   jax.ShapeDtypeStruct(s0.shape, f32), nb_shape, nb_shape, nb_shape),
        grid=(bn // nb, n_g, n_t),
        in_specs=[col(0), col(1), col(2), col(3),
                  pl.BlockSpec((nb, lt, LANES), lambda b, g, t: (b, t, g)),
                  pl.BlockSpec((None, SUBLANES, LANES), lambda b, g, t: (g, 0, 0)),
                  cw(0), cw(1), cw(2),
                  pl.BlockSpec((1, dk), lambda b, g, t: (0, 0)),
                  hist(0), hist(1), hist(2),
                  pl.BlockSpec((nb, hg, dk, dk), lambda b, g, t: (b, g, 0, 0))],
        out_specs=(pl.BlockSpec((nb, lt, gw), lambda b, g, t: (b, t, g)),
                   pl.BlockSpec((nb, hg, dk, dk), lambda b, g, t: (b, g, 0, 0)),
                   nb_spec, nb_spec, nb_spec),
        scratch_shapes=[pltpu.VMEM((nb, hg, dk, dk), f32),
                        pltpu.VMEM((nb, SUBLANES, gw), f32), pltpu.VMEM((nb, SUBLANES, gw), f32),
                        pltpu.VMEM((nb, SUBLANES, gw), f32)],
        compiler_params=_cparams(("arbitrary", "arbitrary", "arbitrary")),
        name="deltanet",
    )(proj3, proj3, proj3, proj3, ba3, par, w_conv, w_conv, w_conv, norm_w.reshape(1, dk),
      hist8, hist8, hist8, s0)


def _sc_body(b_ref, c_ref, x_ref, w_ref, ha_ref, hb_ref, o_ref, tail_ref, *, seq_len):
    u = c_ref[...].astype(f32) * x_ref[...].astype(f32)
    rows = u.shape[0]
    tin = lax.broadcasted_iota(i32, u.shape, 0) % seq_len
    w = w_ref[...]
    conv = (w[2:3, :] * u
            + w[1:2, :] * jnp.where(tin >= 1, pltpu.roll(u, 1, axis=0), 0.0)
            + w[0:1, :] * jnp.where(tin >= 2, pltpu.roll(u, 2, axis=0), 0.0))
    corr = w[1:2, :] * ha_ref[...] + w[0:1, :] * hb_ref[...]
    bv = b_ref[...].astype(f32)
    if corr.shape[0] == rows:
        o_ref[...] = (bv * (conv + corr)).astype(o_ref.dtype)
        tail_ref[...] = u
    else:
        o_ref[...] = (bv * conv).astype(o_ref.dtype)
        o_ref[0:SUBLANES, :] = (bv[0:SUBLANES, :] * (conv[0:SUBLANES, :] + corr)).astype(o_ref.dtype)
        tail_ref[...] = u[rows - SUBLANES:, :]


def _short_conv(proj, col0, scw, w_sc, hist_a, hist_b, seq_len, rows_per_step, tw):
    t = proj.shape[0]
    n_seq = t // seq_len
    hr = SUBLANES if rows_per_step == seq_len else rows_per_step
    n_w = scw // tw
    cb = col0 // tw

    def pin(off):
        return pl.BlockSpec((rows_per_step, tw), lambda r, j, off=off: (r, cb + off * n_w + j))

    h_spec = pl.BlockSpec((hr, tw), lambda r, j: (r, j))
    return pl.pallas_call(
        functools.partial(_sc_body, seq_len=seq_len),
        out_shape=(jax.ShapeDtypeStruct((t, scw), bf16),
                   jax.ShapeDtypeStruct((n_seq * SUBLANES, scw), f32)),
        grid=(t // rows_per_step, n_w),
        in_specs=[pin(0), pin(1), pin(2), pl.BlockSpec((w_sc.shape[0], tw), lambda r, j: (0, j)),
                  h_spec, h_spec],
        out_specs=(pl.BlockSpec((rows_per_step, tw), lambda r, j: (r, j)),
                   pl.BlockSpec((hr, tw), lambda r, j: (r, j))),
        compiler_params=_cparams(("arbitrary", "arbitrary")),
        name="short_conv",
    )(proj, proj, proj, w_sc, hist_a, hist_b)


def _merge_body(o_ref, s_ref, gd_ref, gs_ref, wdn_ref, wsc_ref, m_ref):
    ydn = jnp.dot(o_ref[...], wdn_ref[...], preferred_element_type=f32)
    ysc = jnp.dot(s_ref[...], wsc_ref[...], preferred_element_type=f32)
    m = jax.nn.sigmoid(gd_ref[...].astype(f32)) * ydn + jax.nn.sigmoid(gs_ref[...].astype(f32)) * ysc
    m_ref[...] = m.astype(m_ref.dtype)


def _merge(o_n, scin, proj, gd_col, gs_col, w_dn, w_sc, tm):
    t, dnw = o_n.shape
    scw = scin.shape[1]
    d = w_dn.shape[1]
    return pl.pallas_call(
        _merge_body,
        out_shape=jax.ShapeDtypeStruct((t, d), bf16),
        grid=(t // tm,),
        in_specs=[pl.BlockSpec((tm, dnw), lambda m: (m, 0)),
                  pl.BlockSpec((tm, scw), lambda m: (m, 0)),
                  pl.BlockSpec((tm, d), lambda m: (m, gd_col // d)),
                  pl.BlockSpec((tm, d), lambda m: (m, gs_col // d)),
                  pl.BlockSpec((dnw, d), lambda m: (0, 0)),
                  pl.BlockSpec((scw, d), lambda m: (0, 0))],
        out_specs=pl.BlockSpec((tm, d), lambda m: (m, 0)),
        compiler_params=_cparams(("arbitrary",)),
        name="merge",
    )(o_n, scin, proj, proj, w_dn, w_sc)


def _split_dot3(a, b):
    a_hi = a.astype(bf16)
    a_lo = (a - a_hi.astype(f32)).astype(bf16)
    b_hi = b.astype(bf16)
    b_lo = (b - b_hi.astype(f32)).astype(bf16)
    return (jnp.dot(a_hi, b_hi, preferred_element_type=f32) + jnp.dot(a_hi, b_lo, preferred_element_type=f32)
            + jnp.dot(a_lo, b_hi, preferred_element_type=f32))


def _mix_body(m_ref, x_ref, g1_ref, sh2_ref, sc2_ref, wo_ref, lg_ref, lb_ref, wr_ref, rb_ref, cnt_in_ref,
              x1_ref, hp_ref, idx_ref, wt_ref, pos_ref, cnt_ref, cnt_scr, *, top_k):
    step = pl.program_id(0)

    @pl.when(step == 0)
    def _():
        cnt_scr[...] = cnt_in_ref[...]

    mix = jnp.dot(m_ref[...], wo_ref[...], preferred_element_type=f32)
    x1 = _layer_norm(DEEPNORM_ALPHA * x_ref[...] + (1.0 + g1_ref[...]) * mix, lg_ref[...], lb_ref[...])
    x1_ref[...] = x1
    h2 = x1 * (1.0 + sc2_ref[...]) + sh2_ref[...]
    half = h2.shape[1] // 2
    hp_ref[...] = _pack_pair(h2[:, :half], h2[:, half:])

    tm = h2.shape[0]
    n_e = wr_ref.shape[1]
    scores = jax.nn.sigmoid(_split_dot3(h2, wr_ref[...]))
    work = scores + rb_ref[...]
    lane_e = lax.broadcasted_iota(i32, (tm, n_e), 1).astype(f32)
    lane_o = lax.broadcasted_iota(i32, (tm, LANES), 1)
    sel = jnp.zeros((tm, n_e), f32)
    picks = []
    wsum = jnp.zeros((tm, 1), f32)
    for _ in range(top_k):
        mx = jnp.max(work, axis=-1, keepdims=True)
        idx = jnp.min(jnp.where(work == mx, lane_e, float(n_e)), axis=-1, keepdims=True)
        hit = lane_e == idx
        wk = jnp.sum(jnp.where(hit, scores, 0.0), axis=-1, keepdims=True)
        work = jnp.where(hit, -jnp.inf, work)
        sel = jnp.where(hit, 1.0, sel)
        wsum = wsum + wk
        picks.append((idx, hit, wk))

    ri = lax.broadcasted_iota(i32, (tm, tm), 0)
    ci = lax.broadcasted_iota(i32, (tm, tm), 1)
    tri = jnp.where(ri > ci, 1.0, 0.0).astype(bf16)
    before = jnp.dot(tri, sel.astype(bf16), preferred_element_type=f32) + cnt_scr[...]
    cnt_scr[...] = cnt_scr[...] + jnp.sum(sel, axis=0, keepdims=True)
    cnt_ref[...] = cnt_scr[...]

    idx_o = jnp.zeros((tm, LANES), i32)
    wt_o = jnp.zeros((tm, LANES), f32)
    pos_o = jnp.zeros((tm, LANES), i32)
    for k, (idx, hit, wk) in enumerate(picks):
        pk = jnp.sum(jnp.where(hit, before, 0.0), axis=-1, keepdims=True).astype(i32)
        idx_o = jnp.where(lane_o == k, idx.astype(i32), idx_o)
        wt_o = jnp.where(lane_o == k, wk / wsum * ROUTED_SCALE, wt_o)
        pos_o = jnp.where(lane_o == k, pk, pos_o)
    idx_ref[...] = idx_o
    wt_ref[...] = wt_o
    pos_ref[...] = pos_o


def _mix(merged, x, mod, seq_len, w_o, ln_g, ln_b, w_router, router_bias, cnt_in, tm):
    t, d = x.shape
    n_e = w_router.shape[1]
    g1_spec, sh2_spec, sc2_spec = _mod_specs(mod, tm, seq_len, (2, 3, 4), d)
    row = lambda m: (m, 0)
    const = lambda m: (0, 0)
    lane_out = jax.ShapeDtypeStruct((t, LANES), i32)
    return pl.pallas_call(
        functools.partial(_mix_body, top_k=TOP_K),
        out_shape=(jax.ShapeDtypeStruct((t, d), f32), jax.ShapeDtypeStruct((t, d // 2), u32),
                   lane_out, jax.ShapeDtypeStruct((t, LANES), f32), lane_out,
                   jax.ShapeDtypeStruct((1, n_e), f32)),
        grid=(t // tm,),
        in_specs=[pl.BlockSpec((tm, d), row), pl.BlockSpec((tm, d), row), g1_spec, sh2_spec, sc2_spec,
                  pl.BlockSpec((d, d), const), pl.BlockSpec((1, d), const), pl.BlockSpec((1, d), const),
                  pl.BlockSpec((d, n_e), const), pl.BlockSpec((1, n_e), const), pl.BlockSpec((1, n_e), const)],
        out_specs=(pl.BlockSpec((tm, d), row), pl.BlockSpec((tm, d // 2), row),
                   pl.BlockSpec((tm, LANES), row), pl.BlockSpec((tm, LANES), row), pl.BlockSpec((tm, LANES), row),
                   pl.BlockSpec((1, n_e), const)),
        scratch_shapes=[pltpu.VMEM((1, n_e), f32)],
        compiler_params=_cparams(("arbitrary",)),
        name="mix",
    )(merged, x, mod, mod, mod, w_o, ln_g.reshape(1, d), ln_b.reshape(1, d), w_router,
      router_bias.reshape(1, n_e), cnt_in)


def _row_copy(src, src_row, dst, dst_row, sem):
    return pltpu.make_async_copy(src.at[pl.ds(src_row, 1)], dst.at[pl.ds(dst_row, 1)], sem)


def _dispatch_body(flag_ref, dest_hbm, ha_ref, hb_ref, xs_out, dsm0, dsm1, zeros, sem_idx, sem_zero, sem,
                   *, top_k, steps_a, tile_rows, n_tiles):
    step = pl.program_id(0)
    tm = ha_ref.shape[0]

    @pl.when(step == 0)
    def _():
        zeros[...] = jnp.zeros(zeros.shape, zeros.dtype)

        def tile_copy(i):
            start = pl.multiple_of(i * tile_rows, tile_rows)
            return pltpu.make_async_copy(zeros, xs_out.at[pl.ds(start, tile_rows)], sem_zero)

        def issue_zero(i, c):
            @pl.when(flag_ref[i] == 1)
            def _():
                tile_copy(i).start()
            return c

        def drain_zero(i, c):
            @pl.when(flag_ref[i] == 1)
            def _():
                tile_copy(i).wait()
            return c

        lax.fori_loop(0, n_tiles, issue_zero, 0)
        lax.fori_loop(0, n_tiles, drain_zero, 0)

    dsms = (dsm0, dsm1)
    slot = step % 2
    has_next = step + 1 < pl.num_programs(0)

    def table_fetch(tile, s):
        return pltpu.make_async_copy(dest_hbm.at[tile], dsms[s], sem_idx)

    @pl.when(step == 0)
    def _():
        first = table_fetch(0, 0)
        first.start()
        first.wait()

    for s in (0, 1):
        @pl.when(has_next & (slot == 1 - s))
        def _(s=s):
            table_fetch(step + 1, s).start()

    def scatter(h_ref, dsm):
        def issue(r8, c):
            base = pl.multiple_of(r8 * SUBLANES, SUBLANES)
            for j in range(SUBLANES):
                for k in range(top_k):
                    _row_copy(h_ref, base + j, xs_out, dsm[(base + j) * top_k + k], sem).start(priority=k % 2)
            return c

        lax.fori_loop(0, tm // SUBLANES, issue, 0)

        def drain(r, c):
            for k in range(top_k):
                _row_copy(h_ref, 0, xs_out, 0, sem).wait()
            return c

        lax.fori_loop(0, tm, drain, 0)

    for s in (0, 1):
        @pl.when((step < steps_a) & (slot == s))
        def _(s=s):
            scatter(ha_ref, dsms[s])

        @pl.when((step >= steps_a) & (slot == s))
        def _(s=s):
            scatter(hb_ref, dsms[s])

    for s in (0, 1):
        @pl.when(has_next & (slot == 1 - s))
        def _(s=s):
            table_fetch(step + 1, s).wait()


def _dispatch(flags, dest2d, hp_a, hp_b, rows, tile_rows, tm):
    half = hp_a.shape[1]
    steps_a = hp_a.shape[0] // tm
    steps_b = hp_b.shape[0] // tm
    n_tiles = rows // tile_rows
    kern = functools.partial(_dispatch_body, top_k=TOP_K, steps_a=steps_a, tile_rows=tile_rows, n_tiles=n_tiles)
    return pl.pallas_call(
        kern,
        out_shape=jax.ShapeDtypeStruct((rows, half), u32),
        grid_spec=pltpu.PrefetchScalarGridSpec(
            num_scalar_prefetch=1,
            grid=(steps_a + steps_b,),
            in_specs=[pl.BlockSpec(memory_space=pl.ANY),
                      pl.BlockSpec((tm, half), lambda m, fl: (jnp.minimum(m, steps_a - 1), 0)),
                      pl.BlockSpec((tm, half), lambda m, fl: (jnp.maximum(m - steps_a, 0), 0))],
            out_specs=pl.BlockSpec(memory_space=pl.ANY),
            scratch_shapes=[pltpu.SMEM((tm * TOP_K,), i32), pltpu.SMEM((tm * TOP_K,), i32),
                            pltpu.VMEM((tile_rows, half), u32),
                            pltpu.SemaphoreType.DMA(()), pltpu.SemaphoreType.DMA(()), pltpu.SemaphoreType.DMA(())]),
        compiler_params=_cparams(("arbitrary",)),
        name="dispatch",
    )(flags, dest2d, hp_a, hp_b)


def _expert_body(te_ref, nt_ref, first_ref, slot_ref, nxt_ref, x_ref, wg_hbm, wu_hbm, wd_hbm, o_ref,
                 wg_f, wu_f, wd_f, sems):
    step = pl.program_id(0)

    def weight_copies(e, s):
        return (pltpu.make_async_copy(wg_hbm.at[e], wg_f.at[s], sems.at[s, 0]),
                pltpu.make_async_copy(wu_hbm.at[e], wu_f.at[s], sems.at[s, 1]),
                pltpu.make_async_copy(wd_hbm.at[e], wd_f.at[s], sems.at[s, 2]))

    @pl.when(step < nt_ref[0])
    def _():
        e = te_ref[step]
        s = slot_ref[step]

        @pl.when(first_ref[step] == 1)
        def _():
            @pl.when(step == 0)
            def _():
                for c in weight_copies(e, s):
                    c.start()

            for c in weight_copies(e, s):
                c.wait()
            nxt = nxt_ref[step]

            @pl.when(nxt >= 0)
            def _():
                for c in weight_copies(nxt, 1 - s):
                    c.start()

        lo, hi = _unpack_pair(x_ref[...])
        lo = lo.astype(bf16)
        hi = hi.astype(bf16)
        half = lo.shape[1]
        g = (jnp.dot(lo, wg_f[s, 0:half, :].astype(bf16), preferred_element_type=f32)
             + jnp.dot(hi, wg_f[s, half:, :].astype(bf16), preferred_element_type=f32))
        u = (jnp.dot(lo, wu_f[s, 0:half, :].astype(bf16), preferred_element_type=f32)
             + jnp.dot(hi, wu_f[s, half:, :].astype(bf16), preferred_element_type=f32))
        hid = (_silu(g) * u).astype(bf16)
        out = jnp.dot(hid, wd_f[s].astype(bf16), preferred_element_type=f32)
        o_ref[...] = _pack_pair(out[:, :half], out[:, half:])


def _experts(tile_expert, n_tiles, first, slot, nxt, xs, w_gate, w_up, w_down, tm):
    rows, half = xs.shape
    n_e, d, f = w_gate.shape
    max_tiles = rows // tm

    def tile(i, te, nt, *_):
        return (jnp.minimum(i, nt[0] - 1), 0)

    hbm = pl.BlockSpec(memory_space=pl.ANY)
    return pl.pallas_call(
        _expert_body,
        out_shape=jax.ShapeDtypeStruct((rows, half), u32),
        grid_spec=pltpu.PrefetchScalarGridSpec(
            num_scalar_prefetch=5,
            grid=(max_tiles,),
            in_specs=[pl.BlockSpec((tm, half), tile), hbm, hbm, hbm],
            out_specs=pl.BlockSpec((tm, half), tile),
            scratch_shapes=[pltpu.VMEM((2, d, f), f32), pltpu.VMEM((2, d, f), f32), pltpu.VMEM((2, f, d), f32),
                            pltpu.SemaphoreType.DMA((2, 3))]),
        input_output_aliases={5: 0},
        compiler_params=_cparams(("arbitrary",)),
        name="experts",
    )(tile_expert, n_tiles, first, slot, nxt, xs, w_gate, w_up, w_down)


def _combine_body(dest_hbm, os_hbm, x1_ref, hp_ref, wt_ref, g2_ref, lg_ref, lb_ref, wsg_ref, wsu_ref, wsd_ref,
                  y_ref, dsm0, dsm1, buf, sem_idx, sems, *, top_k, n_steps):
    step = pl.program_id(0)
    tm = x1_ref.shape[0]
    slot = step % 2

    dsms = (dsm0, dsm1)

    def table_fetch(tile, s):
        return pltpu.make_async_copy(dest_hbm.at[tile], dsms[s], sem_idx)

    def gather(s):
        def issue(r8, c):
            base = pl.multiple_of(r8 * SUBLANES, SUBLANES)
            for j in range(SUBLANES):
                for k in range(top_k):
                    src_row = dsms[s][(base + j) * top_k + k]
                    _row_copy(os_hbm, src_row, buf.at[s, k], base + j, sems.at[s]).start(priority=k % 2)
            return c

        lax.fori_loop(0, tm // SUBLANES, issue, 0)

    @pl.when(step == 0)
    def _():
        first = table_fetch(0, 0)
        first.start()
        first.wait()
        gather(0)
        if n_steps > 1:
            second = table_fetch(1, 1)
            second.start()
            second.wait()

    has_next = step + 1 < n_steps
    has_next2 = step + 2 < n_steps
    for s in (0, 1):
        @pl.when(has_next & (slot == 1 - s))
        def _(s=s):
            gather(s)

        @pl.when(has_next2 & (slot == s))
        def _(s=s):
            table_fetch(step + 2, s).start()

    lo, hi = _unpack_pair(hp_ref[...])
    lo = lo.astype(bf16)
    hi = hi.astype(bf16)
    half = lo.shape[1]
    g = (jnp.dot(lo, wsg_ref[0:half, :], preferred_element_type=f32)
         + jnp.dot(hi, wsg_ref[half:, :], preferred_element_type=f32))
    u = (jnp.dot(lo, wsu_ref[0:half, :], preferred_element_type=f32)
         + jnp.dot(hi, wsu_ref[half:, :], preferred_element_type=f32))
    shared = jnp.dot((_silu(g) * u).astype(bf16), wsd_ref[...], preferred_element_type=f32)

    for s in (0, 1):
        @pl.when(has_next2 & (slot == s))
        def _(s=s):
            table_fetch(step + 2, s).wait()

        @pl.when(slot == s)
        def _(s=s):
            def drain(r, c):
                for k in range(top_k):
                    _row_copy(os_hbm, 0, buf.at[s, k], 0, sems.at[s]).wait()
                return c

            lax.fori_loop(0, tm, drain, 0)

    y_lo = shared[:, :half]
    y_hi = shared[:, half:]
    wt = wt_ref[...]
    for k in range(top_k):
        e_lo, e_hi = _unpack_pair(buf[slot, k])
        wk = wt[:, k:k + 1]
        y_lo = y_lo + wk * e_lo
        y_hi = y_hi + wk * e_hi
    ffn = jnp.concatenate([y_lo, y_hi], axis=1)
    y_ref[...] = _layer_norm(DEEPNORM_ALPHA * x1_ref[...] + (1.0 + g2_ref[...]) * ffn, lg_ref[...], lb_ref[...])


def _combine(dest2d, out_sorted, x1, hp, wts, mod, seq_len, ln_g, ln_b, ws_gate, ws_up, ws_down, tm):
    t, d = x1.shape
    half = d // 2
    sf = ws_gate.shape[1]
    (g2_spec,) = _mod_specs(mod, tm, seq_len, (5,), d)
    row = lambda m: (m, 0)
    const = lambda m: (0, 0)
    return pl.pallas_call(
        functools.partial(_combine_body, top_k=TOP_K, n_steps=t // tm),
        out_shape=jax.ShapeDtypeStruct((t, d), f32),
        grid=(t // tm,),
        in_specs=[pl.BlockSpec(memory_space=pl.ANY), pl.BlockSpec(memory_space=pl.ANY),
                  pl.BlockSpec((tm, d), row), pl.BlockSpec((tm, half), row), pl.BlockSpec((tm, LANES), row),
                  g2_spec, pl.BlockSpec((1, d), const), pl.BlockSpec((1, d), const),
                  pl.BlockSpec((d, sf), const), pl.BlockSpec((d, sf), const), pl.BlockSpec((sf, d), const)],
        out_specs=pl.BlockSpec((tm, d), row),
        scratch_shapes=[pltpu.SMEM((tm * TOP_K,), i32), pltpu.SMEM((tm * TOP_K,), i32),
                        pltpu.VMEM((2, TOP_K, tm, half), u32),
                        pltpu.SemaphoreType.DMA(()), pltpu.SemaphoreType.DMA((2,))],
        compiler_params=_cparams(("arbitrary",)),
        name="combine",
    )(dest2d, out_sorted, x1, hp, wts, mod, ln_g.reshape(1, d), ln_b.reshape(1, d), ws_gate, ws_up, ws_down)


def _pick(total, pref, unit=SUBLANES):
    if total <= pref:
        return total
    c = pref - pref % unit
    while total % c:
        c -= unit
    return c


def kernel(x_prompt, x_sample, state_dn_S, state_dn_conv, state_sc_conv, c_prompt, c_sample, w_ada, b_ada, w_in, w_dn_conv, dn_a_log, dn_dt_bias, dn_norm_w, w_sc_conv, w_dn_out, w_sc_out, w_o, ln1_g, ln1_b, w_router, router_bias, w_e_gate, w_e_up, w_e_down, w_s_gate, w_s_up, w_s_down, ln2_g, ln2_b):
    bp, lp, d = x_prompt.shape
    bs, ls, _ = x_sample.shape
    n_heads = dn_a_log.shape[0]
    dk = dn_norm_w.shape[0]
    assert dk == LANES
    dnw = n_heads * dk
    scw = w_sc_conv.shape[1]
    n_e = w_router.shape[1]
    tp, ts = bp * lp, bs * ls

    w_in_t = w_in.T
    n_a = 4 * dnw
    o_sc = n_a + 2 * n_heads
    o_gd = o_sc + 3 * scw
    tn_a = _pick(n_a, 1024, LANES)
    tn_c = _pick(math.gcd(2 * d, 3 * scw), 1024, LANES)
    rows_a = [i * tn_a for i in range(n_a // tn_a)]
    rows_c = [o_gd + i * tn_c for i in range(2 * d // tn_c)] + [o_sc + i * tn_c for i in range(3 * scw // tn_c)]
    c_gd = 0
    c_gs = d
    c_sc = 2 * d
    assert 2 * n_heads <= LANES and n_a + LANES <= w_in.shape[1]
    lane_pad = lambda v: jnp.concatenate([jnp.zeros((n_heads,), f32), v.astype(f32),
                                          jnp.zeros((LANES - 2 * n_heads,), f32)])[None, :]
    par = jnp.concatenate([lane_pad(dn_a_log), lane_pad(dn_dt_bias), jnp.zeros((SUBLANES - 2, LANES), f32)],
                          axis=0)[None]

    w_dn_b = w_dn_out.astype(bf16)
    w_sc_b = w_sc_out.astype(bf16)
    w_o_b = w_o.astype(bf16)
    ws_gate_b = w_s_gate.astype(bf16)
    ws_up_b = w_s_up.astype(bf16)
    ws_down_b = w_s_down.astype(bf16)

    n_c = bp + bs
    m_pad = -(-n_c // SUBLANES) * SUBLANES
    c_all = jnp.concatenate([c_prompt, c_sample, jnp.zeros((m_pad - n_c, d), f32)], axis=0)
    mod = _ada(c_all, w_ada, b_ada)
    mod_p = mod[:bp].reshape(bp, 1, 6 * d)
    mod_s = jnp.repeat(mod[bp:n_c], ls, axis=0)

    def group(x3, mod_g, s0, dn_hist, sc_hist, chunk, lt, cnt_in):
        bn, seq, _ = x3.shape
        t = bn * seq
        x = x3.reshape(t, d)
        tile = (lambda pref: _pick(seq, pref)) if mod_g.ndim == 3 else (lambda pref: _pick(t, pref))
        proj_a = _modmm(x, mod_g, seq, w_in_t, rows_a, tn_a, bf16, tile(1024))
        proj = _modmm(x, mod_g, seq, w_in_t, rows_c, tn_c, bf16, tile(1024))
        ba = _modmm(x, mod_g, seq, w_in_t, [n_a], LANES, f32, tile(1024))
        hist8 = jnp.concatenate([jnp.zeros((bn, SUBLANES - dn_hist.shape[1], 3 * dnw), f32), dn_hist], axis=1)
        o_n, s_new, nbq, nbk, nbv = _deltanet(
            proj_a.reshape(bn, seq, -1), ba.reshape(bn, seq, -1), par, w_dn_conv, dn_norm_w, hist8, s0,
            n_heads=n_heads, heads_per_step=n_heads, chunk=chunk, lt=lt,
            seqs_per_step=4 if (seq == chunk and bn % 4 == 0) else 1)
        keep = w_dn_conv.shape[0] - 1
        dn_buf = jnp.concatenate([nbq, nbk, nbv], axis=2)[:, SUBLANES - keep:, :]

        zrow = jnp.zeros((bn, SUBLANES - 2, scw), f32)
        hist_a = jnp.concatenate([sc_hist[:, 1:2], jnp.zeros((bn, 1, scw), f32), zrow], axis=1).reshape(bn * SUBLANES, scw)
        hist_b = jnp.concatenate([sc_hist[:, 0:1], sc_hist[:, 1:2], zrow], axis=1).reshape(bn * SUBLANES, scw)
        rows_sc = seq if seq > SUBLANES else _pick(t, 256)
        scin, tail = _short_conv(proj, c_sc, scw, w_sc_conv, hist_a, hist_b, seq, rows_sc, min(scw, 256))
        sc_keep = w_sc_conv.shape[0] - 1
        sc_buf = tail.reshape(bn, SUBLANES, scw)[:, SUBLANES - sc_keep:, :]

        merged = _merge(o_n.reshape(t, dnw), scin, proj, c_gd, c_gs, w_dn_b, w_sc_b, tile(256))
        x1, hp, idx, wts, pos, cnt = _mix(merged, x, mod_g, seq, w_o_b, ln1_g, ln1_b, w_router, router_bias,
                                          cnt_in, tile(256))
        return dict(x1=x1, hp=hp, idx=idx[:, :TOP_K], wts=wts, pos=pos[:, :TOP_K], cnt=cnt,
                    s=s_new, dn_buf=dn_buf, sc_buf=sc_buf, mod=mod_g, seq=seq, tm_d=tile(256))

    zeros_s = jnp.zeros((bp,) + state_dn_S.shape[1:], f32)
    zeros_dn = jnp.zeros((bp,) + state_dn_conv.shape[1:], f32)
    zeros_sc = jnp.zeros((bp,) + state_sc_conv.shape[1:], f32)
    chunk_p = math.gcd(DN_CHUNK, lp)
    chunk_s = math.gcd(DN_CHUNK, ls)
    gp = group(x_prompt, mod_p, zeros_s, zeros_dn, zeros_sc, chunk_p, _pick(lp, 512), jnp.zeros((1, n_e), f32))
    gs = group(x_sample, mod_s, state_dn_S, state_dn_conv, state_sc_conv, chunk_s, ls, gp["cnt"])

    tm_e = 256
    counts = gs["cnt"][0].astype(i32)
    tiles_per_e = (counts + tm_e - 1) // tm_e
    tile_end = jnp.cumsum(tiles_per_e)
    start_row = (tile_end - tiles_per_e) * tm_e
    n_rows = (tp + ts) * TOP_K
    max_tiles = n_rows // tm_e + n_e
    tile_ids = jnp.arange(max_tiles, dtype=i32)
    tile_expert = jnp.minimum(jnp.sum((tile_end[None, :] <= tile_ids[:, None]).astype(i32), axis=1), n_e - 1)
    n_tiles = tile_end[-1:].astype(i32)
    e_ids = jnp.arange(n_e, dtype=i32)
    nonempty = tiles_per_e > 0
    ordinal = jnp.cumsum(nonempty.astype(i32)) - 1
    later = (e_ids[None, :] > e_ids[:, None]) & nonempty[None, :]
    next_e = jnp.min(jnp.where(later, e_ids[None, :], n_e), axis=1)
    next_e = jnp.where(next_e == n_e, -1, next_e).astype(i32)
    onehot_te = tile_expert[:, None] == e_ids[None, :]
    lookup = lambda table: jnp.sum(jnp.where(onehot_te, table[None, :], 0), axis=1).astype(i32)
    first = (tile_ids == lookup(tile_end - tiles_per_e)).astype(i32)
    slot = lookup(ordinal) % 2
    nxt = lookup(next_e)

    for g in (gp, gs):
        t = g["x1"].shape[0]
        idx_flat = g["idx"].reshape(-1)
        start_flat = jnp.sum(jnp.where(idx_flat[:, None] == e_ids[None, :], start_row[None, :], 0), axis=1)
        g["dest"] = (start_flat + g["pos"].reshape(-1)).astype(i32).reshape(t // g["tm_d"], g["tm_d"] * TOP_K)
    assert gp["tm_d"] == gs["tm_d"]
    zero_flag = ((tile_ids >= n_tiles[0]) | (tile_ids == lookup(tile_end) - 1)).astype(i32)
    xs = _dispatch(zero_flag, jnp.concatenate([gp["dest"], gs["dest"]], axis=0), gp["hp"], gs["hp"],
                   max_tiles * tm_e, tm_e, gp["tm_d"])
    out_sorted = _experts(tile_expert, n_tiles, first, slot, nxt, xs, w_e_gate, w_e_up, w_e_down, tm_e)
    ys = [_combine(g["dest"], out_sorted, g["x1"], g["hp"], g["wts"], g["mod"], g["seq"], ln2_g, ln2_b,
                   ws_gate_b, ws_up_b, ws_down_b, g["tm_d"]) for g in (gp, gs)]

    y_p = ys[0].reshape(bp, lp, d)
    y_s = ys[1].reshape(bs, ls, d)
    return (y_p, y_s, gp["s"], gp["dn_buf"], gp["sc_buf"], gs["s"], gs["dn_buf"], gs["sc_buf"])
```

```python
import functools
import math

import jax
import jax.numpy as jnp
from jax import lax
from jax.experimental import pallas as pl
from jax.experimental.pallas import tpu as pltpu

f32 = jnp.float32
bf16 = jnp.bfloat16
i32 = jnp.int32
u32 = jnp.uint32

TOP_K = 8
ROUTED_SCALE = 2.5
DN_CHUNK = 64
DEPTH = 1
DEEPNORM_ALPHA = (2 * DEPTH) ** 0.25
LN_EPS = 1e-5
RMS_EPS = 1e-6

LANES = 128
SUBLANES = 8
VMEM_LIMIT = 56 * 1024 * 1024


def _cparams(sem, vmem=VMEM_LIMIT):
    return pltpu.CompilerParams(dimension_semantics=sem, vmem_limit_bytes=vmem)


def _silu(x):
    return x * jax.nn.sigmoid(x)


def _bdot(a, b):
    return jnp.dot(a.astype(bf16), b.astype(bf16), preferred_element_type=f32)


def _layer_norm(x, g, b):
    mu = jnp.mean(x, axis=-1, keepdims=True)
    xc = x - mu
    var = jnp.mean(xc * xc, axis=-1, keepdims=True)
    return xc * lax.rsqrt(var + LN_EPS) * g + b


def _pack_pair(a, b):
    ab = pltpu.bitcast(a.astype(bf16).astype(f32), u32)
    bb = pltpu.bitcast(b.astype(bf16).astype(f32), u32)
    return (ab >> 16) | (bb & jnp.uint32(0xFFFF0000))


def _unpack_pair(p):
    lo = pltpu.bitcast(p << 16, f32)
    hi = pltpu.bitcast(p & jnp.uint32(0xFFFF0000), f32)
    return lo, hi


def _ada_body(c_ref, w_ref, b_ref, o_ref):
    s = _silu(c_ref[...])
    o_ref[...] = _bdot(s, w_ref[...]) + b_ref[...]


def _ada(c_all, w_ada, b_ada):
    m, d = c_all.shape
    n = w_ada.shape[1]
    tn = _pick(n, 1024, LANES)
    return pl.pallas_call(
        _ada_body,
        out_shape=jax.ShapeDtypeStruct((m, n), f32),
        grid=(n // tn,),
        in_specs=[pl.BlockSpec((m, d), lambda j: (0, 0)),
                  pl.BlockSpec((d, tn), lambda j: (0, j)),
                  pl.BlockSpec((1, tn), lambda j: (0, j))],
        out_specs=pl.BlockSpec((m, tn), lambda j: (0, j)),
        compiler_params=_cparams(("arbitrary",)),
        name="ada",
    )(c_all, w_ada, b_ada.reshape(1, n))


def _modmm_body(off_ref, x_ref, sc_ref, sh_ref, w_ref, o_ref, h_scr):
    del off_ref

    @pl.when(pl.program_id(1) == 0)
    def _():
        h_scr[...] = (x_ref[...] * (1.0 + sc_ref[...]) + sh_ref[...]).astype(bf16)

    o_ref[...] = lax.dot_general(h_scr[...], w_ref[...].astype(bf16), (((1,), (1,)), ((), ())),
                                 preferred_element_type=f32).astype(o_ref.dtype)


def _mod_specs(mod, tm, seq_len, cols, d):
    if mod.ndim == 3:
        tiles_per_seq = seq_len // tm
        return [pl.BlockSpec((None, 1, d), lambda m, *_, c=c: (m // tiles_per_seq, 0, c)) for c in cols]
    return [pl.BlockSpec((tm, d), lambda m, *_, c=c: (m, c)) for c in cols]


def _modmm(x, mod, seq_len, w_t, row_starts, tn, out_dtype, tm):
    t, d = x.shape
    assert all(r % SUBLANES == 0 for r in row_starts)
    n_blocks = len(row_starts)
    sc_spec, sh_spec = _mod_specs(mod, tm, seq_len, (1, 0), d)
    return pl.pallas_call(
        _modmm_body,
        out_shape=jax.ShapeDtypeStruct((t, n_blocks * tn), out_dtype),
        grid_spec=pltpu.PrefetchScalarGridSpec(
            num_scalar_prefetch=1,
            grid=(t // tm, n_blocks),
            in_specs=[pl.BlockSpec((tm, d), lambda m, j, off: (m, 0)), sc_spec, sh_spec,
                      pl.BlockSpec((pl.Element(tn), pl.Element(d)), lambda m, j, off: (off[j] * SUBLANES, 0))],
            out_specs=pl.BlockSpec((tm, tn), lambda m, j, off: (m, j)),
            scratch_shapes=[pltpu.VMEM((tm, d), bf16)]),
        compiler_params=_cparams(("arbitrary", "arbitrary")),
        name="modmm",
    )(jnp.asarray([r // SUBLANES for r in row_starts], i32), x, mod, mod, w_t)


def _cumsum_rows(x, n):
    row = lax.broadcasted_iota(i32, x.shape, 0)
    s = 1
    while s < n:
        x = x + jnp.where(row >= s, pltpu.roll(x, s, axis=0), 0.0)
        s *= 2
    return x


def _conv_silu(x, prev, w):
    c = x.shape[0]
    taps = w.shape[0]
    xc = jnp.concatenate([prev, x], axis=0)
    y = w[taps - 1:taps, :] * x
    for i in range(taps - 1):
        back = taps - 1 - i
        y = y + w[i:i + 1, :] * pltpu.roll(xc, back, axis=0)[SUBLANES:SUBLANES + c, :]
    return _silu(y)


def _dn_body(q_ref, k_ref, v_ref, z_ref, ba_ref, par_ref, cwq_ref, cwk_ref, cwv_ref, nw_ref,
             hq_ref, hk_ref, hv_ref, s0_ref,
             o_ref, so_ref, nbq_ref, nbk_ref, nbv_ref,
             s_scr, pq, pk, pv, *, chunk, heads, n_chunks, n_seq):
    t = pl.program_id(2)
    dk = LANES

    @pl.when(t == 0)
    def _init():
        s_scr[...] = s0_ref[...]
        pq[...] = hq_ref[...]
        pk[...] = hk_ref[...]
        pv[...] = hv_ref[...]

    neg_a = -jnp.exp(par_ref[0:1, :])
    dt_b = par_ref[1:2, :]
    nw = nw_ref[...]
    ii = lax.broadcasted_iota(i32, (chunk, chunk), 0)
    jj = lax.broadcasted_iota(i32, (chunk, chunk), 1)
    causal = ii >= jj
    strict = ii > jj
    eye = (ii == jj).astype(f32)
    zpad = jnp.zeros((LANES - chunk, LANES), f32)
    sls = [slice(h * dk, (h + 1) * dk) for h in range(heads)]
    pairs = [(s, h) for s in range(n_seq) for h in range(heads)]
    ps = range(len(pairs))

    def do_chunk(ci):
        r0 = pl.multiple_of(ci * chunk, chunk)
        rows = pl.ds(r0, chunk)
        beta_all, gc_all, gc_t, eg_all, qc, kc, vc = [], [], [], [], [], [], []
        for s in range(n_seq):
            ba = ba_ref[s, rows, :]
            beta_all.append(jax.nn.sigmoid(ba))
            xs = ba + dt_b
            softplus = jnp.maximum(xs, 0.0) + jnp.log1p(jnp.exp(-jnp.abs(xs)))
            g = _cumsum_rows(neg_a * softplus, chunk)
            gc_all.append(g)
            gc_t.append(jnp.concatenate([g, zpad], axis=0).T)
            eg_all.append(jnp.exp(g))
            xq = q_ref[s, rows, :].astype(f32)
            xk = k_ref[s, rows, :].astype(f32)
            xv = v_ref[s, rows, :].astype(f32)
            qc.append(_conv_silu(xq, pq[s], cwq_ref[...]))
            kc.append(_conv_silu(xk, pk[s], cwk_ref[...]))
            vc.append(_conv_silu(xv, pv[s], cwv_ref[...]))
            pq[s] = xq[chunk - SUBLANES:, :]
            pk[s] = xk[chunk - SUBLANES:, :]
            pv[s] = xv[chunk - SUBLANES:, :]

        qn = [qc[s][:, sls[h]] for s, h in pairs]
        kn = [kc[s][:, sls[h]] for s, h in pairs]
        vn = [vc[s][:, sls[h]] for s, h in pairs]
        qn = [q * (lax.rsqrt(jnp.sum(q * q, axis=-1, keepdims=True) + RMS_EPS) * (dk ** -0.5)) for q in qn]
        kn = [k * lax.rsqrt(jnp.sum(k * k, axis=-1, keepdims=True) + RMS_EPS) for k in kn]
        beta = [beta_all[s][:, h:h + 1] for s, h in pairs]
        gc = [gc_all[s][:, heads + h:heads + h + 1] for s, h in pairs]
        eg = [eg_all[s][:, heads + h:heads + h + 1] for s, h in pairs]
        qk_kk = [lax.dot_general(jnp.concatenate([qn[i], kn[i]], axis=0).astype(bf16), kn[i].astype(bf16),
                                 (((1,), (1,)), ((), ())), preferred_element_type=f32) for i in ps]
        decay = []
        for i, (s, h) in enumerate(pairs):
            diff = gc[i] - gc_t[s][heads + h:heads + h + 1, 0:chunk]
            decay.append(jnp.where(causal, jnp.exp(jnp.where(causal, diff, 0.0)), 0.0))
        qk = [qk_kk[i][:chunk, :] * decay[i] for i in ps]
        p = [-jnp.where(strict, qk_kk[i][chunk:, :] * beta[i] * decay[i], 0.0) for i in ps]
        tinv = [eye + p[i] for i in ps]
        n = 1
        while 2 * n < chunk:
            p = [_bdot(p[i], p[i]) for i in ps]
            tinv = [tinv[i] + _bdot(p[i], tinv[i]) for i in ps]
            n *= 2
        sol = [_bdot(tinv[i], jnp.concatenate([vn[i] * beta[i], kn[i] * (beta[i] * eg[i])], axis=1))
               for i in ps]
        s_old = [s_scr[s, h] for s, h in pairs]
        ks_qs = [_bdot(jnp.concatenate([sol[i][:, dk:], qn[i] * eg[i]], axis=0), s_old[i]) for i in ps]
        u = [sol[i][:, :dk] - ks_qs[i][:chunk, :] for i in ps]
        o = [ks_qs[i][chunk:, :] + _bdot(qk[i], u[i]) for i in ps]
        for i, (s, h) in enumerate(pairs):
            gc_last = gc[i][chunk - 1:chunk, :]
            kd = kn[i] * jnp.exp(gc_last - gc[i])
            s_scr[s, h] = s_old[i] * jnp.exp(gc_last) + lax.dot_general(
                kd.astype(bf16), u[i].astype(bf16), (((0,), (0,)), ((), ())), preferred_element_type=f32)
        for i, (s, h) in enumerate(pairs):
            zh = z_ref[s, rows, sls[h]].astype(f32)
            on = o[i] * lax.rsqrt(jnp.mean(o[i] * o[i], axis=-1, keepdims=True) + RMS_EPS) * nw * _silu(zh)
            o_ref[s, rows, sls[h]] = on.astype(o_ref.dtype)

    if n_chunks == 1:
        do_chunk(0)
    else:
        def loop_body(ci, carry):
            do_chunk(ci)
            return carry
        lax.fori_loop(0, n_chunks, loop_body, 0)

    @pl.when(t == pl.num_programs(2) - 1)
    def _fin():
        so_ref[...] = s_scr[...]
        nbq_ref[...] = pq[...]
        nbk_ref[...] = pk[...]
        nbv_ref[...] = pv[...]


def _deltanet(proj3, ba3, par, w_conv, norm_w, hist8, s0, *, n_heads, heads_per_step, seqs_per_step, chunk, lt):
    bn, seq, _ = proj3.shape
    dk = LANES
    dnw = n_heads * dk
    hg = heads_per_step
    nb = seqs_per_step
    gw = hg * dk
    n_g = n_heads // hg
    n_t = seq // lt
    kern = functools.partial(_dn_body, chunk=chunk, heads=hg, n_chunks=lt // chunk, n_seq=nb)

    def col(off):
        return pl.BlockSpec((nb, lt, gw), lambda b, g, t, off=off: (b, t, off * n_g + g))

    def cw(off):
        return pl.BlockSpec((w_conv.shape[0], gw), lambda b, g, t, off=off: (0, off * n_g + g))

    def hist(off):
        return pl.BlockSpec((nb, SUBLANES, gw), lambda b, g, t, off=off: (b, 0, off * n_g + g))

    nb_spec = pl.BlockSpec((nb, SUBLANES, gw), lambda b, g, t: (b, 0, g))
    nb_shape = jax.ShapeDtypeStruct((bn, SUBLANES, dnw), f32)
    return pl.pallas_call(
        kern,
        out_shape=(jax.ShapeDtypeStruct((bn, seq, dnw), bf16),
                   jax.ShapeDtypeStruct(s0.shape, f32), nb_shape, nb_shape, nb_shape),
        grid=(bn // nb, n_g, n_t),
        in_specs=[col(0), col(1), col(2), col(3),
                  pl.BlockSpec((nb, lt, LANES), lambda b, g, t: (b, t, g)),
                  pl.BlockSpec((None, SUBLANES, LANES), lambda b, g, t: (g, 0, 0)),
                  cw(0), cw(1), cw(2),
                  pl.BlockSpec((1, dk), lambda b, g, t: (0, 0)),
                  hist(0), hist(1), hist(2),
                  pl.BlockSpec((nb, hg, dk, dk), lambda b, g, t: (b, g, 0, 0))],
        out_specs=(pl.BlockSpec((nb, lt, gw), lambda b, g, t: (b, t, g)),
                   pl.BlockSpec((nb, hg, dk, dk), lambda b, g, t: (b, g, 0, 0)),
                   nb_spec, nb_spec, nb_spec),
        scratch_shapes=[pltpu.VMEM((nb, hg, dk, dk), f32),
                        pltpu.VMEM((nb, SUBLANES, gw), f32), pltpu.VMEM((nb, SUBLANES, gw), f32),
                        pltpu.VMEM((nb, SUBLANES, gw), f32)],
        compiler_params=_cparams(("arbitrary", "arbitrary", "arbitrary")),
        name="deltanet",
    )(proj3, proj3, proj3, proj3, ba3, par, w_conv, w_conv, w_conv, norm_w.reshape(1, dk),
      hist8, hist8, hist8, s0)


def _sc_body(b_ref, c_ref, x_ref, w_ref, ha_ref, hb_ref, o_ref, tail_ref, *, seq_len):
    u = c_ref[...].astype(f32) * x_ref[...].astype(f32)
    rows = u.shape[0]
    tin = lax.broadcasted_iota(i32, u.shape, 0) % seq_len
    w = w_ref[...]
    conv = (w[2:3, :] * u
            + w[1:2, :] * jnp.where(tin >= 1, pltpu.roll(u, 1, axis=0), 0.0)
            + w[0:1, :] * jnp.where(tin >= 2, pltpu.roll(u, 2, axis=0), 0.0))
    corr = w[1:2, :] * ha_ref[...] + w[0:1, :] * hb_ref[...]
    bv = b_ref[...].astype(f32)
    if corr.shape[0] == rows:
        o_ref[...] = (bv * (conv + corr)).astype(o_ref.dtype)
        tail_ref[...] = u
    else:
        o_ref[...] = (bv * conv).astype(o_ref.dtype)
        o_ref[0:SUBLANES, :] = (bv[0:SUBLANES, :] * (conv[0:SUBLANES, :] + corr)).astype(o_ref.dtype)
        tail_ref[...] = u[rows - SUBLANES:, :]


def _short_conv(proj, col0, scw, w_sc, hist_a, hist_b, seq_len, rows_per_step, tw):
    t = proj.shape[0]
    n_seq = t // seq_len
    hr = SUBLANES if rows_per_step == seq_len else rows_per_step
    n_w = scw // tw
    cb = col0 // tw

    def pin(off):
        return pl.BlockSpec((rows_per_step, tw), lambda r, j, off=off: (r, cb + off * n_w + j))

    h_spec = pl.BlockSpec((hr, tw), lambda r, j: (r, j))
    return pl.pallas_call(
        functools.partial(_sc_body, seq_len=seq_len),
        out_shape=(jax.ShapeDtypeStruct((t, scw), bf16),
                   jax.ShapeDtypeStruct((n_seq * SUBLANES, scw), f32)),
        grid=(t // rows_per_step, n_w),
        in_specs=[pin(0), pin(1), pin(2), pl.BlockSpec((w_sc.shape[0], tw), lambda r, j: (0, j)),
                  h_spec, h_spec],
        out_specs=(pl.BlockSpec((rows_per_step, tw), lambda r, j: (r, j)),
                   pl.BlockSpec((hr, tw), lambda r, j: (r, j))),
        compiler_params=_cparams(("arbitrary", "arbitrary")),
        name="short_conv",
    )(proj, proj, proj, w_sc, hist_a, hist_b)


def _merge_body(o_ref, s_ref, gd_ref, gs_ref, wdn_ref, wsc_ref, m_ref):
    ydn = jnp.dot(o_ref[...], wdn_ref[...], preferred_element_type=f32)
    ysc = jnp.dot(s_ref[...], wsc_ref[...], preferred_element_type=f32)
    m = jax.nn.sigmoid(gd_ref[...].astype(f32)) * ydn + jax.nn.sigmoid(gs_ref[...].astype(f32)) * ysc
    m_ref[...] = m.astype(m_ref.dtype)


def _merge(o_n, scin, proj, gd_col, gs_col, w_dn, w_sc, tm):
    t, dnw = o_n.shape
    scw = scin.shape[1]
    d = w_dn.shape[1]
    return pl.pallas_call(
        _merge_body,
        out_shape=jax.ShapeDtypeStruct((t, d), bf16),
        grid=(t // tm,),
        in_specs=[pl.BlockSpec((tm, dnw), lambda m: (m, 0)),
                  pl.BlockSpec((tm, scw), lambda m: (m, 0)),
                  pl.BlockSpec((tm, d), lambda m: (m, gd_col // d)),
                  pl.BlockSpec((tm, d), lambda m: (m, gs_col // d)),
                  pl.BlockSpec((dnw, d), lambda m: (0, 0)),
                  pl.BlockSpec((scw, d), lambda m: (0, 0))],
        out_specs=pl.BlockSpec((tm, d), lambda m: (m, 0)),
        compiler_params=_cparams(("arbitrary",)),
        name="merge",
    )(o_n, scin, proj, proj, w_dn, w_sc)


def _split_dot3(a, b):
    a_hi = a.astype(bf16)
    a_lo = (a - a_hi.astype(f32)).astype(bf16)
    b_hi = b.astype(bf16)
    b_lo = (b - b_hi.astype(f32)).astype(bf16)
    return (jnp.dot(a_hi, b_hi, preferred_element_type=f32) + jnp.dot(a_hi, b_lo, preferred_element_type=f32)
            + jnp.dot(a_lo, b_hi, preferred_element_type=f32))


def _mix_body(m_ref, x_ref, g1_ref, sh2_ref, sc2_ref, wo_ref, lg_ref, lb_ref, wr_ref, rb_ref, cnt_in_ref,
              x1_ref, hp_ref, idx_ref, wt_ref, pos_ref, cnt_ref, cnt_scr, *, top_k, parts):
    step = pl.program_id(0)

    @pl.when(step == 0)
    def _():
        cnt_scr[...] = cnt_in_ref[...]

    tm_all = m_ref.shape[0]
    pr = tm_all // parts
    n_e = wr_ref.shape[1]
    half = x_ref.shape[1] // 2
    blocks = [pl.ds(i * pr, pr) for i in range(parts)]
    per_token = g1_ref.shape[0] != 1
    mod = lambda ref, rows: ref[rows, :] if per_token else ref[...]
    mix = [jnp.dot(m_ref[rows, :], wo_ref[...], preferred_element_type=f32) for rows in blocks]
    h2 = []
    for rows, mx_ in zip(blocks, mix):
        x1 = _layer_norm(DEEPNORM_ALPHA * x_ref[rows, :] + (1.0 + mod(g1_ref, rows)) * mx_, lg_ref[...], lb_ref[...])
        x1_ref[rows, :] = x1
        h = x1 * (1.0 + mod(sc2_ref, rows)) + mod(sh2_ref, rows)
        hp_ref[rows, :] = _pack_pair(h[:, :half], h[:, half:])
        h2.append(h)
    scores = [jax.nn.sigmoid(_split_dot3(h, wr_ref[...])) for h in h2]
    work = [s + rb_ref[...] for s in scores]
    lane_e = lax.broadcasted_iota(i32, (pr, n_e), 1).astype(f32)
    lane_o = lax.broadcasted_iota(i32, (pr, LANES), 1)
    sel = [jnp.zeros((pr, n_e), f32) for _ in blocks]
    wsum = [jnp.zeros((pr, 1), f32) for _ in blocks]
    picks = [[] for _ in blocks]
    for _ in range(top_k):
        for b in range(parts):
            mx = jnp.max(work[b], axis=-1, keepdims=True)
            idx = jnp.min(jnp.where(work[b] == mx, lane_e, float(n_e)), axis=-1, keepdims=True)
            hit = lane_e == idx
            wk = jnp.sum(jnp.where(hit, scores[b], 0.0), axis=-1, keepdims=True)
            work[b] = jnp.where(hit, -jnp.inf, work[b])
            sel[b] = jnp.where(hit, 1.0, sel[b])
            wsum[b] = wsum[b] + wk
            picks[b].append((idx, hit, wk))

    ri = lax.broadcasted_iota(i32, (pr, pr), 0)
    ci = lax.broadcasted_iota(i32, (pr, pr), 1)
    tri = jnp.where(ri > ci, 1.0, 0.0).astype(bf16)
    running = cnt_scr[...]
    for b, rows in enumerate(blocks):
        before = jnp.dot(tri, sel[b].astype(bf16), preferred_element_type=f32) + running
        running = running + jnp.sum(sel[b], axis=0, keepdims=True)
        idx_o = jnp.zeros((pr, LANES), i32)
        wt_o = jnp.zeros((pr, LANES), f32)
        pos_o = jnp.zeros((pr, LANES), i32)
        for k, (idx, hit, wk) in enumerate(picks[b]):
            pk = jnp.sum(jnp.where(hit, before, 0.0), axis=-1, keepdims=True).astype(i32)
            idx_o = jnp.where(lane_o == k, idx.astype(i32), idx_o)
            wt_o = jnp.where(lane_o == k, wk / wsum[b] * ROUTED_SCALE, wt_o)
            pos_o = jnp.where(lane_o == k, pk, pos_o)
        idx_ref[rows, :] = idx_o
        wt_ref[rows, :] = wt_o
        pos_ref[rows, :] = pos_o
    cnt_scr[...] = running
    cnt_ref[...] = running


def _mix(merged, x, mod, seq_len, w_o, ln_g, ln_b, w_router, router_bias, cnt_in, tm):
    t, d = x.shape
    n_e = w_router.shape[1]
    g1_spec, sh2_spec, sc2_spec = _mod_specs(mod, tm, seq_len, (2, 3, 4), d)
    row = lambda m: (m, 0)
    const = lambda m: (0, 0)
    lane_out = jax.ShapeDtypeStruct((t, LANES), i32)
    return pl.pallas_call(
        functools.partial(_mix_body, top_k=TOP_K, parts=2 if tm % (2 * SUBLANES) == 0 else 1),
        out_shape=(jax.ShapeDtypeStruct((t, d), f32), jax.ShapeDtypeStruct((t, d // 2), u32),
                   lane_out, jax.ShapeDtypeStruct((t, LANES), f32), lane_out,
                   jax.ShapeDtypeStruct((1, n_e), f32)),
        grid=(t // tm,),
        in_specs=[pl.BlockSpec((tm, d), row), pl.BlockSpec((tm, d), row), g1_spec, sh2_spec, sc2_spec,
                  pl.BlockSpec((d, d), const, pipeline_mode=pl.Buffered(1)),
                  pl.BlockSpec((1, d), const), pl.BlockSpec((1, d), const),
                  pl.BlockSpec((d, n_e), const), pl.BlockSpec((1, n_e), const), pl.BlockSpec((1, n_e), const)],
        out_specs=(pl.BlockSpec((tm, d), row), pl.BlockSpec((tm, d // 2), row),
                   pl.BlockSpec((tm, LANES), row), pl.BlockSpec((tm, LANES), row), pl.BlockSpec((tm, LANES), row),
                   pl.BlockSpec((1, n_e), const)),
        scratch_shapes=[pltpu.VMEM((1, n_e), f32)],
        compiler_params=_cparams(("arbitrary",)),
        name="mix",
    )(merged, x, mod, mod, mod, w_o, ln_g.reshape(1, d), ln_b.reshape(1, d), w_router,
      router_bias.reshape(1, n_e), cnt_in)


def _row_copy(src, src_row, dst, dst_row, sem):
    return pltpu.make_async_copy(src.at[pl.ds(src_row, 1)], dst.at[pl.ds(dst_row, 1)], sem)


def _dispatch_body(flag_ref, dest_hbm, ha_ref, hb_ref, xs_out, dsm0, dsm1, zeros, sem_idx, sem_zero, sem,
                   *, top_k, steps_a, tile_rows, n_tiles):
    step = pl.program_id(0)
    tm = ha_ref.shape[0]

    @pl.when(step == 0)
    def _():
        zeros[...] = jnp.zeros(zeros.shape, zeros.dtype)

        def tile_copy(i):
            start = pl.multiple_of(i * tile_rows, tile_rows)
            return pltpu.make_async_copy(zeros, xs_out.at[pl.ds(start, tile_rows)], sem_zero)

        def issue_zero(i, c):
            @pl.when(flag_ref[i] == 1)
            def _():
                tile_copy(i).start()
            return c

        def drain_zero(i, c):
            @pl.when(flag_ref[i] == 1)
            def _():
                tile_copy(i).wait()
            return c

        lax.fori_loop(0, n_tiles, issue_zero, 0)
        lax.fori_loop(0, n_tiles, drain_zero, 0)

    dsms = (dsm0, dsm1)
    slot = step % 2
    has_next = step + 1 < pl.num_programs(0)

    def table_fetch(tile, s):
        return pltpu.make_async_copy(dest_hbm.at[tile], dsms[s], sem_idx)

    @pl.when(step == 0)
    def _():
        first = table_fetch(0, 0)
        first.start()
        first.wait()

    for s in (0, 1):
        @pl.when(has_next & (slot == 1 - s))
        def _(s=s):
            table_fetch(step + 1, s).start()

    def scatter(h_ref, dsm):
        def issue(r8, c):
            base = pl.multiple_of(r8 * SUBLANES, SUBLANES)
            for j in range(SUBLANES):
                for k in range(top_k):
                    _row_copy(h_ref, base + j, xs_out, dsm[(base + j) * top_k + k], sem).start(priority=k % 2)
            return c

        lax.fori_loop(0, tm // SUBLANES, issue, 0)

        def drain(r, c):
            for k in range(top_k):
                _row_copy(h_ref, 0, xs_out, 0, sem).wait()
            return c

        lax.fori_loop(0, tm, drain, 0)

    for s in (0, 1):
        @pl.when((step < steps_a) & (slot == s))
        def _(s=s):
            scatter(ha_ref, dsms[s])

        @pl.when((step >= steps_a) & (slot == s))
        def _(s=s):
            scatter(hb_ref, dsms[s])

    for s in (0, 1):
        @pl.when(has_next & (slot == 1 - s))
        def _(s=s):
            table_fetch(step + 1, s).wait()


def _dispatch(flags, dest2d, hp_a, hp_b, rows, tile_rows, tm):
    half = hp_a.shape[1]
    steps_a = hp_a.shape[0] // tm
    steps_b = hp_b.shape[0] // tm
    n_tiles = rows // tile_rows
    kern = functools.partial(_dispatch_body, top_k=TOP_K, steps_a=steps_a, tile_rows=tile_rows, n_tiles=n_tiles)
    return pl.pallas_call(
        kern,
        out_shape=jax.ShapeDtypeStruct((rows, half), u32),
        grid_spec=pltpu.PrefetchScalarGridSpec(
            num_scalar_prefetch=1,
            grid=(steps_a + steps_b,),
            in_specs=[pl.BlockSpec(memory_space=pl.ANY),
                      pl.BlockSpec((tm, half), lambda m, fl: (jnp.minimum(m, steps_a - 1), 0)),
                      pl.BlockSpec((tm, half), lambda m, fl: (jnp.maximum(m - steps_a, 0), 0))],
            out_specs=pl.BlockSpec(memory_space=pl.ANY),
            scratch_shapes=[pltpu.SMEM((tm * TOP_K,), i32), pltpu.SMEM((tm * TOP_K,), i32),
                            pltpu.VMEM((tile_rows, half), u32),
                            pltpu.SemaphoreType.DMA(()), pltpu.SemaphoreType.DMA(()), pltpu.SemaphoreType.DMA(())]),
        compiler_params=_cparams(("arbitrary",)),
        name="dispatch",
    )(flags, dest2d, hp_a, hp_b)


def _expert_body(te_ref, nt_ref, first_ref, slot_ref, nxt_ref, x_ref, wg_hbm, wu_hbm, wd_hbm, o_ref,
                 wg_f, wu_f, wd_f, wg_s, wu_s, wd_s, sems):
    step = pl.program_id(0)

    def weight_copies(e, s):
        return (pltpu.make_async_copy(wg_hbm.at[e], wg_f.at[s], sems.at[s, 0]),
                pltpu.make_async_copy(wu_hbm.at[e], wu_f.at[s], sems.at[s, 1]),
                pltpu.make_async_copy(wd_hbm.at[e], wd_f.at[s], sems.at[s, 2]))

    @pl.when(step < nt_ref[0])
    def _():
        e = te_ref[step]
        s = slot_ref[step]

        @pl.when(first_ref[step] == 1)
        def _():
            @pl.when(step == 0)
            def _():
                for c in weight_copies(e, s):
                    c.start()

            for c in weight_copies(e, s):
                c.wait()
            nxt = nxt_ref[step]

            @pl.when(nxt >= 0)
            def _():
                for c in weight_copies(nxt, 1 - s):
                    c.start()

            wg_s[...] = wg_f[s].astype(bf16)
            wu_s[...] = wu_f[s].astype(bf16)
            wd_s[...] = wd_f[s].astype(bf16)

        lo, hi = _unpack_pair(x_ref[...])
        lo = lo.astype(bf16)
        hi = hi.astype(bf16)
        half = lo.shape[1]
        g = (jnp.dot(lo, wg_s[0:half, :], preferred_element_type=f32)
             + jnp.dot(hi, wg_s[half:, :], preferred_element_type=f32))
        u = (jnp.dot(lo, wu_s[0:half, :], preferred_element_type=f32)
             + jnp.dot(hi, wu_s[half:, :], preferred_element_type=f32))
        hid = (_silu(g) * u).astype(bf16)
        out = jnp.dot(hid, wd_s[...], preferred_element_type=f32)
        o_ref[...] = _pack_pair(out[:, :half], out[:, half:])


def _experts(tile_expert, n_tiles, first, slot, nxt, xs, w_gate, w_up, w_down, tm):
    rows, half = xs.shape
    n_e, d, f = w_gate.shape
    max_tiles = rows // tm

    def tile(i, te, nt, *_):
        return (jnp.minimum(i, nt[0] - 1), 0)

    hbm = pl.BlockSpec(memory_space=pl.ANY)
    return pl.pallas_call(
        _expert_body,
        out_shape=jax.ShapeDtypeStruct((rows, half), u32),
        grid_spec=pltpu.PrefetchScalarGridSpec(
            num_scalar_prefetch=5,
            grid=(max_tiles,),
            in_specs=[pl.BlockSpec((tm, half), tile), hbm, hbm, hbm],
            out_specs=pl.BlockSpec((tm, half), tile),
            scratch_shapes=[pltpu.VMEM((2, d, f), f32), pltpu.VMEM((2, d, f), f32), pltpu.VMEM((2, f, d), f32),
                            pltpu.VMEM((d, f), bf16), pltpu.VMEM((d, f), bf16), pltpu.VMEM((f, d), bf16),
                            pltpu.SemaphoreType.DMA((2, 3))]),
        input_output_aliases={5: 0},
        compiler_params=_cparams(("arbitrary",)),
        name="experts",
    )(tile_expert, n_tiles, first, slot, nxt, xs, w_gate, w_up, w_down)


def _combine_body(dest_hbm, os_hbm, x1_ref, hp_ref, wt_ref, g2_ref, lg_ref, lb_ref, wsg_ref, wsu_ref, wsd_ref,
                  y_ref, dsm0, dsm1, buf, sem_idx, sems, *, top_k, n_steps):
    step = pl.program_id(0)
    tm = x1_ref.shape[0]
    slot = step % 2

    dsms = (dsm0, dsm1)

    def table_fetch(tile, s):
        return pltpu.make_async_copy(dest_hbm.at[tile], dsms[s], sem_idx)

    def gather(s):
        def issue(r8, c):
            base = pl.multiple_of(r8 * SUBLANES, SUBLANES)
            for j in range(SUBLANES):
                for k in range(top_k):
                    src_row = dsms[s][(base + j) * top_k + k]
                    _row_copy(os_hbm, src_row, buf.at[s, k], base + j, sems.at[s]).start(priority=k % 2)
            return c

        lax.fori_loop(0, tm // SUBLANES, issue, 0)

    @pl.when(step == 0)
    def _():
        first = table_fetch(0, 0)
        first.start()
        first.wait()
        gather(0)
        if n_steps > 1:
            second = table_fetch(1, 1)
            second.start()
            second.wait()

    has_next = step + 1 < n_steps
    has_next2 = step + 2 < n_steps
    for s in (0, 1):
        @pl.when(has_next & (slot == 1 - s))
        def _(s=s):
            gather(s)

        @pl.when(has_next2 & (slot == s))
        def _(s=s):
            table_fetch(step + 2, s).start()

    lo, hi = _unpack_pair(hp_ref[...])
    lo = lo.astype(bf16)
    hi = hi.astype(bf16)
    half = lo.shape[1]
    g = (jnp.dot(lo, wsg_ref[0:half, :], preferred_element_type=f32)
         + jnp.dot(hi, wsg_ref[half:, :], preferred_element_type=f32))
    u = (jnp.dot(lo, wsu_ref[0:half, :], preferred_element_type=f32)
         + jnp.dot(hi, wsu_ref[half:, :], preferred_element_type=f32))
    shared = jnp.dot((_silu(g) * u).astype(bf16), wsd_ref[...], preferred_element_type=f32)

    for s in (0, 1):
        @pl.when(has_next2 & (slot == s))
        def _(s=s):
            table_fetch(step + 2, s).wait()

        @pl.when(slot == s)
        def _(s=s):
            def drain(r, c):
                for k in range(top_k):
                    _row_copy(os_hbm, 0, buf.at[s, k], 0, sems.at[s]).wait()
                return c

            lax.fori_loop(0, tm, drain, 0)

    y_lo = shared[:, :half]
    y_hi = shared[:, half:]
    wt = wt_ref[...]
    for k in range(top_k):
        e_lo, e_hi = _unpack_pair(buf[slot, k])
        wk = wt[:, k:k + 1]
        y_lo = y_lo + wk * e_lo
        y_hi = y_hi + wk * e_hi
    ffn = jnp.concatenate([y_lo, y_hi], axis=1)
    y_ref[...] = _layer_norm(DEEPNORM_ALPHA * x1_ref[...] + (1.0 + g2_ref[...]) * ffn, lg_ref[...], lb_ref[...])


def _combine(dest2d, out_sorted, x1, hp, wts, mod, seq_len, ln_g, ln_b, ws_gate, ws_up, ws_down, tm):
    t, d = x1.shape
    half = d // 2
    sf = ws_gate.shape[1]
    (g2_spec,) = _mod_specs(mod, tm, seq_len, (5,), d)
    row = lambda m: (m, 0)
    const = lambda m: (0, 0)
    return pl.pallas_call(
        functools.partial(_combine_body, top_k=TOP_K, n_steps=t // tm),
        out_shape=jax.ShapeDtypeStruct((t, d), f32),
        grid=(t // tm,),
        in_specs=[pl.BlockSpec(memory_space=pl.ANY), pl.BlockSpec(memory_space=pl.ANY),
                  pl.BlockSpec((tm, d), row), pl.BlockSpec((tm, half), row), pl.BlockSpec((tm, LANES), row),
                  g2_spec, pl.BlockSpec((1, d), const), pl.BlockSpec((1, d), const),
                  pl.BlockSpec((d, sf), const), pl.BlockSpec((d, sf), const), pl.BlockSpec((sf, d), const)],
        out_specs=pl.BlockSpec((tm, d), row),
        scratch_shapes=[pltpu.SMEM((tm * TOP_K,), i32), pltpu.SMEM((tm * TOP_K,), i32),
                        pltpu.VMEM((2, TOP_K, tm, half), u32),
                        pltpu.SemaphoreType.DMA(()), pltpu.SemaphoreType.DMA((2,))],
        compiler_params=_cparams(("arbitrary",)),
        name="combine",
    )(dest2d, out_sorted, x1, hp, wts, mod, ln_g.reshape(1, d), ln_b.reshape(1, d), ws_gate, ws_up, ws_down)


def _pick(total, pref, unit=SUBLANES):
    if total <= pref:
        return total
    c = pref - pref % unit
    while total % c:
        c -= unit
    return c


def kernel(x_prompt, x_sample, state_dn_S, state_dn_conv, state_sc_conv, c_prompt, c_sample, w_ada, b_ada, w_in, w_dn_conv, dn_a_log, dn_dt_bias, dn_norm_w, w_sc_conv, w_dn_out, w_sc_out, w_o, ln1_g, ln1_b, w_router, router_bias, w_e_gate, w_e_up, w_e_down, w_s_gate, w_s_up, w_s_down, ln2_g, ln2_b):
    bp, lp, d = x_prompt.shape
    bs, ls, _ = x_sample.shape
    n_heads = dn_a_log.shape[0]
    dk = dn_norm_w.shape[0]
    assert dk == LANES
    dnw = n_heads * dk
    scw = w_sc_conv.shape[1]
    n_e = w_router.shape[1]
    tp, ts = bp * lp, bs * ls

    w_in_t = w_in.T
    n_a = 4 * dnw
    o_sc = n_a + 2 * n_heads
    o_gd = o_sc + 3 * scw
    tn_a = _pick(n_a, 1024, LANES)
    tn_c = _pick(math.gcd(2 * d, 3 * scw), 1024, LANES)
    rows_a = [i * tn_a for i in range(n_a // tn_a)]
    rows_c = [o_gd + i * tn_c for i in range(2 * d // tn_c)] + [o_sc + i * tn_c for i in range(3 * scw // tn_c)]
    c_gd = 0
    c_gs = d
    c_sc = 2 * d
    assert 2 * n_heads <= LANES and n_a + LANES <= w_in.shape[1]
    lane_pad = lambda v: jnp.concatenate([jnp.zeros((n_heads,), f32), v.astype(f32),
                                          jnp.zeros((LANES - 2 * n_heads,), f32)])[None, :]
    par = jnp.concatenate([lane_pad(dn_a_log), lane_pad(dn_dt_bias), jnp.zeros((SUBLANES - 2, LANES), f32)],
                          axis=0)[None]

    w_dn_b = w_dn_out.astype(bf16)
    w_sc_b = w_sc_out.astype(bf16)
    w_o_b = w_o.astype(bf16)
    ws_gate_b = w_s_gate.astype(bf16)
    ws_up_b = w_s_up.astype(bf16)
    ws_down_b = w_s_down.astype(bf16)

    n_c = bp + bs
    m_pad = -(-n_c // SUBLANES) * SUBLANES
    c_all = jnp.concatenate([c_prompt, c_sample, jnp.zeros((m_pad - n_c, d), f32)], axis=0)
    mod = _ada(c_all, w_ada, b_ada)
    mod_p = mod[:bp].reshape(bp, 1, 6 * d)
    mod_s = jnp.repeat(mod[bp:n_c], ls, axis=0)

    def group(x3, mod_g, s0, dn_hist, sc_hist, chunk, lt, cnt_in):
        bn, seq, _ = x3.shape
        t = bn * seq
        x = x3.reshape(t, d)
        tile = (lambda pref: _pick(seq, pref)) if mod_g.ndim == 3 else (lambda pref: _pick(t, pref))
        proj_a = _modmm(x, mod_g, seq, w_in_t, rows_a, tn_a, bf16, tile(1024))
        proj = _modmm(x, mod_g, seq, w_in_t, rows_c, tn_c, bf16, tile(1024))
        ba = _modmm(x, mod_g, seq, w_in_t, [n_a], LANES, f32, tile(1024))
        hist8 = jnp.concatenate([jnp.zeros((bn, SUBLANES - dn_hist.shape[1], 3 * dnw), f32), dn_hist], axis=1)
        o_n, s_new, nbq, nbk, nbv = _deltanet(
            proj_a.reshape(bn, seq, -1), ba.reshape(bn, seq, -1), par, w_dn_conv, dn_norm_w, hist8, s0,
            n_heads=n_heads, heads_per_step=n_heads, chunk=chunk, lt=lt,
            seqs_per_step=4 if (seq == chunk and bn % 4 == 0) else 1)
        keep = w_dn_conv.shape[0] - 1
        dn_buf = jnp.concatenate([nbq, nbk, nbv], axis=2)[:, SUBLANES - keep:, :]

        zrow = jnp.zeros((bn, SUBLANES - 2, scw), f32)
        hist_a = jnp.concatenate([sc_hist[:, 1:2], jnp.zeros((bn, 1, scw), f32), zrow], axis=1).reshape(bn * SUBLANES, scw)
        hist_b = jnp.concatenate([sc_hist[:, 0:1], sc_hist[:, 1:2], zrow], axis=1).reshape(bn * SUBLANES, scw)
        rows_sc = seq if seq > SUBLANES else _pick(t, 256)
        scin, tail = _short_conv(proj, c_sc, scw, w_sc_conv, hist_a, hist_b, seq, rows_sc, min(scw, 256))
        sc_keep = w_sc_conv.shape[0] - 1
        sc_buf = tail.reshape(bn, SUBLANES, scw)[:, SUBLANES - sc_keep:, :]

        merged = _merge(o_n.reshape(t, dnw), scin, proj, c_gd, c_gs, w_dn_b, w_sc_b, tile(256))
        x1, hp, idx, wts, pos, cnt = _mix(merged, x, mod_g, seq, w_o_b, ln1_g, ln1_b, w_router, router_bias,
                                          cnt_in, tile(512 if mod_g.ndim == 3 else 256))
        return dict(x1=x1, hp=hp, idx=idx[:, :TOP_K], wts=wts, pos=pos[:, :TOP_K], cnt=cnt,
                    s=s_new, dn_buf=dn_buf, sc_buf=sc_buf, mod=mod_g, seq=seq, tm_d=tile(256))

    zeros_s = jnp.zeros((bp,) + state_dn_S.shape[1:], f32)
    zeros_dn = jnp.zeros((bp,) + state_dn_conv.shape[1:], f32)
    zeros_sc = jnp.zeros((bp,) + state_sc_conv.shape[1:], f32)
    chunk_p = math.gcd(DN_CHUNK, lp)
    chunk_s = math.gcd(DN_CHUNK, ls)
    gp = group(x_prompt, mod_p, zeros_s, zeros_dn, zeros_sc, chunk_p, _pick(lp, 512), jnp.zeros((1, n_e), f32))
    gs = group(x_sample, mod_s, state_dn_S, state_dn_conv, state_sc_conv, chunk_s, ls, gp["cnt"])

    tm_e = 256
    counts = gs["cnt"][0].astype(i32)
    tiles_per_e = (counts + tm_e - 1) // tm_e
    tile_end = jnp.cumsum(tiles_per_e)
    start_row = (tile_end - tiles_per_e) * tm_e
    n_rows = (tp + ts) * TOP_K
    max_tiles = n_rows // tm_e + n_e
    tile_ids = jnp.arange(max_tiles, dtype=i32)
    tile_expert = jnp.minimum(jnp.sum((tile_end[None, :] <= tile_ids[:, None]).astype(i32), axis=1), n_e - 1)
    n_tiles = tile_end[-1:].astype(i32)
    e_ids = jnp.arange(n_e, dtype=i32)
    nonempty = tiles_per_e > 0
    ordinal = jnp.cumsum(nonempty.astype(i32)) - 1
    later = (e_ids[None, :] > e_ids[:, None]) & nonempty[None, :]
    next_e = jnp.min(jnp.where(later, e_ids[None, :], n_e), axis=1)
    next_e = jnp.where(next_e == n_e, -1, next_e).astype(i32)
    onehot_te = tile_expert[:, None] == e_ids[None, :]
    lookup = lambda table: jnp.sum(jnp.where(onehot_te, table[None, :], 0), axis=1).astype(i32)
    first = (tile_ids == lookup(tile_end - tiles_per_e)).astype(i32)
    slot = lookup(ordinal) % 2
    nxt = lookup(next_e)

    for g in (gp, gs):
        t = g["x1"].shape[0]
        idx_flat = g["idx"].reshape(-1)
        start_flat = jnp.sum(jnp.where(idx_flat[:, None] == e_ids[None, :], start_row[None, :], 0), axis=1)
        g["dest"] = (start_flat + g["pos"].reshape(-1)).astype(i32).reshape(t // g["tm_d"], g["tm_d"] * TOP_K)
    assert gp["tm_d"] == gs["tm_d"]
    zero_flag = ((tile_ids >= n_tiles[0]) | (tile_ids == lookup(tile_end) - 1)).astype(i32)
    xs = _dispatch(zero_flag, jnp.concatenate([gp["dest"], gs["dest"]], axis=0), gp["hp"], gs["hp"],
                   max_tiles * tm_e, tm_e, gp["tm_d"])
    out_sorted = _experts(tile_expert, n_tiles, first, slot, nxt, xs, w_e_gate, w_e_up, w_e_down, tm_e)
    ys = [_combine(g["dest"], out_sorted, g["x1"], g["hp"], g["wts"], g["mod"], g["seq"], ln2_g, ln2_b,
                   ws_gate_b, ws_up_b, ws_down_b, g["tm_d"]) for g in (gp, gs)]

    y_p = ys[0].reshape(bp, lp, d)
    y_s = ys[1].reshape(bs, ls, d)
    return (y_p, y_s, gp["s"], gp["dn_buf"], gp["sc_buf"], gs["s"], gs["dn_buf"], gs["sc_buf"])
```

```python
import functools
import math

import jax
import jax.numpy as jnp
from jax import lax
from jax.experimental import pallas as pl
from jax.experimental.pallas import tpu as pltpu

f32 = jnp.float32
bf16 = jnp.bfloat16
i32 = jnp.int32
u32 = jnp.uint32

TOP_K = 8
ROUTED_SCALE = 2.5
DN_CHUNK = 64
DEPTH = 1
DEEPNORM_ALPHA = (2 * DEPTH) ** 0.25
LN_EPS = 1e-5
RMS_EPS = 1e-6

LANES = 128
SUBLANES = 8
VMEM_LIMIT = 56 * 1024 * 1024


def _cparams(sem, vmem=VMEM_LIMIT):
    return pltpu.CompilerParams(dimension_semantics=sem, vmem_limit_bytes=vmem)


def _silu(x):
    return x * jax.nn.sigmoid(x)


def _bdot(a, b):
    return jnp.dot(a.astype(bf16), b.astype(bf16), preferred_element_type=f32)


def _layer_norm(x, g, b):
    mu = jnp.mean(x, axis=-1, keepdims=True)
    xc = x - mu
    var = jnp.mean(xc * xc, axis=-1, keepdims=True)
    return xc * lax.rsqrt(var + LN_EPS) * g + b


def _pack_pair(a, b):
    ab = pltpu.bitcast(a.astype(bf16).astype(f32), u32)
    bb = pltpu.bitcast(b.astype(bf16).astype(f32), u32)
    return (ab >> 16) | (bb & jnp.uint32(0xFFFF0000))


def _unpack_pair(p):
    lo = pltpu.bitcast(p << 16, f32)
    hi = pltpu.bitcast(p & jnp.uint32(0xFFFF0000), f32)
    return lo, hi


def _ada_body(c_ref, w_ref, b_ref, o_ref):
    s = _silu(c_ref[...])
    o_ref[...] = _bdot(s, w_ref[...]) + b_ref[...]


def _ada(c_all, w_ada, b_ada):
    m, d = c_all.shape
    n = w_ada.shape[1]
    tn = _pick(n, 1024, LANES)
    return pl.pallas_call(
        _ada_body,
        out_shape=jax.ShapeDtypeStruct((m, n), f32),
        grid=(n // tn,),
        in_specs=[pl.BlockSpec((m, d), lambda j: (0, 0)),
                  pl.BlockSpec((d, tn), lambda j: (0, j)),
                  pl.BlockSpec((1, tn), lambda j: (0, j))],
        out_specs=pl.BlockSpec((m, tn), lambda j: (0, j)),
        compiler_params=_cparams(("arbitrary",)),
        name="ada",
    )(c_all, w_ada, b_ada.reshape(1, n))


def _modmm_body(off_ref, x_ref, sc_ref, sh_ref, w_ref, o_ref, h_scr):
    del off_ref

    @pl.when(pl.program_id(1) == 0)
    def _():
        h_scr[...] = (x_ref[...] * (1.0 + sc_ref[...]) + sh_ref[...]).astype(bf16)

    o_ref[...] = lax.dot_general(h_scr[...], w_ref[...].astype(bf16), (((1,), (1,)), ((), ())),
                                 preferred_element_type=f32).astype(o_ref.dtype)


def _mod_specs(mod, tm, seq_len, cols, d):
    if mod.ndim == 3:
        tiles_per_seq = seq_len // tm
        return [pl.BlockSpec((None, 1, d), lambda m, *_, c=c: (m // tiles_per_seq, 0, c)) for c in cols]
    return [pl.BlockSpec((tm, d), lambda m, *_, c=c: (m, c)) for c in cols]


def _modmm(x, mod, seq_len, w_t, row_starts, tn, out_dtype, tm):
    t, d = x.shape
    assert all(r % SUBLANES == 0 for r in row_starts)
    n_blocks = len(row_starts)
    sc_spec, sh_spec = _mod_specs(mod, tm, seq_len, (1, 0), d)
    return pl.pallas_call(
        _modmm_body,
        out_shape=jax.ShapeDtypeStruct((t, n_blocks * tn), out_dtype),
        grid_spec=pltpu.PrefetchScalarGridSpec(
            num_scalar_prefetch=1,
            grid=(t // tm, n_blocks),
            in_specs=[pl.BlockSpec((tm, d), lambda m, j, off: (m, 0)), sc_spec, sh_spec,
                      pl.BlockSpec((pl.Element(tn), pl.Element(d)), lambda m, j, off: (off[j] * SUBLANES, 0))],
            out_specs=pl.BlockSpec((tm, tn), lambda m, j, off: (m, j)),
            scratch_shapes=[pltpu.VMEM((tm, d), bf16)]),
        compiler_params=_cparams(("arbitrary", "arbitrary")),
        name="modmm",
    )(jnp.asarray([r // SUBLANES for r in row_starts], i32), x, mod, mod, w_t)


def _cumsum_rows(x, n):
    row = lax.broadcasted_iota(i32, x.shape, 0)
    s = 1
    while s < n:
        x = x + jnp.where(row >= s, pltpu.roll(x, s, axis=0), 0.0)
        s *= 2
    return x


def _conv_silu(x, prev, w):
    c = x.shape[0]
    taps = w.shape[0]
    xc = jnp.concatenate([prev, x], axis=0)
    y = w[taps - 1:taps, :] * x
    for i in range(taps - 1):
        back = taps - 1 - i
        y = y + w[i:i + 1, :] * pltpu.roll(xc, back, axis=0)[SUBLANES:SUBLANES + c, :]
    return _silu(y)


def _dn_body(q_ref, k_ref, v_ref, z_ref, ba_ref, par_ref, cwq_ref, cwk_ref, cwv_ref, nw_ref,
             hq_ref, hk_ref, hv_ref, s0_ref,
             o_ref, so_ref, nbq_ref, nbk_ref, nbv_ref,
             s_scr, pq, pk, pv, *, chunk, heads, n_chunks, n_seq):
    t = pl.program_id(2)
    dk = LANES

    @pl.when(t == 0)
    def _init():
        s_scr[...] = s0_ref[...]
        pq[...] = hq_ref[...]
        pk[...] = hk_ref[...]
        pv[...] = hv_ref[...]

    neg_a = -jnp.exp(par_ref[0:1, :])
    dt_b = par_ref[1:2, :]
    nw = nw_ref[...]
    ii = lax.broadcasted_iota(i32, (chunk, chunk), 0)
    jj = lax.broadcasted_iota(i32, (chunk, chunk), 1)
    causal = ii >= jj
    strict = ii > jj
    eye = (ii == jj).astype(f32)
    zpad = jnp.zeros((LANES - chunk, LANES), f32)
    sls = [slice(h * dk, (h + 1) * dk) for h in range(heads)]
    pairs = [(s, h) for s in range(n_seq) for h in range(heads)]
    ps = range(len(pairs))

    def do_chunk(ci):
        r0 = pl.multiple_of(ci * chunk, chunk)
        rows = pl.ds(r0, chunk)
        beta_all, gc_all, gc_t, eg_all, qc, kc, vc = [], [], [], [], [], [], []
        for s in range(n_seq):
            ba = ba_ref[s, rows, :]
            beta_all.append(jax.nn.sigmoid(ba))
            xs = ba + dt_b
            softplus = jnp.maximum(xs, 0.0) + jnp.log1p(jnp.exp(-jnp.abs(xs)))
            g = _cumsum_rows(neg_a * softplus, chunk)
            gc_all.append(g)
            gc_t.append(jnp.concatenate([g, zpad], axis=0).T)
            eg_all.append(jnp.exp(g))
            xq = q_ref[s, rows, :].astype(f32)
            xk = k_ref[s, rows, :].astype(f32)
            xv = v_ref[s, rows, :].astype(f32)
            qc.append(_conv_silu(xq, pq[s], cwq_ref[...]))
            kc.append(_conv_silu(xk, pk[s], cwk_ref[...]))
            vc.append(_conv_silu(xv, pv[s], cwv_ref[...]))
            pq[s] = xq[chunk - SUBLANES:, :]
            pk[s] = xk[chunk - SUBLANES:, :]
            pv[s] = xv[chunk - SUBLANES:, :]

        qn = [qc[s][:, sls[h]] for s, h in pairs]
        kn = [kc[s][:, sls[h]] for s, h in pairs]
        vn = [vc[s][:, sls[h]] for s, h in pairs]
        qn = [q * (lax.rsqrt(jnp.sum(q * q, axis=-1, keepdims=True) + RMS_EPS) * (dk ** -0.5)) for q in qn]
        kn = [k * lax.rsqrt(jnp.sum(k * k, axis=-1, keepdims=True) + RMS_EPS) for k in kn]
        beta = [beta_all[s][:, h:h + 1] for s, h in pairs]
        gc = [gc_all[s][:, heads + h:heads + h + 1] for s, h in pairs]
        eg = [eg_all[s][:, heads + h:heads + h + 1] for s, h in pairs]
        qk_kk = [lax.dot_general(jnp.concatenate([qn[i], kn[i]], axis=0).astype(bf16), kn[i].astype(bf16),
                                 (((1,), (1,)), ((), ())), preferred_element_type=f32) for i in ps]
        decay = []
        for i, (s, h) in enumerate(pairs):
            diff = gc[i] - gc_t[s][heads + h:heads + h + 1, 0:chunk]
            decay.append(jnp.where(causal, jnp.exp(jnp.where(causal, diff, 0.0)), 0.0))
        qk = [qk_kk[i][:chunk, :] * decay[i] for i in ps]
        p = [-jnp.where(strict, qk_kk[i][chunk:, :] * beta[i] * decay[i], 0.0) for i in ps]
        tinv = [eye + p[i] for i in ps]
        n = 1
        while 2 * n < chunk:
            p = [_bdot(p[i], p[i]) for i in ps]
            tinv = [tinv[i] + _bdot(p[i], tinv[i]) for i in ps]
            n *= 2
        sol = [_bdot(tinv[i], jnp.concatenate([vn[i] * beta[i], kn[i] * (beta[i] * eg[i])], axis=1))
               for i in ps]
        s_old = [s_scr[s, h] for s, h in pairs]
        ks_qs = [_bdot(jnp.concatenate([sol[i][:, dk:], qn[i] * eg[i]], axis=0), s_old[i]) for i in ps]
        u = [sol[i][:, :dk] - ks_qs[i][:chunk, :] for i in ps]
        o = [ks_qs[i][chunk:, :] + _bdot(qk[i], u[i]) for i in ps]
        for i, (s, h) in enumerate(pairs):
            gc_last = gc[i][chunk - 1:chunk, :]
            kd = kn[i] * jnp.exp(gc_last - gc[i])
            s_scr[s, h] = s_old[i] * jnp.exp(gc_last) + lax.dot_general(
                kd.astype(bf16), u[i].astype(bf16), (((0,), (0,)), ((), ())), preferred_element_type=f32)
        for i, (s, h) in enumerate(pairs):
            zh = z_ref[s, rows, sls[h]].astype(f32)
            on = o[i] * lax.rsqrt(jnp.mean(o[i] * o[i], axis=-1, keepdims=True) + RMS_EPS) * nw * _silu(zh)
            o_ref[s, rows, sls[h]] = on.astype(o_ref.dtype)

    if n_chunks == 1:
        do_chunk(0)
    else:
        def loop_body(ci, carry):
            do_chunk(ci)
            return carry
        lax.fori_loop(0, n_chunks, loop_body, 0)

    @pl.when(t == pl.num_programs(2) - 1)
    def _fin():
        so_ref[...] = s_scr[...]
        nbq_ref[...] = pq[...]
        nbk_ref[...] = pk[...]
        nbv_ref[...] = pv[...]


def _deltanet(proj3, ba3, par, w_conv, norm_w, hist8, s0, *, n_heads, heads_per_step, seqs_per_step, chunk, lt):
    bn, seq, _ = proj3.shape
    dk = LANES
    dnw = n_heads * dk
    hg = heads_per_step
    nb = seqs_per_step
    gw = hg * dk
    n_g = n_heads // hg
    n_t = seq // lt
    kern = functools.partial(_dn_body, chunk=chunk, heads=hg, n_chunks=lt // chunk, n_seq=nb)

    def col(off):
        return pl.BlockSpec((nb, lt, gw), lambda b, g, t, off=off: (b, t, off * n_g + g))

    def cw(off):
        return pl.BlockSpec((w_conv.shape[0], gw), lambda b, g, t, off=off: (0, off * n_g + g))

    def hist(off):
        return pl.BlockSpec((nb, SUBLANES, gw), lambda b, g, t, off=off: (b, 0, off * n_g + g))

    nb_spec = pl.BlockSpec((nb, SUBLANES, gw), lambda b, g, t: (b, 0, g))
    nb_shape = jax.ShapeDtypeStruct((bn, SUBLANES, dnw), f32)
    return pl.pallas_call(
        kern,
        out_shape=(jax.ShapeDtypeStruct((bn, seq, dnw), bf16),
                   jax.ShapeDtypeStruct(s0.shape, f32), nb_shape, nb_shape, nb_shape),
        grid=(bn // nb, n_g, n_t),
        in_specs=[col(0), col(1), col(2), col(3),
                  pl.BlockSpec((nb, lt, LANES), lambda b, g, t: (b, t, g)),
                  pl.BlockSpec((None, SUBLANES, LANES), lambda b, g, t: (g, 0, 0)),
                  cw(0), cw(1), cw(2),
                  pl.BlockSpec((1, dk), lambda b, g, t: (0, 0)),
                  hist(0), hist(1), hist(2),
                  pl.BlockSpec((nb, hg, dk, dk), lambda b, g, t: (b, g, 0, 0))],
        out_specs=(pl.BlockSpec((nb, lt, gw), lambda b, g, t: (b, t, g)),
                   pl.BlockSpec((nb, hg, dk, dk), lambda b, g, t: (b, g, 0, 0)),
                   nb_spec, nb_spec, nb_spec),
        scratch_shapes=[pltpu.VMEM((nb, hg, dk, dk), f32),
                        pltpu.VMEM((nb, SUBLANES, gw), f32), pltpu.VMEM((nb, SUBLANES, gw), f32),
                        pltpu.VMEM((nb, SUBLANES, gw), f32)],
        compiler_params=_cparams(("arbitrary", "arbitrary", "arbitrary")),
        name="deltanet",
    )(proj3, proj3, proj3, proj3, ba3, par, w_conv, w_conv, w_conv, norm_w.reshape(1, dk),
      hist8, hist8, hist8, s0)


def _sc_body(b_ref, c_ref, x_ref, w_ref, ha_ref, hb_ref, o_ref, tail_ref, *, seq_len):
    u = c_ref[...].astype(f32) * x_ref[...].astype(f32)
    rows = u.shape[0]
    tin = lax.broadcasted_iota(i32, u.shape, 0) % seq_len
    w = w_ref[...]
    conv = (w[2:3, :] * u
            + w[1:2, :] * jnp.where(tin >= 1, pltpu.roll(u, 1, axis=0), 0.0)
            + w[0:1, :] * jnp.where(tin >= 2, pltpu.roll(u, 2, axis=0), 0.0))
    corr = w[1:2, :] * ha_ref[...] + w[0:1, :] * hb_ref[...]
    bv = b_ref[...].astype(f32)
    if corr.shape[0] == rows:
        o_ref[...] = (bv * (conv + corr)).astype(o_ref.dtype)
        tail_ref[...] = u
    else:
        o_ref[...] = (bv * conv).astype(o_ref.dtype)
        o_ref[0:SUBLANES, :] = (bv[0:SUBLANES, :] * (conv[0:SUBLANES, :] + corr)).astype(o_ref.dtype)
        tail_ref[...] = u[rows - SUBLANES:, :]


def _short_conv(proj, col0, scw, w_sc, hist_a, hist_b, seq_len, rows_per_step, tw):
    t = proj.shape[0]
    n_seq = t // seq_len
    hr = SUBLANES if rows_per_step == seq_len else rows_per_step
    n_w = scw // tw
    cb = col0 // tw

    def pin(off):
        return pl.BlockSpec((rows_per_step, tw), lambda r, j, off=off: (r, cb + off * n_w + j))

    h_spec = pl.BlockSpec((hr, tw), lambda r, j: (r, j))
    return pl.pallas_call(
        functools.partial(_sc_body, seq_len=seq_len),
        out_shape=(jax.ShapeDtypeStruct((t, scw), bf16),
                   jax.ShapeDtypeStruct((n_seq * SUBLANES, scw), f32)),
        grid=(t // rows_per_step, n_w),
        in_specs=[pin(0), pin(1), pin(2), pl.BlockSpec((w_sc.shape[0], tw), lambda r, j: (0, j)),
                  h_spec, h_spec],
        out_specs=(pl.BlockSpec((rows_per_step, tw), lambda r, j: (r, j)),
                   pl.BlockSpec((hr, tw), lambda r, j: (r, j))),
        compiler_params=_cparams(("arbitrary", "arbitrary")),
        name="short_conv",
    )(proj, proj, proj, w_sc, hist_a, hist_b)


def _merge_body(o_ref, s_ref, gd_ref, gs_ref, wdn_ref, wsc_ref, m_ref):
    ydn = jnp.dot(o_ref[...], wdn_ref[...], preferred_element_type=f32)
    ysc = jnp.dot(s_ref[...], wsc_ref[...], preferred_element_type=f32)
    m = jax.nn.sigmoid(gd_ref[...].astype(f32)) * ydn + jax.nn.sigmoid(gs_ref[...].astype(f32)) * ysc
    m_ref[...] = m.astype(m_ref.dtype)


def _merge(o_n, scin, proj, gd_col, gs_col, w_dn, w_sc, tm):
    t, dnw = o_n.shape
    scw = scin.shape[1]
    d = w_dn.shape[1]
    return pl.pallas_call(
        _merge_body,
        out_shape=jax.ShapeDtypeStruct((t, d), bf16),
        grid=(t // tm,),
        in_specs=[pl.BlockSpec((tm, dnw), lambda m: (m, 0)),
                  pl.BlockSpec((tm, scw), lambda m: (m, 0)),
                  pl.BlockSpec((tm, d), lambda m: (m, gd_col // d)),
                  pl.BlockSpec((tm, d), lambda m: (m, gs_col // d)),
                  pl.BlockSpec((dnw, d), lambda m: (0, 0)),
                  pl.BlockSpec((scw, d), lambda m: (0, 0))],
        out_specs=pl.BlockSpec((tm, d), lambda m: (m, 0)),
        compiler_params=_cparams(("arbitrary",)),
        name="merge",
    )(o_n, scin, proj, proj, w_dn, w_sc)


def _split_dot3(a, b):
    a_hi = a.astype(bf16)
    a_lo = (a - a_hi.astype(f32)).astype(bf16)
    b_hi = b.astype(bf16)
    b_lo = (b - b_hi.astype(f32)).astype(bf16)
    return (jnp.dot(a_hi, b_hi, preferred_element_type=f32) + jnp.dot(a_hi, b_lo, preferred_element_type=f32)
            + jnp.dot(a_lo, b_hi, preferred_element_type=f32))


def _mix_body(m_ref, x_ref, g1_ref, sh2_ref, sc2_ref, wo_ref, lg_ref, lb_ref, wr_ref, rb_ref, cnt_in_ref,
              x1_ref, hp_ref, idx_ref, wt_ref, pos_ref, cnt_ref, cnt_scr, *, top_k, parts):
    step = pl.program_id(0)

    @pl.when(step == 0)
    def _():
        cnt_scr[...] = cnt_in_ref[...]

    tm_all = m_ref.shape[0]
    pr = tm_all // parts
    n_e = wr_ref.shape[1]
    half = x_ref.shape[1] // 2
    blocks = [pl.ds(i * pr, pr) for i in range(parts)]
    per_token = g1_ref.shape[0] != 1
    mod = lambda ref, rows: ref[rows, :] if per_token else ref[...]
    mix = [jnp.dot(m_ref[rows, :], wo_ref[...], preferred_element_type=f32) for rows in blocks]
    h2 = []
    for rows, mx_ in zip(blocks, mix):
        x1 = _layer_norm(DEEPNORM_ALPHA * x_ref[rows, :] + (1.0 + mod(g1_ref, rows)) * mx_, lg_ref[...], lb_ref[...])
        x1_ref[rows, :] = x1
        h = x1 * (1.0 + mod(sc2_ref, rows)) + mod(sh2_ref, rows)
        hp_ref[rows, :] = _pack_pair(h[:, :half], h[:, half:])
        h2.append(h)
    scores = [jax.nn.sigmoid(_split_dot3(h, wr_ref[...])) for h in h2]
    work = [s + rb_ref[...] for s in scores]
    lane_e = lax.broadcasted_iota(i32, (pr, n_e), 1).astype(f32)
    lane_o = lax.broadcasted_iota(i32, (pr, LANES), 1)
    sel = [jnp.zeros((pr, n_e), f32) for _ in blocks]
    wsum = [jnp.zeros((pr, 1), f32) for _ in blocks]
    picks = [[] for _ in blocks]
    for _ in range(top_k):
        for b in range(parts):
            mx = jnp.max(work[b], axis=-1, keepdims=True)
            idx = jnp.min(jnp.where(work[b] == mx, lane_e, float(n_e)), axis=-1, keepdims=True)
            hit = lane_e == idx
            wk = jnp.sum(jnp.where(hit, scores[b], 0.0), axis=-1, keepdims=True)
            work[b] = jnp.where(hit, -jnp.inf, work[b])
            sel[b] = jnp.where(hit, 1.0, sel[b])
            wsum[b] = wsum[b] + wk
            picks[b].append((idx, hit, wk))

    ri = lax.broadcasted_iota(i32, (pr, pr), 0)
    ci = lax.broadcasted_iota(i32, (pr, pr), 1)
    tri = jnp.where(ri > ci, 1.0, 0.0).astype(bf16)
    running = cnt_scr[...]
    for b, rows in enumerate(blocks):
        before = jnp.dot(tri, sel[b].astype(bf16), preferred_element_type=f32) + running
        running = running + jnp.sum(sel[b], axis=0, keepdims=True)
        idx_o = jnp.zeros((pr, LANES), i32)
        wt_o = jnp.zeros((pr, LANES), f32)
        pos_o = jnp.zeros((pr, LANES), i32)
        for k, (idx, hit, wk) in enumerate(picks[b]):
            pk = jnp.sum(jnp.where(hit, before, 0.0), axis=-1, keepdims=True).astype(i32)
            idx_o = jnp.where(lane_o == k, idx.astype(i32), idx_o)
            wt_o = jnp.where(lane_o == k, wk / wsum[b] * ROUTED_SCALE, wt_o)
            pos_o = jnp.where(lane_o == k, pk, pos_o)
        idx_ref[rows, :] = idx_o
        wt_ref[rows, :] = wt_o
        pos_ref[rows, :] = pos_o
    cnt_scr[...] = running
    cnt_ref[...] = running


def _mix(merged, x, mod, seq_len, w_o, ln_g, ln_b, w_router, router_bias, cnt_in, tm):
    t, d = x.shape
    n_e = w_router.shape[1]
    g1_spec, sh2_spec, sc2_spec = _mod_specs(mod, tm, seq_len, (2, 3, 4), d)
    row = lambda m: (m, 0)
    const = lambda m: (0, 0)
    lane_out = jax.ShapeDtypeStruct((t, LANES), i32)
    return pl.pallas_call(
        functools.partial(_mix_body, top_k=TOP_K, parts=2 if tm % (2 * SUBLANES) == 0 else 1),
        out_shape=(jax.ShapeDtypeStruct((t, d), f32), jax.ShapeDtypeStruct((t, d // 2), u32),
                   lane_out, jax.ShapeDtypeStruct((t, LANES), f32), lane_out,
                   jax.ShapeDtypeStruct((1, n_e), f32)),
        grid=(t // tm,),
        in_specs=[pl.BlockSpec((tm, d), row), pl.BlockSpec((tm, d), row), g1_spec, sh2_spec, sc2_spec,
                  pl.BlockSpec((d, d), const, pipeline_mode=pl.Buffered(1)),
                  pl.BlockSpec((1, d), const), pl.BlockSpec((1, d), const),
                  pl.BlockSpec((d, n_e), const), pl.BlockSpec((1, n_e), const), pl.BlockSpec((1, n_e), const)],
        out_specs=(pl.BlockSpec((tm, d), row), pl.BlockSpec((tm, d // 2), row),
                   pl.BlockSpec((tm, LANES), row), pl.BlockSpec((tm, LANES), row), pl.BlockSpec((tm, LANES), row),
                   pl.BlockSpec((1, n_e), const)),
        scratch_shapes=[pltpu.VMEM((1, n_e), f32)],
        compiler_params=_cparams(("arbitrary",)),
        name="mix",
    )(merged, x, mod, mod, mod, w_o, ln_g.reshape(1, d), ln_b.reshape(1, d), w_router,
      router_bias.reshape(1, n_e), cnt_in)


def _row_copy(src, src_row, dst, dst_row, sem):
    return pltpu.make_async_copy(src.at[pl.ds(src_row, 1)], dst.at[pl.ds(dst_row, 1)], sem)


def _dispatch_body(flag_ref, dest_hbm, ha_ref, hb_ref, xs_out, dsm0, dsm1, zeros, sem_idx, sem_zero, sem,
                   *, top_k, steps_a, tile_rows, n_tiles):
    step = pl.program_id(0)
    tm = ha_ref.shape[0]

    @pl.when(step == 0)
    def _():
        zeros[...] = jnp.zeros(zeros.shape, zeros.dtype)

        def tile_copy(i):
            start = pl.multiple_of(i * tile_rows, tile_rows)
            return pltpu.make_async_copy(zeros, xs_out.at[pl.ds(start, tile_rows)], sem_zero)

        def issue_zero(i, c):
            @pl.when(flag_ref[i] == 1)
            def _():
                tile_copy(i).start()
            return c

        def drain_zero(i, c):
            @pl.when(flag_ref[i] == 1)
            def _():
                tile_copy(i).wait()
            return c

        lax.fori_loop(0, n_tiles, issue_zero, 0)
        lax.fori_loop(0, n_tiles, drain_zero, 0)

    dsms = (dsm0, dsm1)
    slot = step % 2
    has_next = step + 1 < pl.num_programs(0)

    def table_fetch(tile, s):
        return pltpu.make_async_copy(dest_hbm.at[tile], dsms[s], sem_idx)

    @pl.when(step == 0)
    def _():
        first = table_fetch(0, 0)
        first.start()
        first.wait()

    for s in (0, 1):
        @pl.when(has_next & (slot == 1 - s))
        def _(s=s):
            table_fetch(step + 1, s).start()

    def scatter(h_ref, dsm):
        def issue(r8, c):
            base = pl.multiple_of(r8 * SUBLANES, SUBLANES)
            for j in range(SUBLANES):
                for k in range(top_k):
                    _row_copy(h_ref, base + j, xs_out, dsm[(base + j) * top_k + k], sem).start(priority=k % 2)
            return c

        lax.fori_loop(0, tm // SUBLANES, issue, 0)

        def drain(r, c):
            for k in range(top_k):
                _row_copy(h_ref, 0, xs_out, 0, sem).wait()
            return c

        lax.fori_loop(0, tm, drain, 0)

    for s in (0, 1):
        @pl.when((step < steps_a) & (slot == s))
        def _(s=s):
            scatter(ha_ref, dsms[s])

        @pl.when((step >= steps_a) & (slot == s))
        def _(s=s):
            scatter(hb_ref, dsms[s])

    for s in (0, 1):
        @pl.when(has_next & (slot == 1 - s))
        def _(s=s):
            table_fetch(step + 1, s).wait()


def _dispatch(flags, dest2d, hp_a, hp_b, rows, tile_rows, tm):
    half = hp_a.shape[1]
    steps_a = hp_a.shape[0] // tm
    steps_b = hp_b.shape[0] // tm
    n_tiles = rows // tile_rows
    kern = functools.partial(_dispatch_body, top_k=TOP_K, steps_a=steps_a, tile_rows=tile_rows, n_tiles=n_tiles)
    return pl.pallas_call(
        kern,
        out_shape=jax.ShapeDtypeStruct((rows, half), u32),
        grid_spec=pltpu.PrefetchScalarGridSpec(
            num_scalar_prefetch=1,
            grid=(steps_a + steps_b,),
            in_specs=[pl.BlockSpec(memory_space=pl.ANY),
                      pl.BlockSpec((tm, half), lambda m, fl: (jnp.minimum(m, steps_a - 1), 0)),
                      pl.BlockSpec((tm, half), lambda m, fl: (jnp.maximum(m - steps_a, 0), 0))],
            out_specs=pl.BlockSpec(memory_space=pl.ANY),
            scratch_shapes=[pltpu.SMEM((tm * TOP_K,), i32), pltpu.SMEM((tm * TOP_K,), i32),
                            pltpu.VMEM((tile_rows, half), u32),
                            pltpu.SemaphoreType.DMA(()), pltpu.SemaphoreType.DMA(()), pltpu.SemaphoreType.DMA(())]),
        compiler_params=_cparams(("arbitrary",)),
        name="dispatch",
    )(flags, dest2d, hp_a, hp_b)


def _expert_body(te_ref, nt_ref, first_ref, slot_ref, nxt_ref, x_ref, wg_hbm, wu_hbm, wd_hbm, o_ref,
                 wg_f, wu_f, wd_f, wg_s, wu_s, wd_s, sems):
    step = pl.program_id(0)

    def weight_copies(e, s):
        return (pltpu.make_async_copy(wg_hbm.at[e], wg_f.at[s], sems.at[s, 0]),
                pltpu.make_async_copy(wu_hbm.at[e], wu_f.at[s], sems.at[s, 1]),
                pltpu.make_async_copy(wd_hbm.at[e], wd_f.at[s], sems.at[s, 2]))

    @pl.when(step < nt_ref[0])
    def _():
        e = te_ref[step]
        s = slot_ref[step]

        @pl.when(first_ref[step] == 1)
        def _():
            @pl.when(step == 0)
            def _():
                for c in weight_copies(e, s):
                    c.start()

            for c in weight_copies(e, s):
                c.wait()
            nxt = nxt_ref[step]

            @pl.when(nxt >= 0)
            def _():
                for c in weight_copies(nxt, 1 - s):
                    c.start()

            wg_s[...] = wg_f[s].astype(bf16)
            wu_s[...] = wu_f[s].astype(bf16)
            wd_s[...] = wd_f[s].astype(bf16)

        lo, hi = _unpack_pair(x_ref[...])
        lo = lo.astype(bf16)
        hi = hi.astype(bf16)
        half = lo.shape[1]
        g = (jnp.dot(lo, wg_s[0:half, :], preferred_element_type=f32)
             + jnp.dot(hi, wg_s[half:, :], preferred_element_type=f32))
        u = (jnp.dot(lo, wu_s[0:half, :], preferred_element_type=f32)
             + jnp.dot(hi, wu_s[half:, :], preferred_element_type=f32))
        hid = (_silu(g) * u).astype(bf16)
        out = jnp.dot(hid, wd_s[...], preferred_element_type=f32)
        o_ref[...] = _pack_pair(out[:, :half], out[:, half:])


def _experts(tile_expert, n_tiles, first, slot, nxt, xs, w_gate, w_up, w_down, tm):
    rows, half = xs.shape
    n_e, d, f = w_gate.shape
    max_tiles = rows // tm

    def tile(i, te, nt, *_):
        return (jnp.minimum(i, nt[0] - 1), 0)

    hbm = pl.BlockSpec(memory_space=pl.ANY)
    return pl.pallas_call(
        _expert_body,
        out_shape=jax.ShapeDtypeStruct((rows, half), u32),
        grid_spec=pltpu.PrefetchScalarGridSpec(
            num_scalar_prefetch=5,
            grid=(max_tiles,),
            in_specs=[pl.BlockSpec((tm, half), tile), hbm, hbm, hbm],
            out_specs=pl.BlockSpec((tm, half), tile),
            scratch_shapes=[pltpu.VMEM((2, d, f), f32), pltpu.VMEM((2, d, f), f32), pltpu.VMEM((2, f, d), f32),
                            pltpu.VMEM((d, f), bf16), pltpu.VMEM((d, f), bf16), pltpu.VMEM((f, d), bf16),
                            pltpu.SemaphoreType.DMA((2, 3))]),
        input_output_aliases={5: 0},
        compiler_params=_cparams(("arbitrary",)),
        name="experts",
    )(tile_expert, n_tiles, first, slot, nxt, xs, w_gate, w_up, w_down)


def _combine_body(dest_hbm, os_hbm, x1_ref, hp_ref, wt_ref, g2_ref, lg_ref, lb_ref, wsg_ref, wsu_ref, wsd_ref,
                  y_ref, dsm0, dsm1, buf, sem_idx, sems, *, top_k, n_steps):
    step = pl.program_id(0)
    tm = x1_ref.shape[0]
    slot = step % 2

    dsms = (dsm0, dsm1)

    def table_fetch(tile, s):
        return pltpu.make_async_copy(dest_hbm.at[tile], dsms[s], sem_idx)

    def gather(s):
        def issue(r8, c):
            base = pl.multiple_of(r8 * SUBLANES, SUBLANES)
            for j in range(SUBLANES):
                for k in range(top_k):
                    src_row = dsms[s][(base + j) * top_k + k]
                    _row_copy(os_hbm, src_row, buf.at[s, k], base + j, sems.at[s]).start(priority=k % 2)
            return c

        lax.fori_loop(0, tm // SUBLANES, issue, 0)

    @pl.when(step == 0)
    def _():
        first = table_fetch(0, 0)
        first.start()
        first.wait()
        gather(0)
        if n_steps > 1:
            second = table_fetch(1, 1)
            second.start()
            second.wait()

    has_next = step + 1 < n_steps
    has_next2 = step + 2 < n_steps
    for s in (0, 1):
        @pl.when(has_next & (slot == 1 - s))
        def _(s=s):
            gather(s)

        @pl.when(has_next2 & (slot == s))
        def _(s=s):
            table_fetch(step + 2, s).start()

    lo, hi = _unpack_pair(hp_ref[...])
    lo = lo.astype(bf16)
    hi = hi.astype(bf16)
    half = lo.shape[1]
    g = (jnp.dot(lo, wsg_ref[0:half, :], preferred_element_type=f32)
         + jnp.dot(hi, wsg_ref[half:, :], preferred_element_type=f32))
    u = (jnp.dot(lo, wsu_ref[0:half, :], preferred_element_type=f32)
         + jnp.dot(hi, wsu_ref[half:, :], preferred_element_type=f32))
    shared = jnp.dot((_silu(g) * u).astype(bf16), wsd_ref[...], preferred_element_type=f32)

    for s in (0, 1):
        @pl.when(has_next2 & (slot == s))
        def _(s=s):
            table_fetch(step + 2, s).wait()

        @pl.when(slot == s)
        def _(s=s):
            def drain(r, c):
                for k in range(top_k):
                    _row_copy(os_hbm, 0, buf.at[s, k], 0, sems.at[s]).wait()
                return c

            lax.fori_loop(0, tm, drain, 0)

    y_lo = shared[:, :half]
    y_hi = shared[:, half:]
    wt = wt_ref[...]
    for k in range(top_k):
        e_lo, e_hi = _unpack_pair(buf[slot, k])
        wk = wt[:, k:k + 1]
        y_lo = y_lo + wk * e_lo
        y_hi = y_hi + wk * e_hi
    ffn = jnp.concatenate([y_lo, y_hi], axis=1)
    y_ref[...] = _layer_norm(DEEPNORM_ALPHA * x1_ref[...] + (1.0 + g2_ref[...]) * ffn, lg_ref[...], lb_ref[...])


def _combine(dest2d, out_sorted, x1, hp, wts, mod, seq_len, ln_g, ln_b, ws_gate, ws_up, ws_down, tm):
    t, d = x1.shape
    half = d // 2
    sf = ws_gate.shape[1]
    (g2_spec,) = _mod_specs(mod, tm, seq_len, (5,), d)
    row = lambda m: (m, 0)
    const = lambda m: (0, 0)
    return pl.pallas_call(
        functools.partial(_combine_body, top_k=TOP_K, n_steps=t // tm),
        out_shape=jax.ShapeDtypeStruct((t, d), f32),
        grid=(t // tm,),
        in_specs=[pl.BlockSpec(memory_space=pl.ANY), pl.BlockSpec(memory_space=pl.ANY),
                  pl.BlockSpec((tm, d), row), pl.BlockSpec((tm, half), row), pl.BlockSpec((tm, LANES), row),
                  g2_spec, pl.BlockSpec((1, d), const), pl.BlockSpec((1, d), const),
                  pl.BlockSpec((d, sf), const), pl.BlockSpec((d, sf), const), pl.BlockSpec((sf, d), const)],
        out_specs=pl.BlockSpec((tm, d), row),
        scratch_shapes=[pltpu.SMEM((tm * TOP_K,), i32), pltpu.SMEM((tm * TOP_K,), i32),
                        pltpu.VMEM((2, TOP_K, tm, half), u32),
                        pltpu.SemaphoreType.DMA(()), pltpu.SemaphoreType.DMA((2,))],
        compiler_params=_cparams(("arbitrary",)),
        name="combine",
    )(dest2d, out_sorted, x1, hp, wts, mod, ln_g.reshape(1, d), ln_b.reshape(1, d), ws_gate, ws_up, ws_down)


def _pick(total, pref, unit=SUBLANES):
    if total <= pref:
        return total
    c = pref - pref % unit
    while total % c:
        c -= unit
    return c


def kernel(x_prompt, x_sample, state_dn_S, state_dn_conv, state_sc_conv, c_prompt, c_sample, w_ada, b_ada, w_in, w_dn_conv, dn_a_log, dn_dt_bias, dn_norm_w, w_sc_conv, w_dn_out, w_sc_out, w_o, ln1_g, ln1_b, w_router, router_bias, w_e_gate, w_e_up, w_e_down, w_s_gate, w_s_up, w_s_down, ln2_g, ln2_b):
    bp, lp, d = x_prompt.shape
    bs, ls, _ = x_sample.shape
    n_heads = dn_a_log.shape[0]
    dk = dn_norm_w.shape[0]
    assert dk == LANES
    dnw = n_heads * dk
    scw = w_sc_conv.shape[1]
    n_e = w_router.shape[1]
    tp, ts = bp * lp, bs * ls

    w_in_t = w_in.T
    n_a = 4 * dnw
    o_sc = n_a + 2 * n_heads
    o_gd = o_sc + 3 * scw
    tn_a = _pick(n_a, 1024, LANES)
    tn_c = _pick(math.gcd(2 * d, 3 * scw), 1024, LANES)
    rows_a = [i * tn_a for i in range(n_a // tn_a)]
    rows_c = [o_gd + i * tn_c for i in range(2 * d // tn_c)] + [o_sc + i * tn_c for i in range(3 * scw // tn_c)]
    c_gd = 0
    c_gs = d
    c_sc = 2 * d
    assert 2 * n_heads <= LANES and n_a + LANES <= w_in.shape[1]
    lane_pad = lambda v: jnp.concatenate([jnp.zeros((n_heads,), f32), v.astype(f32),
                                          jnp.zeros((LANES - 2 * n_heads,), f32)])[None, :]
    par = jnp.concatenate([lane_pad(dn_a_log), lane_pad(dn_dt_bias), jnp.zeros((SUBLANES - 2, LANES), f32)],
                          axis=0)[None]

    w_dn_b = w_dn_out.astype(bf16)
    w_sc_b = w_sc_out.astype(bf16)
    w_o_b = w_o.astype(bf16)
    ws_gate_b = w_s_gate.astype(bf16)
    ws_up_b = w_s_up.astype(bf16)
    ws_down_b = w_s_down.astype(bf16)

    n_c = bp + bs
    m_pad = -(-n_c // SUBLANES) * SUBLANES
    c_all = jnp.concatenate([c_prompt, c_sample, jnp.zeros((m_pad - n_c, d), f32)], axis=0)
    mod = _ada(c_all, w_ada, b_ada)
    mod_p = mod[:bp].reshape(bp, 1, 6 * d)
    mod_s = jnp.repeat(mod[bp:n_c], ls, axis=0)

    def group(x3, mod_g, s0, dn_hist, sc_hist, chunk, lt, cnt_in):
        bn, seq, _ = x3.shape
        t = bn * seq
        x = x3.reshape(t, d)
        tile = (lambda pref: _pick(seq, pref)) if mod_g.ndim == 3 else (lambda pref: _pick(t, pref))
        proj_a = _modmm(x, mod_g, seq, w_in_t, rows_a, tn_a, bf16, tile(1024))
        proj = _modmm(x, mod_g, seq, w_in_t, rows_c, tn_c, bf16, tile(1024))
        ba = _modmm(x, mod_g, seq, w_in_t, [n_a], LANES, f32, tile(1024))
        hist8 = jnp.concatenate([jnp.zeros((bn, SUBLANES - dn_hist.shape[1], 3 * dnw), f32), dn_hist], axis=1)
        o_n, s_new, nbq, nbk, nbv = _deltanet(
            proj_a.reshape(bn, seq, -1), ba.reshape(bn, seq, -1), par, w_dn_conv, dn_norm_w, hist8, s0,
            n_heads=n_heads, heads_per_step=n_heads, chunk=chunk, lt=lt,
            seqs_per_step=4 if (seq == chunk and bn % 4 == 0) else (2 if bn % 2 == 0 else 1))
        keep = w_dn_conv.shape[0] - 1
        dn_buf = jnp.concatenate([nbq, nbk, nbv], axis=2)[:, SUBLANES - keep:, :]

        zrow = jnp.zeros((bn, SUBLANES - 2, scw), f32)
        hist_a = jnp.concatenate([sc_hist[:, 1:2], jnp.zeros((bn, 1, scw), f32), zrow], axis=1).reshape(bn * SUBLANES, scw)
        hist_b = jnp.concatenate([sc_hist[:, 0:1], sc_hist[:, 1:2], zrow], axis=1).reshape(bn * SUBLANES, scw)
        rows_sc = seq if seq > SUBLANES else _pick(t, 256)
        scin, tail = _short_conv(proj, c_sc, scw, w_sc_conv, hist_a, hist_b, seq, rows_sc, min(scw, 256))
        sc_keep = w_sc_conv.shape[0] - 1
        sc_buf = tail.reshape(bn, SUBLANES, scw)[:, SUBLANES - sc_keep:, :]

        merged = _merge(o_n.reshape(t, dnw), scin, proj, c_gd, c_gs, w_dn_b, w_sc_b, tile(256))
        x1, hp, idx, wts, pos, cnt = _mix(merged, x, mod_g, seq, w_o_b, ln1_g, ln1_b, w_router, router_bias,
                                          cnt_in, tile(512 if mod_g.ndim == 3 else 256))
        return dict(x1=x1, hp=hp, idx=idx[:, :TOP_K], wts=wts, pos=pos[:, :TOP_K], cnt=cnt,
                    s=s_new, dn_buf=dn_buf, sc_buf=sc_buf, mod=mod_g, seq=seq, tm_d=tile(256))

    zeros_s = jnp.zeros((bp,) + state_dn_S.shape[1:], f32)
    zeros_dn = jnp.zeros((bp,) + state_dn_conv.shape[1:], f32)
    zeros_sc = jnp.zeros((bp,) + state_sc_conv.shape[1:], f32)
    chunk_p = math.gcd(DN_CHUNK, lp)
    chunk_s = math.gcd(DN_CHUNK, ls)
    gp = group(x_prompt, mod_p, zeros_s, zeros_dn, zeros_sc, chunk_p, _pick(lp, 256), jnp.zeros((1, n_e), f32))
    gs = group(x_sample, mod_s, state_dn_S, state_dn_conv, state_sc_conv, chunk_s, ls, gp["cnt"])

    tm_e = 256
    counts = gs["cnt"][0].astype(i32)
    tiles_per_e = (counts + tm_e - 1) // tm_e
    tile_end = jnp.cumsum(tiles_per_e)
    start_row = (tile_end - tiles_per_e) * tm_e
    n_rows = (tp + ts) * TOP_K
    max_tiles = n_rows // tm_e + n_e
    tile_ids = jnp.arange(max_tiles, dtype=i32)
    tile_expert = jnp.minimum(jnp.sum((tile_end[None, :] <= tile_ids[:, None]).astype(i32), axis=1), n_e - 1)
    n_tiles = tile_end[-1:].astype(i32)
    e_ids = jnp.arange(n_e, dtype=i32)
    nonempty = tiles_per_e > 0
    ordinal = jnp.cumsum(nonempty.astype(i32)) - 1
    later = (e_ids[None, :] > e_ids[:, None]) & nonempty[None, :]
    next_e = jnp.min(jnp.where(later, e_ids[None, :], n_e), axis=1)
    next_e = jnp.where(next_e == n_e, -1, next_e).astype(i32)
    onehot_te = tile_expert[:, None] == e_ids[None, :]
    lookup = lambda table: jnp.sum(jnp.where(onehot_te, table[None, :], 0), axis=1).astype(i32)
    first = (tile_ids == lookup(tile_end - tiles_per_e)).astype(i32)
    slot = lookup(ordinal) % 2
    nxt = lookup(next_e)

    for g in (gp, gs):
        t = g["x1"].shape[0]
        idx_flat = g["idx"].reshape(-1)
        start_flat = jnp.sum(jnp.where(idx_flat[:, None] == e_ids[None, :], start_row[None, :], 0), axis=1)
        g["dest"] = (start_flat + g["pos"].reshape(-1)).astype(i32).reshape(t // g["tm_d"], g["tm_d"] * TOP_K)
    assert gp["tm_d"] == gs["tm_d"]
    zero_flag = ((tile_ids >= n_tiles[0]) | (tile_ids == lookup(tile_end) - 1)).astype(i32)
    xs = _dispatch(zero_flag, jnp.concatenate([gp["dest"], gs["dest"]], axis=0), gp["hp"], gs["hp"],
                   max_tiles * tm_e, tm_e, gp["tm_d"])
    out_sorted = _experts(tile_expert, n_tiles, first, slot, nxt, xs, w_e_gate, w_e_up, w_e_down, tm_e)
    ys = [_combine(g["dest"], out_sorted, g["x1"], g["hp"], g["wts"], g["mod"], g["seq"], ln2_g, ln2_b,
                   ws_gate_b, ws_up_b, ws_down_b, g["tm_d"]) for g in (gp, gs)]

    y_p = ys[0].reshape(bp, lp, d)
    y_s = ys[1].reshape(bs, ls, d)
    return (y_p, y_s, gp["s"], gp["dn_buf"], gp["sc_buf"], gs["s"], gs["dn_buf"], gs["sc_buf"])
```
